```python
import jax, jax.numpy as jnp
from jax import lax
import numpy as np

D_MODEL = 2048
BATCH = 4
SEQ = 8192
DEPTH = 2
DEC_BATCH = 16
DEC_SEQ = 16
PAST_LEN = 4096

CHUNK = 64
RET_HEADS = 8
RET_DK = 128
RET_DV = 128
GLA_HEADS = 4
GLA_DK = 128
GLA_DV = 256
GLA_LOW_RANK = 16
GLA_TAU = 16.0
BRANCH_WIDTH = RET_HEADS * RET_DV
N_BRANCHES = 2
N_GROUPS = 4
EXPERTS_PER_GROUP = 4
N_EXPERTS = N_GROUPS * EXPERTS_PER_GROUP
TOP_K_IN_GROUP = 2
D_EXPERT = 1024
ROPE_BASE = 10000.0
EPS = 1e-6
IN_SPLITS = (RET_HEADS * RET_DK, RET_HEADS * RET_DK, RET_HEADS * RET_DV, RET_HEADS * RET_DV,
             GLA_HEADS * GLA_DK, GLA_HEADS * GLA_DK, GLA_HEADS * GLA_DV, GLA_HEADS * GLA_DV, GLA_LOW_RANK)
IN_DIM = sum(IN_SPLITS)

kernel_name = 'hybrid_retention_gla_hmoe_stream'


def rms_norm(x, w):
    xf = x.astype(jnp.float32)
    y = xf * lax.rsqrt(jnp.mean(xf * xf, axis=-1, keepdims=True) + EPS)
    return (y * w.astype(jnp.float32)).astype(x.dtype)


def head_group_norm(o, w):
    B, T, H, dv = o.shape
    mu = jnp.mean(o, axis=-1, keepdims=True)
    oc = o - mu
    on = oc * lax.rsqrt(jnp.mean(oc * oc, axis=-1, keepdims=True) + EPS)
    return on.reshape(B, T, H * dv) * w.astype(jnp.float32)


def head_rms_norm(o, w):
    B, T, H, dv = o.shape
    on = o * lax.rsqrt(jnp.mean(o * o, axis=-1, keepdims=True) + EPS)
    return on.reshape(B, T, H * dv) * w.astype(jnp.float32)


def rotary(x, pos):
    half = x.shape[-1] // 2
    inv = ROPE_BASE ** (-jnp.arange(half, dtype=jnp.float32) / half)
    ang = pos.astype(jnp.float32)[:, None] * inv[None, :]
    cos = jnp.cos(ang)[None, :, None, :]
    sin = jnp.sin(ang)[None, :, None, :]
    x1, x2 = x[..., :half], x[..., half:]
    return jnp.concatenate([x1 * cos - x2 * sin, x1 * sin + x2 * cos], axis=-1)


def to_chunks(a, C):
    B, T, H, d = a.shape
    return a.reshape(B, T // C, C, H, d).transpose(1, 0, 3, 2, 4)


def from_chunks(o):
    NC, B, H, C, d = o.shape
    return o.transpose(1, 0, 3, 2, 4).reshape(B, NC * C, H, d)


def retention_chunked(q, k, v, s0):
    T = q.shape[1]
    C = min(CHUNK, T)
    log_gamma = jnp.log1p(-jnp.exp2(-5.0 - jnp.arange(RET_HEADS, dtype=jnp.float32)))
    idx = jnp.arange(C, dtype=jnp.float32)
    diff = idx[:, None] - idx[None, :]
    causal = diff >= 0
    decay_mask = jnp.where(causal[None], jnp.exp(log_gamma[:, None, None] * jnp.where(causal, diff, 0.0)[None]), 0.0)
    inner_decay = jnp.exp(log_gamma[:, None] * (idx + 1.0))[None, :, :, None]
    state_decay = jnp.exp(log_gamma[:, None] * (C - 1.0 - idx))[None, :, :, None]
    chunk_decay = jnp.exp(log_gamma * C)[None, :, None, None]

    def step(s, blk):
        qc, kc, vc = blk
        scores = jnp.einsum('bhid,bhjd->bhij', qc, kc) * decay_mask[None]
        o = jnp.einsum('bhij,bhjv->bhiv', scores, vc) + jnp.einsum('bhid,bhdv->bhiv', qc, s) * inner_decay
        s_new = s * chunk_decay + jnp.einsum('bhjd,bhjv->bhdv', kc * state_decay, vc)
        return s_new, o

    s_final, o = lax.scan(step, s0, (to_chunks(q, C), to_chunks(k, C), to_chunks(v, C)))
    return from_chunks(o), s_final


def gla_chunked(q, k, v, log_alpha, s0):
    T = q.shape[1]
    C = min(CHUNK, T)
    causal = (jnp.arange(C)[:, None] >= jnp.arange(C)[None, :])[None, None, :, :, None]

    def step(s, blk):
        qc, kc, vc, gc = blk
        b = jnp.cumsum(gc, axis=2)
        rel = jnp.exp(jnp.where(causal, b[:, :, :, None, :] - b[:, :, None, :, :], -jnp.inf))
        scores = jnp.sum(qc[:, :, :, None, :] * kc[:, :, None, :, :] * rel, axis=-1)
        o = jnp.einsum('bhij,bhjv->bhiv', scores, vc) + jnp.einsum('bhid,bhdv->bhiv', qc * jnp.exp(b), s)
        b_last = b[:, :, -1:, :]
        s_new = jnp.exp(b_last[:, :, 0, :])[..., None] * s + jnp.einsum('bhjd,bhjv->bhdv', kc * jnp.exp(b_last - b), vc)
        return s_new, o

    s_final, o = lax.scan(step, s0, (to_chunks(q, C), to_chunks(k, C), to_chunks(v, C), to_chunks(log_alpha, C)))
    return from_chunks(o), s_final


def token_mixers(h, pos, s_ret0, s_gla0, w_in, w_gla_up, b_gla, ret_norm_w, gla_norm_w,
                 w_branch, w_merge, b_merge, w_o):
    B, T, _ = h.shape
    proj = (h @ w_in).astype(jnp.float32)
    offs = [int(o) for o in np.cumsum(IN_SPLITS)[:-1]]
    rq, rk, rv, rg, gq, gk, gv, gg, glr = jnp.split(proj, offs, axis=-1)
    q = rotary(rq.reshape(B, T, RET_HEADS, RET_DK), pos)
    k = rotary(rk.reshape(B, T, RET_HEADS, RET_DK), pos) * (RET_DK ** -0.5)
    v = rv.reshape(B, T, RET_HEADS, RET_DV)
    o_ret, s_ret = retention_chunked(q, k, v, s_ret0.astype(jnp.float32))
    o_ret = jax.nn.silu(rg) * head_group_norm(o_ret, ret_norm_w)
    log_alpha = jax.nn.log_sigmoid(glr @ w_gla_up.astype(jnp.float32) + b_gla.astype(jnp.float32)) / GLA_TAU
    q = gq.reshape(B, T, GLA_HEADS, GLA_DK) * (GLA_DK ** -0.5)
    k = gk.reshape(B, T, GLA_HEADS, GLA_DK)
    v = gv.reshape(B, T, GLA_HEADS, GLA_DV)
    o_gla, s_gla = gla_chunked(q, k, v, log_alpha.reshape(B, T, GLA_HEADS, GLA_DK), s_gla0.astype(jnp.float32))
    o_gla = jax.nn.silu(gg) * head_rms_norm(o_gla, gla_norm_w)
    branches = jnp.stack([o_ret, o_gla], axis=2).astype(h.dtype)
    y = jnp.einsum('btnw,nwd->btnd', branches, w_branch).astype(jnp.float32)
    gates = jax.nn.sigmoid((h @ w_merge + b_merge).astype(jnp.float32)).reshape(B, T, N_BRANCHES, D_MODEL)
    merged = jnp.sum(gates * y, axis=2).astype(h.dtype)
    return merged @ w_o, s_ret, s_gla


def hier_moe(h, w_rg, b_rg, w_re, b_re, w_g, w_u, w_d):
    B, T, D = h.shape
    hf = h.reshape(B * T, D)
    group_probs = jax.nn.softmax((hf @ w_rg + b_rg).astype(jnp.float32), axis=-1)
    g_idx = jnp.argmax(group_probs, axis=-1)
    p_group = jnp.take_along_axis(group_probs, g_idx[:, None], axis=-1)
    exp_logits = (hf @ w_re + b_re).astype(jnp.float32).reshape(-1, N_GROUPS, EXPERTS_PER_GROUP)
    in_group = jnp.take_along_axis(exp_logits, g_idx[:, None, None], axis=1)[:, 0]
    top_p, top_i = lax.top_k(jax.nn.softmax(in_group, axis=-1), TOP_K_IN_GROUP)
    top_p = top_p / jnp.sum(top_p, axis=-1, keepdims=True)
    expert_id = g_idx[:, None] * EXPERTS_PER_GROUP + top_i
    gates = jnp.einsum('nk,nke->ne', p_group * top_p, jax.nn.one_hot(expert_id, N_EXPERTS, dtype=jnp.float32))
    out = jnp.zeros(hf.shape, jnp.float32)
    for e in range(N_EXPERTS):
        a = jax.nn.silu(hf @ w_g[e]) * (hf @ w_u[e])
        out = out + gates[:, e:e + 1] * (a @ w_d[e]).astype(jnp.float32)
    return out.astype(h.dtype).reshape(B, T, D)


def trunk(x, c, pos, s_ret_in, s_gla_in, w_ada, b_ada, norm_mix_w, norm_ffn_w, w_in, w_gla_up, b_gla,
          ret_norm_w, gla_norm_w, w_branch, w_merge, b_merge, w_o, w_router_group, b_router_group,
          w_router_expert, b_router_expert, w_exp_gate, w_exp_up, w_exp_down, final_norm_w):
    ret_states, gla_states = [], []
    for l in range(DEPTH):
        mod = (jax.nn.silu(c) @ w_ada[l] + b_ada[l])[:, None, :]
        sh1, sc1, g1, sh2, sc2, g2 = jnp.split(mod, 6, axis=-1)
        h = rms_norm(x, norm_mix_w[l]) * (1.0 + sc1) + sh1
        y, s_r, s_g = token_mixers(h, pos, s_ret_in[l], s_gla_in[l], w_in[l], w_gla_up[l], b_gla[l],
                                   ret_norm_w[l], gla_norm_w[l], w_branch[l], w_merge[l], b_merge[l], w_o[l])
        x = x + g1 * y
        h = rms_norm(x, norm_ffn_w[l]) * (1.0 + sc2) + sh2
        x = x + g2 * hier_moe(h, w_router_group[l], b_router_group[l], w_router_expert[l], b_router_expert[l],
                              w_exp_gate[l], w_exp_up[l], w_exp_down[l])
        ret_states.append(s_r)
        gla_states.append(s_g)
    return rms_norm(x, final_norm_w), jnp.stack(ret_states), jnp.stack(gla_states)


def setup_inputs(seed: int = 0) -> dict:
    key = jax.random.key(seed)
    ks = jax.random.split(key, 32)
    D = D_MODEL

    def nrm(k, shape, scale):
        return jax.random.normal(k, shape, jnp.float32) * scale

    return {
        'x_prompt': nrm(ks[0], (BATCH, SEQ, D), 1.0),
        'x_sample': nrm(ks[1], (DEC_BATCH, DEC_SEQ, D), 1.0),
        'state_ret': nrm(ks[2], (DEPTH, DEC_BATCH, RET_HEADS, RET_DK, RET_DV), 0.5),
        'state_gla': nrm(ks[3], (DEPTH, DEC_BATCH, GLA_HEADS, GLA_DK, GLA_DV), 0.5),
        'c_prompt': nrm(ks[4], (BATCH, D), 1.0),
        'c_sample': nrm(ks[5], (DEC_BATCH, D), 1.0),
        'w_ada': nrm(ks[6], (DEPTH, D, 6 * D), 0.5 * D ** -0.5),
        'b_ada': nrm(ks[7], (DEPTH, 6 * D), 0.02),
        'norm_mix_w': 1.0 + nrm(ks[8], (DEPTH, D), 0.02),
        'norm_ffn_w': 1.0 + nrm(ks[9], (DEPTH, D), 0.02),
        'w_in': nrm(ks[10], (DEPTH, D, IN_DIM), D ** -0.5),
        'w_gla_up': nrm(ks[11], (DEPTH, GLA_LOW_RANK, GLA_HEADS * GLA_DK), GLA_LOW_RANK ** -0.5),
        'b_gla': nrm(ks[12], (DEPTH, GLA_HEADS * GLA_DK), 0.1),
        'ret_norm_w': 1.0 + nrm(ks[13], (DEPTH, BRANCH_WIDTH), 0.02),
        'gla_norm_w': 1.0 + nrm(ks[14], (DEPTH, BRANCH_WIDTH), 0.02),
        'w_branch': nrm(ks[15], (DEPTH, N_BRANCHES, BRANCH_WIDTH, D), BRANCH_WIDTH ** -0.5),
        'w_merge': nrm(ks[16], (DEPTH, D, N_BRANCHES * D), D ** -0.5),
        'b_merge': nrm(ks[17], (DEPTH, N_BRANCHES * D), 0.02),
        'w_o': nrm(ks[18], (DEPTH, D, D), D ** -0.5),
        'w_router_group': nrm(ks[19], (DEPTH, D, N_GROUPS), D ** -0.5),
        'b_router_group': nrm(ks[20], (DEPTH, N_GROUPS), 0.01),
        'w_router_expert': nrm(ks[21], (DEPTH, D, N_EXPERTS), D ** -0.5),
        'b_router_expert': nrm(ks[22], (DEPTH, N_EXPERTS), 0.01),
        'w_exp_gate': nrm(ks[23], (DEPTH, N_EXPERTS, D, D_EXPERT), D ** -0.5),
        'w_exp_up': nrm(ks[24], (DEPTH, N_EXPERTS, D, D_EXPERT), D ** -0.5),
        'w_exp_down': nrm(ks[25], (DEPTH, N_EXPERTS, D_EXPERT, D), D_EXPERT ** -0.5),
        'final_norm_w': 1.0 + nrm(ks[26], (D,), 0.02),
    }


def reference(x_prompt, x_sample, state_ret, state_gla, c_prompt, c_sample, w_ada, b_ada, norm_mix_w,
              norm_ffn_w, w_in, w_gla_up, b_gla, ret_norm_w, gla_norm_w, w_branch, w_merge, b_merge, w_o,
              w_router_group, b_router_group, w_router_expert, b_router_expert, w_exp_gate, w_exp_up,
              w_exp_down, final_norm_w):
    params = (w_ada, b_ada, norm_mix_w, norm_ffn_w, w_in, w_gla_up, b_gla, ret_norm_w, gla_norm_w,
              w_branch, w_merge, b_merge, w_o, w_router_group, b_router_group, w_router_expert,
              b_router_expert, w_exp_gate, w_exp_up, w_exp_down, final_norm_w)
    nb = x_prompt.shape[0]
    zero_ret = jnp.zeros((DEPTH, nb, RET_HEADS, RET_DK, RET_DV), jnp.float32)
    zero_gla = jnp.zeros((DEPTH, nb, GLA_HEADS, GLA_DK, GLA_DV), jnp.float32)
    pos_prompt = jnp.arange(x_prompt.shape[1])
    pos_sample = PAST_LEN + jnp.arange(x_sample.shape[1])
    y_prompt, ret_prompt, gla_prompt = trunk(x_prompt, c_prompt, pos_prompt, zero_ret, zero_gla, *params)
    y_sample, ret_sample, gla_sample = trunk(x_sample, c_sample, pos_sample, state_ret, state_gla, *params)
    return (y_prompt, y_sample, ret_prompt, gla_prompt, ret_sample, gla_sample)
```

```python
import functools

import jax
import jax.numpy as jnp
from jax import lax
from jax.experimental import pallas as pl
from jax.experimental.pallas import tpu as pltpu

F32 = jnp.float32
BF16 = jnp.bfloat16
U32 = jnp.uint32
I32 = jnp.int32

PAST_LEN = 4096
GLA_TAU = 16.0
ROPE_BASE = 10000.0
EPS = 1e-6
RET_DECAY_LOG2_BASE = -5.0
TOP_K = 2

LANES = 128
SUBLANES = 8
V7X_VMEM_BYTES = 64 * 1024 * 1024
VMEM_LIMIT = V7X_VMEM_BYTES - 8 * 1024 * 1024

ROWS_MATMUL = 1024
ROWS_OUT_PROJ = 512
ROWS_NORM = 512
ROWS_COMBINE = 256
ROWS_DISPATCH = 512
ROWS_RANK = 512
ROWS_FFN = 512
CHUNK_RET = 256
CHUNK_GLA = 256


def _params(*sem):
    return pltpu.CompilerParams(dimension_semantics=sem, vmem_limit_bytes=VMEM_LIMIT)


def _tile(n, pref, align):
    if n <= pref:
        return n
    t = (pref // align) * align
    while t >= align:
        if n % t == 0:
            return t
        t -= align
    return n


def _seq_tiles(s, t, rows):
    if t >= rows:
        return 1, _tile(t, rows, 2 * SUBLANES)
    return _tile(s, max(rows // t, 1), 1), t


def _dot(a, b):
    return jnp.dot(a, b, preferred_element_type=F32)


def _dot_nt(a, b):
    return lax.dot_general(a, b, (((1,), (1,)), ((), ())), preferred_element_type=F32)


def _dot_tn(a, b):
    return lax.dot_general(a, b, (((0,), (0,)), ((), ())), preferred_element_type=F32)


def _sigmoid(x):
    return 1.0 / (1.0 + jnp.exp(-x))


def _silu(x):
    return x * _sigmoid(x)


def _log_sigmoid(z):
    return jnp.minimum(z, 0.0) - jnp.log(1.0 + jnp.exp(-jnp.abs(z)))


def _split_bf16(x):
    hi = x.astype(BF16)
    lo = (x - hi.astype(F32)).astype(BF16)
    return hi, lo


def _dot_split(a, w_hi, w_lo):
    a_hi, a_lo = _split_bf16(a)
    return _dot(a_hi, w_hi) + _dot(a_lo, w_hi) + _dot(a_hi, w_lo)


def _ada_kernel(c_ref, w_ref, b_ref, o_ref):
    a = _silu(c_ref[...]).astype(BF16)
    o_ref[...] = _dot(a, w_ref[...].astype(BF16)) + b_ref[...]


def _ada(c_all, w_ada, b_ada):
    depth, d, n6 = w_ada.shape
    r = c_all.shape[0]
    tn = _tile(n6, 1024, LANES)
    return pl.pallas_call(
        _ada_kernel,
        out_shape=jax.ShapeDtypeStruct((depth, r, n6), F32),
        grid=(depth, n6 // tn),
        in_specs=[
            pl.BlockSpec((r, d), lambda l, j: (0, 0)),
            pl.BlockSpec((None, d, tn), lambda l, j: (l, 0, j)),
            pl.BlockSpec((None, 1, tn), lambda l, j: (l, 0, j)),
        ],
        out_specs=pl.BlockSpec((None, r, tn), lambda l, j: (l, 0, j)),
        compiler_params=_params("arbitrary", "arbitrary"),
        name="ada_mod",
    )(c_all, w_ada, b_ada.reshape(depth, 1, n6))


def _norm_kernel(x_ref, w_ref, sc_ref, sh_ref, o_ref):
    x = x_ref[...]
    y = x * lax.rsqrt(jnp.mean(x * x, axis=-1, keepdims=True) + EPS) * w_ref[...]
    o_ref[...] = (y * (1.0 + sc_ref[...]) + sh_ref[...]).astype(o_ref.dtype)


def _norm(x, w, sc, sh):
    s, t, d = x.shape
    bs, tt = _seq_tiles(s, t, ROWS_NORM)
    return pl.pallas_call(
        _norm_kernel,
        out_shape=jax.ShapeDtypeStruct((s, t, d), BF16),
        grid=(s // bs, t // tt),
        in_specs=[
            pl.BlockSpec((bs, tt, d), lambda i, j: (i, j, 0)),
            pl.BlockSpec((1, d), lambda i, j: (0, 0)),
            pl.BlockSpec((bs, 1, d), lambda i, j: (i, 0, 0)),
            pl.BlockSpec((bs, 1, d), lambda i, j: (i, 0, 0)),
        ],
        out_specs=pl.BlockSpec((bs, tt, d), lambda i, j: (i, j, 0)),
        compiler_params=_params("arbitrary", "arbitrary"),
        name="mod_norm",
    )(x, w.reshape(1, d), sc, sh)


def _proj_ret_kernel(h_ref, w_ref, cos_ref, sin_ref, o_ref, *, heads, dk, kscale):
    j = pl.program_id(1)
    acc = _dot(h_ref[...], w_ref[...])

    @pl.when(j < 2)
    def _():
        cos = cos_ref[...]
        sin = sin_ref[...]
        mult = jnp.where(j == 1, kscale, 1.0).astype(F32)
        for hd in range(heads):
            a = acc[:, hd * dk:(hd + 1) * dk]
            r = a * cos + pltpu.roll(a, dk // 2, 1) * sin
            o_ref[:, hd * dk:(hd + 1) * dk] = (r * mult).astype(o_ref.dtype)

    @pl.when(j == 2)
    def _():
        o_ref[...] = acc.astype(o_ref.dtype)

    @pl.when(j == 3)
    def _():
        o_ref[...] = _silu(acc).astype(o_ref.dtype)


def _proj_ret(h2d, w, cos, sin, *, heads, dk, tm):
    n, d = h2d.shape
    tn = heads * dk
    assert w.shape == (d, 4 * tn) and cos.shape[0] % tm == 0
    nt = cos.shape[0] // tm
    return pl.pallas_call(
        functools.partial(_proj_ret_kernel, heads=heads, dk=dk, kscale=dk ** -0.5),
        out_shape=jax.ShapeDtypeStruct((n, 4 * tn), BF16),
        grid=(n // tm, 4),
        in_specs=[
            pl.BlockSpec((tm, d), lambda i, j: (i, 0)),
            pl.BlockSpec((d, tn), lambda i, j: (0, j)),
            pl.BlockSpec((tm, dk), lambda i, j: (i % nt, 0)),
            pl.BlockSpec((tm, dk), lambda i, j: (i % nt, 0)),
        ],
        out_specs=pl.BlockSpec((tm, tn), lambda i, j: (i, j)),
        compiler_params=_params("arbitrary", "arbitrary"),
        name="proj_ret",
    )(h2d, w, cos, sin)


def _proj_gla_kernel(h_ref, w_ref, wup_hi_ref, wup_lo_ref, bup_ref, qk_ref, vg_ref, la_ref, *, nv, qscale):
    j = pl.program_id(1)
    acc = _dot(h_ref[...], w_ref[...])

    @pl.when(j == 0)
    def _():
        qk_ref[...] = acc * qscale

    @pl.when(j == 1)
    def _():
        qk_ref[...] = acc

    @pl.when((j >= 2) & (j < 2 + nv))
    def _():
        vg_ref[...] = acc.astype(vg_ref.dtype)

    @pl.when((j >= 2 + nv) & (j < 2 + 2 * nv))
    def _():
        vg_ref[...] = _silu(acc).astype(vg_ref.dtype)

    @pl.when(j == 2 + 2 * nv)
    def _():
        z = _dot_split(acc[:, :LANES], wup_hi_ref[...], wup_lo_ref[...])
        la_ref[...] = _log_sigmoid(z + bup_ref[...]) * (1.0 / GLA_TAU)


def _proj_gla(h2d, w, wup_hi, wup_lo, bup, *, hdk, hdv, dk, tm):
    n, d = h2d.shape
    tn = hdk
    nv = hdv // tn
    nblk = 3 + 2 * nv
    assert hdv % tn == 0 and w.shape == (d, nblk * tn) and wup_hi.shape == (LANES, hdk)
    return pl.pallas_call(
        functools.partial(_proj_gla_kernel, nv=nv, qscale=dk ** -0.5),
        out_shape=(
            jax.ShapeDtypeStruct((n, 2 * hdk), F32),
            jax.ShapeDtypeStruct((n, 2 * hdv), BF16),
            jax.ShapeDtypeStruct((n, hdk), F32),
        ),
        grid=(n // tm, nblk),
        in_specs=[
            pl.BlockSpec((tm, d), lambda i, j: (i, 0)),
            pl.BlockSpec((d, tn), lambda i, j: (0, j)),
            pl.BlockSpec((LANES, hdk), lambda i, j: (0, 0)),
            pl.BlockSpec((LANES, hdk), lambda i, j: (0, 0)),
            pl.BlockSpec((1, hdk), lambda i, j: (0, 0)),
        ],
        out_specs=(
            pl.BlockSpec((tm, tn), lambda i, j: (i, jnp.minimum(j, 1))),
            pl.BlockSpec((tm, tn), lambda i, j: (i, jnp.clip(j - 2, 0, 2 * nv - 1))),
            pl.BlockSpec((tm, hdk), lambda i, j: (i, 0)),
        ),
        compiler_params=_params("arbitrary", "arbitrary"),
        name="proj_gla",
    )(h2d, w, wup_hi, wup_lo, bup)


def _ret_kernel(q_ref, k_ref, v_ref, g_ref, s0_ref, dm_ref, ind_ref, sd_ref, cd_ref, w_ref,
                o_ref, so_ref, st_ref, *, heads, dk, dv):
    c = pl.program_id(1)

    @pl.when(c == 0)
    def _():
        st_ref[...] = s0_ref[0]

    for hd in range(heads):
        q = q_ref[0, :, hd * dk:(hd + 1) * dk]
        k = k_ref[0, :, hd * dk:(hd + 1) * dk]
        v = v_ref[0, :, hd * dv:(hd + 1) * dv]
        s = st_ref[hd]
        p = (_dot_nt(q, k) * dm_ref[hd]).astype(BF16)
        o = _dot(p, v) + _dot(q, s.astype(BF16)) * ind_ref[hd]
        ks = (k.astype(F32) * sd_ref[hd]).astype(BF16)
        st_ref[hd] = s * cd_ref[hd] + _dot_tn(ks, v)
        oc = o - jnp.mean(o, axis=-1, keepdims=True)
        on = oc * lax.rsqrt(jnp.mean(oc * oc, axis=-1, keepdims=True) + EPS)
        gate = g_ref[0, :, hd * dv:(hd + 1) * dv].astype(F32)
        o_ref[0, :, hd * dv:(hd + 1) * dv] = (gate * (on * w_ref[:, hd * dv:(hd + 1) * dv])).astype(o_ref.dtype)

    @pl.when(c == pl.num_programs(1) - 1)
    def _():
        so_ref[0] = st_ref[...]


def _ret_tables(heads, c, dk, dv):
    log_gamma = jnp.log1p(-jnp.exp2(RET_DECAY_LOG2_BASE - jnp.arange(heads, dtype=F32)))
    idx = jnp.arange(c, dtype=F32)
    diff = idx[:, None] - idx[None, :]
    causal = diff >= 0
    dmask = jnp.where(causal[None], jnp.exp(log_gamma[:, None, None] * jnp.where(causal, diff, 0.0)[None]), 0.0)
    inner = jnp.exp(log_gamma[:, None] * (idx + 1.0))
    sdecay = jnp.exp(log_gamma[:, None] * (c - 1.0 - idx))
    cdecay = jnp.exp(log_gamma * c)
    return (dmask,
            jnp.broadcast_to(inner[:, :, None], (heads, c, dv)),
            jnp.broadcast_to(sdecay[:, :, None], (heads, c, dk)),
            jnp.broadcast_to(cdecay[:, None, None], (heads, 1, dv)))


def _retention(qkvg, s0, norm_w, *, heads, dk, dv):
    s, t, _ = qkvg.shape
    assert dk == dv
    c = _tile(t, CHUNK_RET, 2 * SUBLANES)
    w = heads * dk
    dm, ind, sd, cd = _ret_tables(heads, c, dk, dv)
    const3 = lambda b, i: (0, 0, 0)
    return pl.pallas_call(
        functools.partial(_ret_kernel, heads=heads, dk=dk, dv=dv),
        out_shape=(jax.ShapeDtypeStruct((s, t, w), BF16), jax.ShapeDtypeStruct(s0.shape, F32)),
        grid=(s, t // c),
        in_specs=[
            pl.BlockSpec((1, c, w), lambda b, i: (b, i, 0)),
            pl.BlockSpec((1, c, w), lambda b, i: (b, i, 1)),
            pl.BlockSpec((1, c, w), lambda b, i: (b, i, 2)),
            pl.BlockSpec((1, c, w), lambda b, i: (b, i, 3)),
            pl.BlockSpec((1, heads, dk, dv), lambda b, i: (b, 0, 0, 0)),
            pl.BlockSpec((heads, c, c), const3),
            pl.BlockSpec((heads, c, dv), const3),
            pl.BlockSpec((heads, c, dk), const3),
            pl.BlockSpec((heads, 1, dv), const3),
            pl.BlockSpec((1, w), lambda b, i: (0, 0)),
        ],
        out_specs=(
            pl.BlockSpec((1, c, w), lambda b, i: (b, i, 0)),
            pl.BlockSpec((1, heads, dk, dv), lambda b, i: (b, 0, 0, 0)),
        ),
        scratch_shapes=[pltpu.VMEM((heads, dk, dv), F32)],
        compiler_params=_params("arbitrary", "arbitrary"),
        name="retention",
    )(qkvg, qkvg, qkvg, qkvg, s0, dm, ind, sd, cd, norm_w.reshape(1, w))


def _gla_kernel(q_ref, k_ref, la_ref, v_ref, g_ref, s0_ref, w_ref, o_ref, so_ref, st_ref, *, heads, dk, dv, c):
    ci = pl.program_id(1)

    @pl.when(ci == 0)
    def _():
        st_ref[...] = s0_ref[0]

    row = lax.broadcasted_iota(I32, (c, dk), 0)
    ri = lax.broadcasted_iota(I32, (c, c), 0)
    cj = lax.broadcasted_iota(I32, (c, c), 1)
    for hd in range(heads):
        q = q_ref[0, :, hd * dk:(hd + 1) * dk]
        k = k_ref[0, :, hd * dk:(hd + 1) * dk]
        v = v_ref[0, :, hd * dv:(hd + 1) * dv]
        b = la_ref[0, :, hd * dk:(hd + 1) * dk]
        sh = 1
        while sh < c:
            b = b + jnp.where(row >= sh, pltpu.roll(b, sh, 0), 0.0)
            sh *= 2
        scores = jnp.where(ri == cj, _dot_nt(q.astype(BF16), k.astype(BF16)), 0.0)
        first = b
        half = 1
        while half < c:
            upper = (row & (2 * half - 1)) >= half
            mid = jnp.where(upper, first, pltpu.roll(first, c - half, 0))
            ql = jnp.where(upper, q * jnp.exp(jnp.minimum(b - mid, 0.0)), 0.0).astype(BF16)
            kl = jnp.where(upper, 0.0, k * jnp.exp(jnp.minimum(mid - b, 0.0))).astype(BF16)
            same = (ri & -(2 * half)) == (cj & -(2 * half))
            scores = scores + jnp.where(same, _dot_nt(ql, kl), 0.0)
            first = jnp.where(upper, pltpu.roll(first, half, 0), first)
            half *= 2
        st = st_ref[hd]
        o = _dot(scores.astype(BF16), v) + _dot_nt((q * jnp.exp(b)).astype(BF16), st.astype(BF16))
        b_last = b[c - 1:c, :]
        kd = (k * jnp.exp(b_last - b)).astype(BF16)
        st_ref[hd] = st * jnp.exp(b_last) + _dot_tn(v, kd)
        on = o * lax.rsqrt(jnp.mean(o * o, axis=-1, keepdims=True) + EPS)
        gate = g_ref[0, :, hd * dv:(hd + 1) * dv].astype(F32)
        o_ref[0, :, hd * dv:(hd + 1) * dv] = (gate * (on * w_ref[:, hd * dv:(hd + 1) * dv])).astype(o_ref.dtype)

    @pl.when(ci == pl.num_programs(1) - 1)
    def _():
        so_ref[0] = st_ref[...]


def _gla(qk, vg, la, s0t, norm_w, *, heads, dk, dv):
    s, t, _ = qk.shape
    c = _tile(t, CHUNK_GLA, 2 * SUBLANES)
    assert c & (c - 1) == 0, "chunk length must be a power of two"
    wk, wv = heads * dk, heads * dv
    return pl.pallas_call(
        functools.partial(_gla_kernel, heads=heads, dk=dk, dv=dv, c=c),
        out_shape=(jax.ShapeDtypeStruct((s, t, wv), BF16), jax.ShapeDtypeStruct(s0t.shape, F32)),
        grid=(s, t // c),
        in_specs=[
            pl.BlockSpec((1, c, wk), lambda b, i: (b, i, 0)),
            pl.BlockSpec((1, c, wk), lambda b, i: (b, i, 1)),
            pl.BlockSpec((1, c, wk), lambda b, i: (b, i, 0)),
            pl.BlockSpec((1, c, wv), lambda b, i: (b, i, 0)),
            pl.BlockSpec((1, c, wv), lambda b, i: (b, i, 1)),
            pl.BlockSpec((1, heads, dv, dk), lambda b, i: (b, 0, 0, 0)),
            pl.BlockSpec((1, wv), lambda b, i: (0, 0)),
        ],
        out_specs=(
            pl.BlockSpec((1, c, wv), lambda b, i: (b, i, 0)),
            pl.BlockSpec((1, heads, dv, dk), lambda b, i: (b, 0, 0, 0)),
        ),
        scratch_shapes=[pltpu.VMEM((heads, dv, dk), F32)],
        compiler_params=_params("arbitrary", "arbitrary"),
        name="gla",
    )(qk, qk, la, vg, vg, s0t, norm_w.reshape(1, wv))


def _merge_kernel(br_ref, bg_ref, h_ref, wr_ref, wg_ref, wm0_ref, wm1_ref, bm0_ref, bm1_ref, o_ref):
    h = h_ref[...]
    g0 = _sigmoid(_dot(h, wm0_ref[...]) + bm0_ref[...])
    g1 = _sigmoid(_dot(h, wm1_ref[...]) + bm1_ref[...])
    y = g0 * _dot(br_ref[...], wr_ref[...]) + g1 * _dot(bg_ref[...], wg_ref[...])
    o_ref[...] = y.astype(o_ref.dtype)


def _merge(o_ret, o_gla, h2d, w_branch, w_merge, b_merge, *, tm):
    n, d = h2d.shape
    wdt = o_ret.shape[1]
    tn = _tile(d, 512, LANES)
    nj = d // tn
    b2 = b_merge.reshape(1, 2 * d)
    return pl.pallas_call(
        _merge_kernel,
        out_shape=jax.ShapeDtypeStruct((n, d), BF16),
        grid=(n // tm, nj),
        in_specs=[
            pl.BlockSpec((tm, wdt), lambda i, j: (i, 0)),
            pl.BlockSpec((tm, wdt), lambda i, j: (i, 0)),
            pl.BlockSpec((tm, d), lambda i, j: (i, 0)),
            pl.BlockSpec((None, wdt, tn), lambda i, j: (0, 0, j)),
            pl.BlockSpec((None, wdt, tn), lambda i, j: (1, 0, j)),
            pl.BlockSpec((d, tn), lambda i, j: (0, j)),
            pl.BlockSpec((d, tn), lambda i, j: (0, j + nj)),
            pl.BlockSpec((1, tn), lambda i, j: (0, j)),
            pl.BlockSpec((1, tn), lambda i, j: (0, j + nj)),
        ],
        out_specs=pl.BlockSpec((tm, tn), lambda i, j: (i, j)),
        compiler_params=_params("arbitrary", "arbitrary"),
        name="branch_merge",
    )(o_ret, o_gla, h2d, w_branch, w_branch, w_merge, w_merge, b2, b2)


def _pack_bf16_pairs(h):
    half = h.shape[-1] // 2
    a = lax.bitcast_convert_type(h[:, :half].astype(BF16).astype(F32), U32)
    b = lax.bitcast_convert_type(h[:, half:].astype(BF16).astype(F32), U32)
    return a | (b >> 16)


def _unpack_bf16_pairs(w):
    a = lax.bitcast_convert_type(w & jnp.uint32(0xFFFF0000), F32)
    b = lax.bitcast_convert_type(w << 16, F32)
    return jnp.concatenate([a, b], axis=-1).astype(BF16)


def _route(logits, n_groups, per_group):
    lane = lax.broadcasted_iota(I32, logits.shape, 1).astype(F32)
    neg = jnp.float32(-jnp.inf)

    def first_max(mask):
        m = jnp.max(jnp.where(mask, logits, neg), axis=-1, keepdims=True)
        idx = jnp.min(jnp.where(mask & (logits == m), lane, float(LANES)), axis=-1, keepdims=True)
        return m, idx

    gmask = lane < n_groups
    gmax, gidx = first_max(gmask)
    p_group = 1.0 / jnp.sum(jnp.where(gmask, jnp.exp(logits - gmax), 0.0), axis=-1, keepdims=True)
    lo = n_groups + gidx * per_group
    emask = (lane >= lo) & (lane < lo + per_group)
    m1, i1 = first_max(emask)
    m2, i2 = first_max(emask & (lane != i1))
    w1 = 1.0 / (1.0 + jnp.exp(m2 - m1))
    w2 = 1.0 - w1
    return jnp.where(lane == 0.0, i1 - n_groups, jnp.where(lane == 1.0, i2 - n_groups, jnp.where(
        lane == 2.0, p_group * w1, jnp.where(lane == 3.0, p_group * w2, 0.0))))


def _out_proj_kernel(m_ref, x_ref, g1_ref, sc_ref, sh_ref, nw_ref, wo_ref, wr_hi_ref, wr_lo_ref, br_ref,
                     x1_ref, hp_ref, rt_ref, *, n_groups, per_group):
    bs, tt, d = x_ref.shape
    y = _dot(m_ref[...].reshape(bs * tt, d), wo_ref[...]).reshape(bs, tt, d)
    x1 = x_ref[...] + g1_ref[...] * y
    x1_ref[...] = x1
    hn = x1 * lax.rsqrt(jnp.mean(x1 * x1, axis=-1, keepdims=True) + EPS) * nw_ref[...]
    h = (hn * (1.0 + sc_ref[...]) + sh_ref[...]).reshape(bs * tt, d)
    hp_ref[...] = _pack_bf16_pairs(h).reshape(bs, tt, d // 2)
    logits = _dot_split(h, wr_hi_ref[...], wr_lo_ref[...]) + br_ref[...]
    rt_ref[...] = _route(logits, n_groups, per_group).reshape(bs, tt, LANES)


def _out_proj(merged, x, g1, sc2, sh2, norm_w, w_o, wr_hi, wr_lo, b_r, *, n_groups, per_group):
    s, t, d = x.shape
    bs, tt = _seq_tiles(s, t, ROWS_OUT_PROJ)
    tok = lambda i, j: (i, j, 0)
    seq = lambda i, j: (i, 0, 0)
    const = lambda i, j: (0, 0)
    return pl.pallas_call(
        functools.partial(_out_proj_kernel, n_groups=n_groups, per_group=per_group),
        out_shape=(
            jax.ShapeDtypeStruct((s, t, d), F32),
            jax.ShapeDtypeStruct((s, t, d // 2), U32),
            jax.ShapeDtypeStruct((s, t, LANES), F32),
        ),
        grid=(s // bs, t // tt),
        in_specs=[
            pl.BlockSpec((bs, tt, d), tok),
            pl.BlockSpec((bs, tt, d), tok),
            pl.BlockSpec((bs, 1, d), seq),
            pl.BlockSpec((bs, 1, d), seq),
            pl.BlockSpec((bs, 1, d), seq),
            pl.BlockSpec((1, d), const),
            pl.BlockSpec((d, d), const),
            pl.BlockSpec((d, LANES), const),
            pl.BlockSpec((d, LANES), const),
            pl.BlockSpec((1, LANES), const),
        ],
        out_specs=(
            pl.BlockSpec((bs, tt, d), tok),
            pl.BlockSpec((bs, tt, d // 2), tok),
            pl.BlockSpec((bs, tt, LANES), tok),
        ),
        compiler_params=_params("arbitrary", "arbitrary"),
        name="out_proj_router",
    )(merged, x, g1, sc2, sh2, norm_w.reshape(1, d), w_o, wr_hi, wr_lo, b_r)


def _rank_kernel(rt_ref, rank_ref, cnt_ref, carry_ref):
    i = pl.program_id(0)

    @pl.when(i == 0)
    def _():
        carry_ref[...] = jnp.zeros_like(carry_ref)

    r = rt_ref[...]
    tr = r.shape[0]
    lane = lax.broadcasted_iota(I32, r.shape, 1).astype(F32)
    a1 = lane == r[:, 0:1]
    a2 = lane == r[:, 1:2]
    hit = jnp.where(a1 | a2, 1.0, 0.0)
    ri = lax.broadcasted_iota(I32, (tr, tr), 0)
    cj = lax.broadcasted_iota(I32, (tr, tr), 1)
    before = _dot(jnp.where(ri > cj, 1.0, 0.0).astype(BF16), hit.astype(BF16)) + carry_ref[...]
    k1 = jnp.sum(jnp.where(a1, before, 0.0), axis=-1, keepdims=True)
    k2 = jnp.sum(jnp.where(a2, before, 0.0), axis=-1, keepdims=True)
    rank_ref[...] = jnp.where(lane == 0.0, k1, jnp.where(lane == 1.0, k2, 0.0))
    carry_ref[...] += jnp.sum(hit, axis=0, keepdims=True)
    cnt_ref[...] = carry_ref[...]


def _rank(route2d):
    n = route2d.shape[0]
    tr = _tile(n, ROWS_RANK, SUBLANES)
    return pl.pallas_call(
        _rank_kernel,
        out_shape=(jax.ShapeDtypeStruct((n, LANES), F32), jax.ShapeDtypeStruct((1, LANES), F32)),
        grid=(n // tr,),
        in_specs=[pl.BlockSpec((tr, LANES), lambda i: (i, 0))],
        out_specs=(pl.BlockSpec((tr, LANES), lambda i: (i, 0)), pl.BlockSpec((1, LANES), lambda i: (0, 0))),
        scratch_shapes=[pltpu.VMEM((1, LANES), F32)],
        compiler_params=_params("arbitrary"),
        name="moe_rank",
    )(route2d)


def _row_copy(src_ref, src_row, dst_ref, dst_row, sem):
    return pltpu.make_async_copy(src_ref.at[pl.ds(src_row, 1)], dst_ref.at[pl.ds(dst_row, 1)], sem)


def _dispatch_kernel(pos_ref, h_ref, xs_in_ref, xs_ref, sem, *, td):
    del xs_in_ref
    base = pl.program_id(0) * td

    def issue(r, carry):
        t = base + r
        for kk in range(TOP_K):
            _row_copy(h_ref, t, xs_ref, pos_ref[TOP_K * t + kk], sem).start()
        return carry

    lax.fori_loop(0, td, issue, 0)

    def drain(r, carry):
        for kk in range(TOP_K):
            _row_copy(h_ref, 0, xs_ref, 0, sem).wait()
        return carry

    lax.fori_loop(0, td, drain, 0)


def _dispatch(pos, hp2d, n_rows):
    n, w = hp2d.shape
    td = _tile(n, ROWS_DISPATCH, 1)
    return pl.pallas_call(
        functools.partial(_dispatch_kernel, td=td),
        out_shape=jax.ShapeDtypeStruct((n_rows, w), hp2d.dtype),
        grid_spec=pltpu.PrefetchScalarGridSpec(
            num_scalar_prefetch=1,
            grid=(n // td,),
            in_specs=[pl.BlockSpec(memory_space=pl.ANY), pl.BlockSpec(memory_space=pl.ANY)],
            out_specs=pl.BlockSpec(memory_space=pl.ANY),
            scratch_shapes=[pltpu.SemaphoreType.DMA],
        ),
        input_output_aliases={2: 0},
        compiler_params=_params("arbitrary"),
        name="moe_dispatch",
    )(pos, hp2d, jnp.zeros((n_rows, w), hp2d.dtype))


def _ffn_kernel(te_ref, valid_ref, xs_ref, wg_ref, wu_ref, wd_ref, ys_ref):
    del te_ref
    i = pl.program_id(0)

    @pl.when(valid_ref[i] != 0)
    def _():
        x = _unpack_bf16_pairs(xs_ref[...])
        act = (_silu(_dot(x, wg_ref[...])) * _dot(x, wu_ref[...])).astype(BF16)
        ys_ref[...] = _dot(act, wd_ref[...])

    @pl.when(valid_ref[i] == 0)
    def _():
        ys_ref[...] = jnp.zeros_like(ys_ref)


def _ffn(tile_expert, tile_valid, xs, wg, wu, wd, *, tf):
    n_rows, half = xs.shape
    _, d, f = wg.shape
    return pl.pallas_call(
        _ffn_kernel,
        out_shape=jax.ShapeDtypeStruct((n_rows, d), F32),
        grid_spec=pltpu.PrefetchScalarGridSpec(
            num_scalar_prefetch=2,
            grid=(n_rows // tf,),
            in_specs=[
                pl.BlockSpec((tf, half), lambda i, te, tv: (i, 0)),
                pl.BlockSpec((None, d, f), lambda i, te, tv: (te[i], 0, 0)),
                pl.BlockSpec((None, d, f), lambda i, te, tv: (te[i], 0, 0)),
                pl.BlockSpec((None, f, d), lambda i, te, tv: (te[i], 0, 0)),
            ],
            out_specs=pl.BlockSpec((tf, d), lambda i, te, tv: (i, 0)),
        ),
        compiler_params=_params("arbitrary"),
        name="moe_ffn",
    )(tile_expert, tile_valid, xs, wg, wu, wd)


def _combine_kernel(pos_ref, x_ref, rt_ref, g2_ref, fw_ref, ys_ref, o_ref, ybuf, sem, *, seq_len, final_norm):
    bs, tt, d = x_ref.shape
    rows = bs * tt
    base = pl.program_id(0) * (bs * seq_len) + pl.program_id(1) * tt

    def issue(r, carry):
        for kk in range(TOP_K):
            _row_copy(ys_ref, pos_ref[TOP_K * (base + r) + kk], ybuf.at[kk], r, sem).start()
        return carry

    lax.fori_loop(0, rows, issue, 0)

    def drain(r, carry):
        for kk in range(TOP_K):
            _row_copy(ys_ref, 0, ybuf.at[kk], 0, sem).wait()
        return carry

    lax.fori_loop(0, rows, drain, 0)

    rt = rt_ref[...].reshape(rows, LANES)
    y = rt[:, 2:3] * ybuf[0] + rt[:, 3:4] * ybuf[1]
    xn = x_ref[...] + g2_ref[...] * y.reshape(bs, tt, d)
    if final_norm:
        xn = xn * lax.rsqrt(jnp.mean(xn * xn, axis=-1, keepdims=True) + EPS) * fw_ref[...]
    o_ref[...] = xn


def _combine(pos, x1, route, g2, final_w, ys, *, final_norm):
    s, t, d = x1.shape
    bs, tt = _seq_tiles(s, t, ROWS_COMBINE)
    tok = lambda i, j, p: (i, j, 0)
    return pl.pallas_call(
        functools.partial(_combine_kernel, seq_len=t, final_norm=final_norm),
        out_shape=jax.ShapeDtypeStruct((s, t, d), F32),
        grid_spec=pltpu.PrefetchScalarGridSpec(
            num_scalar_prefetch=1,
            grid=(s // bs, t // tt),
            in_specs=[
                pl.BlockSpec((bs, tt, d), tok),
                pl.BlockSpec((bs, tt, LANES), tok),
                pl.BlockSpec((bs, 1, d), lambda i, j, p: (i, 0, 0)),
                pl.BlockSpec((1, d), lambda i, j, p: (0, 0)),
                pl.BlockSpec(memory_space=pl.ANY),
            ],
            out_specs=pl.BlockSpec((bs, tt, d), tok),
            scratch_shapes=[pltpu.VMEM((TOP_K, bs * tt, d), F32), pltpu.SemaphoreType.DMA],
        ),
        compiler_params=_params("arbitrary", "arbitrary"),
        name="moe_combine",
    )(pos, x1, route, g2, final_w.reshape(1, d), ys)


def _moe(x1, hp, route, g2, wg, wu, wd, final_w, *, final_norm):
    s, t, d = x1.shape
    n = s * t
    n_exp = wg.shape[0]
    rank, counts = _rank(route.reshape(n, LANES))
    tf = 2 * SUBLANES
    while tf < ROWS_FFN and tf < 2 * TOP_K * n // n_exp:
        tf *= 2
    n_tiles = (TOP_K * n) // tf + n_exp
    cnt = counts[0, :n_exp].astype(I32)
    padded = ((cnt + tf - 1) // tf) * tf
    ends = jnp.cumsum(padded)
    starts = ends - padded
    expert = route.reshape(n, LANES)[:, :TOP_K].astype(I32)
    pos = (jnp.take(starts, expert) + rank[:, :TOP_K].astype(I32)).reshape(TOP_K * n)
    tile_start = jnp.arange(n_tiles, dtype=I32) * tf
    tile_valid = (tile_start < ends[-1]).astype(I32)
    tile_expert = jnp.minimum(jnp.searchsorted(ends, tile_start, side="right"), n_exp - 1).astype(I32)
    last_used = jnp.max(jnp.where(tile_valid != 0, tile_expert, 0))
    tile_expert = jnp.where(tile_valid != 0, tile_expert, last_used)
    xs = _dispatch(pos, hp.reshape(n, d // 2), n_tiles * tf)
    ys = _ffn(tile_expert, tile_valid, xs, wg, wu, wd, tf=tf)
    return _combine(pos, x1, route, g2, final_w, ys, final_norm=final_norm)


def _rotary_tables(pos0, t, dk, reps):
    half = dk // 2
    inv = ROPE_BASE ** (-jnp.arange(half, dtype=F32) / half)
    ang = (pos0 + jnp.arange(t)).astype(F32)[:, None] * inv[None, :]
    cos = jnp.cos(ang)
    sin = jnp.sin(ang)
    cos2 = jnp.concatenate([cos, cos], axis=-1)
    sin2 = jnp.concatenate([-sin, sin], axis=-1)
    return jnp.tile(cos2, (reps, 1)), jnp.tile(sin2, (reps, 1))


def _trunk(x, mods, pos0, s_ret_in, s_gla_in_t, wts, dims):
    s, t, d = x.shape
    n = s * t
    depth = len(wts)
    rh, rdk, rdv, gh, gdk, gdv = dims["ret_heads"], dims["ret_dk"], dims["ret_dv"], dims["gla_heads"], dims["gla_dk"], dims["gla_dv"]
    bs, tt = _seq_tiles(s, t, ROWS_MATMUL)
    tm = bs * tt
    cos, sin = _rotary_tables(pos0, t, rdk, bs)
    ret_states, gla_states = [], []
    for l in range(depth):
        w = wts[l]
        sh1, sc1, g1, sh2, sc2, g2 = (m[l] for m in mods)
        h2d = _norm(x, w["norm_mix"], sc1, sh1).reshape(n, d)
        qkvg = _proj_ret(h2d, w["w_ret"], cos, sin, heads=rh, dk=rdk, tm=tm)
        qk, vg, la = _proj_gla(h2d, w["w_gla"], w["wup_hi"], w["wup_lo"], w["b_gla"],
                               hdk=gh * gdk, hdv=gh * gdv, dk=gdk, tm=tm)
        o_ret, s_r = _retention(qkvg.reshape(s, t, -1), s_ret_in[l], w["ret_norm"], heads=rh, dk=rdk, dv=rdv)
        o_gla, s_g = _gla(qk.reshape(s, t, -1), vg.reshape(s, t, -1), la.reshape(s, t, -1), s_gla_in_t[l],
                          w["gla_norm"], heads=gh, dk=gdk, dv=gdv)
        merged = _merge(o_ret.reshape(n, -1), o_gla.reshape(n, -1), h2d, w["w_branch"], w["w_merge"], w["b_merge"], tm=tm)
        x1, hp, route = _out_proj(merged.reshape(s, t, d), x, g1, sc2, sh2, w["norm_ffn"], w["w_o"],
                                  w["wr_hi"], w["wr_lo"], w["b_r"], n_groups=dims["n_groups"], per_group=dims["per_group"])
        x = _moe(x1, hp, route, g2, w["w_exp_gate"], w["w_exp_up"], w["w_exp_down"], w["final_norm"],
                 final_norm=(l == depth - 1))
        ret_states.append(s_r)
        gla_states.append(jnp.swapaxes(s_g, -1, -2))
    return x, jnp.stack(ret_states), jnp.stack(gla_states)


def kernel(x_prompt, x_sample, state_ret, state_gla, c_prompt, c_sample, w_ada, b_ada, norm_mix_w, norm_ffn_w, w_in, w_gla_up, b_gla, ret_norm_w, gla_norm_w, w_branch, w_merge, b_merge, w_o, w_router_group, b_router_group, w_router_expert, b_router_expert, w_exp_gate, w_exp_up, w_exp_down, final_norm_w):
    depth, d, _ = w_in.shape
    _, _, rh, rdk, rdv = state_ret.shape
    _, _, gh, gdk, gdv = state_gla.shape
    low_rank = w_gla_up.shape[1]
    n_groups = w_router_group.shape[-1]
    n_exp = w_router_expert.shape[-1]
    assert low_rank <= LANES and n_groups + n_exp <= LANES and n_exp % n_groups == 0
    dims = dict(ret_heads=rh, ret_dk=rdk, ret_dv=rdv, gla_heads=gh, gla_dk=gdk, gla_dv=gdv,
                n_groups=n_groups, per_group=n_exp // n_groups)

    n_ret = 2 * rh * rdk + 2 * rh * rdv
    n_gla = 2 * gh * gdk + 2 * gh * gdv
    assert w_in.shape[-1] == n_ret + n_gla + low_rank
    code_pad = gh * gdk - low_rank
    w_router = jnp.concatenate([w_router_group, w_router_expert], axis=-1)
    w_router = jnp.pad(w_router, ((0, 0), (0, 0), (0, LANES - n_groups - n_exp)))
    b_router = jnp.pad(jnp.concatenate([b_router_group, b_router_expert], axis=-1), ((0, 0), (0, LANES - n_groups - n_exp)))
    wup = jnp.pad(w_gla_up, ((0, 0), (0, LANES - low_rank), (0, 0)))
    wts = []
    for l in range(depth):
        wr_hi, wr_lo = _split_bf16(w_router[l])
        wup_hi, wup_lo = _split_bf16(wup[l])
        wts.append(dict(
            norm_mix=norm_mix_w[l], norm_ffn=norm_ffn_w[l],
            w_ret=w_in[l, :, :n_ret].astype(BF16),
            w_gla=jnp.pad(w_in[l, :, n_ret:], ((0, 0), (0, code_pad))).astype(BF16),
            wup_hi=wup_hi, wup_lo=wup_lo, b_gla=b_gla[l].reshape(1, -1),
            ret_norm=ret_norm_w[l], gla_norm=gla_norm_w[l],
            w_branch=w_branch[l].astype(BF16), w_merge=w_merge[l].astype(BF16), b_merge=b_merge[l],
            w_o=w_o[l].astype(BF16), wr_hi=wr_hi, wr_lo=wr_lo, b_r=b_router[l].reshape(1, LANES),
            w_exp_gate=w_exp_gate[l].astype(BF16), w_exp_up=w_exp_up[l].astype(BF16),
            w_exp_down=w_exp_down[l].astype(BF16), final_norm=final_norm_w,
        ))

    nb, ns = c_prompt.shape[0], c_sample.shape[0]
    mod = _ada(jnp.concatenate([c_prompt, c_sample], axis=0), w_ada, b_ada)
    mod = mod.reshape(depth, nb + ns, 6, d)
    mods_p = [mod[:, :nb, i][:, :, None, :] for i in range(6)]
    mods_s = [mod[:, nb:, i][:, :, None, :] for i in range(6)]

    zero_ret = jnp.zeros((depth, nb, rh, rdk, rdv), F32)
    zero_gla_t = jnp.zeros((depth, nb, gh, gdv, gdk), F32)
    y_p, ret_p, gla_p = _trunk(x_prompt, mods_p, 0, zero_ret, zero_gla_t, wts, dims)
    y_s, ret_s, gla_s = _trunk(x_sample, mods_s, PAST_LEN, state_ret.astype(F32),
                               jnp.swapaxes(state_gla.astype(F32), -1, -2), wts, dims)
    return (y_p, y_s, ret_p, gla_p, ret_s, gla_s)
```

```python
import functools

import jax
import jax.numpy as jnp
from jax import lax
from jax.experimental import pallas as pl
from jax.experimental.pallas import tpu as pltpu

F32 = jnp.float32
BF16 = jnp.bfloat16
U32 = jnp.uint32
I32 = jnp.int32

PAST_LEN = 4096
GLA_TAU = 16.0
ROPE_BASE = 10000.0
EPS = 1e-6
RET_DECAY_LOG2_BASE = -5.0
TOP_K = 2

LANES = 128
SUBLANES = 8
V7X_VMEM_BYTES = 64 * 1024 * 1024
VMEM_LIMIT = V7X_VMEM_BYTES - 8 * 1024 * 1024

ROWS_MATMUL = 1024
ROWS_OUT_PROJ = 512
ROWS_NORM = 512
ROWS_COMBINE = 256
ROWS_RANK = 512
GATHER_UNROLL = 8
ROWS_FFN = 512
CHUNK_RET = 256
CHUNK_GLA = 256


def _params(*sem):
    return pltpu.CompilerParams(dimension_semantics=sem, vmem_limit_bytes=VMEM_LIMIT)


def _tile(n, pref, align):
    if n <= pref:
        return n
    t = (pref // align) * align
    while t >= align:
        if n % t == 0:
            return t
        t -= align
    return n


def _seq_tiles(s, t, rows):
    if t >= rows:
        return 1, _tile(t, rows, 2 * SUBLANES)
    return _tile(s, max(rows // t, 1), 1), t


def _dot(a, b):
    return jnp.dot(a, b, preferred_element_type=F32)


def _dot_nt(a, b):
    return lax.dot_general(a, b, (((1,), (1,)), ((), ())), preferred_element_type=F32)


def _dot_tn(a, b):
    return lax.dot_general(a, b, (((0,), (0,)), ((), ())), preferred_element_type=F32)


def _sigmoid(x):
    return 1.0 / (1.0 + jnp.exp(-x))


def _silu(x):
    return x * _sigmoid(x)


def _log_sigmoid(z):
    return jnp.minimum(z, 0.0) - jnp.log(1.0 + jnp.exp(-jnp.abs(z)))


def _split_bf16(x):
    hi = x.astype(BF16)
    lo = (x - hi.astype(F32)).astype(BF16)
    return hi, lo


def _dot_split(a, w_hi, w_lo):
    a_hi, a_lo = _split_bf16(a)
    return _dot(a_hi, w_hi) + _dot(a_lo, w_hi) + _dot(a_hi, w_lo)


def _ada_kernel(c_ref, w_ref, b_ref, o_ref):
    a = _silu(c_ref[...]).astype(BF16)
    o_ref[...] = _dot(a, w_ref[...].astype(BF16)) + b_ref[...]


def _ada(c_all, w_ada, b_ada):
    depth, d, n6 = w_ada.shape
    r = c_all.shape[0]
    tn = _tile(n6, 1024, LANES)
    return pl.pallas_call(
        _ada_kernel,
        out_shape=jax.ShapeDtypeStruct((depth, r, n6), F32),
        grid=(depth, n6 // tn),
        in_specs=[
            pl.BlockSpec((r, d), lambda l, j: (0, 0)),
            pl.BlockSpec((None, d, tn), lambda l, j: (l, 0, j)),
            pl.BlockSpec((None, 1, tn), lambda l, j: (l, 0, j)),
        ],
        out_specs=pl.BlockSpec((None, r, tn), lambda l, j: (l, 0, j)),
        compiler_params=_params("arbitrary", "arbitrary"),
        name="ada_mod",
    )(c_all, w_ada, b_ada.reshape(depth, 1, n6))


def _norm_kernel(x_ref, w_ref, sc_ref, sh_ref, o_ref):
    x = x_ref[...]
    y = x * lax.rsqrt(jnp.mean(x * x, axis=-1, keepdims=True) + EPS) * w_ref[...]
    o_ref[...] = (y * (1.0 + sc_ref[...]) + sh_ref[...]).astype(o_ref.dtype)


def _norm(x, w, sc, sh):
    s, t, d = x.shape
    bs, tt = _seq_tiles(s, t, ROWS_NORM)
    return pl.pallas_call(
        _norm_kernel,
        out_shape=jax.ShapeDtypeStruct((s, t, d), BF16),
        grid=(s // bs, t // tt),
        in_specs=[
            pl.BlockSpec((bs, tt, d), lambda i, j: (i, j, 0)),
            pl.BlockSpec((1, d), lambda i, j: (0, 0)),
            pl.BlockSpec((bs, 1, d), lambda i, j: (i, 0, 0)),
            pl.BlockSpec((bs, 1, d), lambda i, j: (i, 0, 0)),
        ],
        out_specs=pl.BlockSpec((bs, tt, d), lambda i, j: (i, j, 0)),
        compiler_params=_params("arbitrary", "arbitrary"),
        name="mod_norm",
    )(x, w.reshape(1, d), sc, sh)


def _proj_ret_kernel(h_ref, w_ref, cos_ref, sin_ref, o_ref, *, heads, dk, kscale):
    j = pl.program_id(1)
    acc = _dot(h_ref[...], w_ref[...])

    @pl.when(j < 2)
    def _():
        cos = cos_ref[...]
        sin = sin_ref[...]
        mult = jnp.where(j == 1, kscale, 1.0).astype(F32)
        for hd in range(heads):
            a = acc[:, hd * dk:(hd + 1) * dk]
            r = a * cos + pltpu.roll(a, dk // 2, 1) * sin
            o_ref[:, hd * dk:(hd + 1) * dk] = (r * mult).astype(o_ref.dtype)

    @pl.when(j == 2)
    def _():
        o_ref[...] = acc.astype(o_ref.dtype)

    @pl.when(j == 3)
    def _():
        o_ref[...] = _silu(acc).astype(o_ref.dtype)


def _proj_ret(h2d, w, cos, sin, *, heads, dk, tm):
    n, d = h2d.shape
    tn = heads * dk
    assert w.shape == (d, 4 * tn) and cos.shape[0] % tm == 0
    nt = cos.shape[0] // tm
    return pl.pallas_call(
        functools.partial(_proj_ret_kernel, heads=heads, dk=dk, kscale=dk ** -0.5),
        out_shape=jax.ShapeDtypeStruct((n, 4 * tn), BF16),
        grid=(n // tm, 4),
        in_specs=[
            pl.BlockSpec((tm, d), lambda i, j: (i, 0)),
            pl.BlockSpec((d, tn), lambda i, j: (0, j)),
            pl.BlockSpec((tm, dk), lambda i, j: (i % nt, 0)),
            pl.BlockSpec((tm, dk), lambda i, j: (i % nt, 0)),
        ],
        out_specs=pl.BlockSpec((tm, tn), lambda i, j: (i, j)),
        compiler_params=_params("arbitrary", "arbitrary"),
        name="proj_ret",
    )(h2d, w, cos, sin)


def _proj_gla_kernel(h_ref, w_ref, wup_hi_ref, wup_lo_ref, bup_ref, qk_ref, vg_ref, la_ref, *, nv, qscale):
    j = pl.program_id(1)
    acc = _dot(h_ref[...], w_ref[...])

    @pl.when(j == 0)
    def _():
        qk_ref[...] = acc * qscale

    @pl.when(j == 1)
    def _():
        qk_ref[...] = acc

    @pl.when((j >= 2) & (j < 2 + nv))
    def _():
        vg_ref[...] = acc.astype(vg_ref.dtype)

    @pl.when((j >= 2 + nv) & (j < 2 + 2 * nv))
    def _():
        vg_ref[...] = _silu(acc).astype(vg_ref.dtype)

    @pl.when(j == 2 + 2 * nv)
    def _():
        z = _dot_split(acc[:, :LANES], wup_hi_ref[...], wup_lo_ref[...])
        la_ref[...] = _log_sigmoid(z + bup_ref[...]) * (1.0 / GLA_TAU)


def _proj_gla(h2d, w, wup_hi, wup_lo, bup, *, hdk, hdv, dk, tm):
    n, d = h2d.shape
    tn = hdk
    nv = hdv // tn
    nblk = 3 + 2 * nv
    assert hdv % tn == 0 and w.shape == (d, nblk * tn) and wup_hi.shape == (LANES, hdk)
    return pl.pallas_call(
        functools.partial(_proj_gla_kernel, nv=nv, qscale=dk ** -0.5),
        out_shape=(
            jax.ShapeDtypeStruct((n, 2 * hdk), F32),
            jax.ShapeDtypeStruct((n, 2 * hdv), BF16),
            jax.ShapeDtypeStruct((n, hdk), F32),
        ),
        grid=(n // tm, nblk),
        in_specs=[
            pl.BlockSpec((tm, d), lambda i, j: (i, 0)),
            pl.BlockSpec((d, tn), lambda i, j: (0, j)),
            pl.BlockSpec((LANES, hdk), lambda i, j: (0, 0)),
            pl.BlockSpec((LANES, hdk), lambda i, j: (0, 0)),
            pl.BlockSpec((1, hdk), lambda i, j: (0, 0)),
        ],
        out_specs=(
            pl.BlockSpec((tm, tn), lambda i, j: (i, jnp.minimum(j, 1))),
            pl.BlockSpec((tm, tn), lambda i, j: (i, jnp.clip(j - 2, 0, 2 * nv - 1))),
            pl.BlockSpec((tm, hdk), lambda i, j: (i, 0)),
        ),
        compiler_params=_params("arbitrary", "arbitrary"),
        name="proj_gla",
    )(h2d, w, wup_hi, wup_lo, bup)


def _ret_kernel(q_ref, k_ref, v_ref, g_ref, s0_ref, dm_ref, ind_ref, sd_ref, cd_ref, w_ref,
                o_ref, so_ref, st_ref, *, heads, dk, dv):
    c = pl.program_id(1)

    @pl.when(c == 0)
    def _():
        st_ref[...] = s0_ref[0]

    for hd in range(heads):
        q = q_ref[0, :, hd * dk:(hd + 1) * dk]
        k = k_ref[0, :, hd * dk:(hd + 1) * dk]
        v = v_ref[0, :, hd * dv:(hd + 1) * dv]
        s = st_ref[hd]
        p = (_dot_nt(q, k) * dm_ref[hd]).astype(BF16)
        o = _dot(p, v) + _dot(q, s.astype(BF16)) * ind_ref[hd]
        ks = (k.astype(F32) * sd_ref[hd]).astype(BF16)
        st_ref[hd] = s * cd_ref[hd] + _dot_tn(ks, v)
        oc = o - jnp.mean(o, axis=-1, keepdims=True)
        on = oc * lax.rsqrt(jnp.mean(oc * oc, axis=-1, keepdims=True) + EPS)
        gate = g_ref[0, :, hd * dv:(hd + 1) * dv].astype(F32)
        o_ref[0, :, hd * dv:(hd + 1) * dv] = (gate * (on * w_ref[:, hd * dv:(hd + 1) * dv])).astype(o_ref.dtype)

    @pl.when(c == pl.num_programs(1) - 1)
    def _():
        so_ref[0] = st_ref[...]


def _ret_tables(heads, c, dk, dv):
    log_gamma = jnp.log1p(-jnp.exp2(RET_DECAY_LOG2_BASE - jnp.arange(heads, dtype=F32)))
    idx = jnp.arange(c, dtype=F32)
    diff = idx[:, None] - idx[None, :]
    causal = diff >= 0
    dmask = jnp.where(causal[None], jnp.exp(log_gamma[:, None, None] * jnp.where(causal, diff, 0.0)[None]), 0.0)
    inner = jnp.exp(log_gamma[:, None] * (idx + 1.0))
    sdecay = jnp.exp(log_gamma[:, None] * (c - 1.0 - idx))
    cdecay = jnp.exp(log_gamma * c)
    return (dmask,
            jnp.broadcast_to(inner[:, :, None], (heads, c, dv)),
            jnp.broadcast_to(sdecay[:, :, None], (heads, c, dk)),
            jnp.broadcast_to(cdecay[:, None, None], (heads, 1, dv)))


def _retention(qkvg, s0, norm_w, *, heads, dk, dv):
    s, t, _ = qkvg.shape
    assert dk == dv
    c = _tile(t, CHUNK_RET, 2 * SUBLANES)
    w = heads * dk
    dm, ind, sd, cd = _ret_tables(heads, c, dk, dv)
    const3 = lambda b, i: (0, 0, 0)
    return pl.pallas_call(
        functools.partial(_ret_kernel, heads=heads, dk=dk, dv=dv),
        out_shape=(jax.ShapeDtypeStruct((s, t, w), BF16), jax.ShapeDtypeStruct(s0.shape, F32)),
        grid=(s, t // c),
        in_specs=[
            pl.BlockSpec((1, c, w), lambda b, i: (b, i, 0)),
            pl.BlockSpec((1, c, w), lambda b, i: (b, i, 1)),
            pl.BlockSpec((1, c, w), lambda b, i: (b, i, 2)),
            pl.BlockSpec((1, c, w), lambda b, i: (b, i, 3)),
            pl.BlockSpec((1, heads, dk, dv), lambda b, i: (b, 0, 0, 0)),
            pl.BlockSpec((heads, c, c), const3),
            pl.BlockSpec((heads, c, dv), const3),
            pl.BlockSpec((heads, c, dk), const3),
            pl.BlockSpec((heads, 1, dv), const3),
            pl.BlockSpec((1, w), lambda b, i: (0, 0)),
        ],
        out_specs=(
            pl.BlockSpec((1, c, w), lambda b, i: (b, i, 0)),
            pl.BlockSpec((1, heads, dk, dv), lambda b, i: (b, 0, 0, 0)),
        ),
        scratch_shapes=[pltpu.VMEM((heads, dk, dv), F32)],
        compiler_params=_params("arbitrary", "arbitrary"),
        name="retention",
    )(qkvg, qkvg, qkvg, qkvg, s0, dm, ind, sd, cd, norm_w.reshape(1, w))


def _gla_kernel(q_ref, k_ref, la_ref, v_ref, g_ref, s0_ref, w_ref, o_ref, so_ref, st_ref, *, heads, dk, dv, c):
    ci = pl.program_id(1)

    @pl.when(ci == 0)
    def _():
        st_ref[...] = s0_ref[0]

    row = lax.broadcasted_iota(I32, (c, dk), 0)
    ri = lax.broadcasted_iota(I32, (c, c), 0)
    cj = lax.broadcasted_iota(I32, (c, c), 1)
    for hd in range(heads):
        q = q_ref[0, :, hd * dk:(hd + 1) * dk]
        k = k_ref[0, :, hd * dk:(hd + 1) * dk]
        v = v_ref[0, :, hd * dv:(hd + 1) * dv]
        b = la_ref[0, :, hd * dk:(hd + 1) * dk]
        sh = 1
        while sh < c:
            b = b + jnp.where(row >= sh, pltpu.roll(b, sh, 0), 0.0)
            sh *= 2
        scores = jnp.where(ri == cj, _dot_nt(q.astype(BF16), k.astype(BF16)), 0.0)
        first = b
        half = 1
        while half < c:
            upper = (row & (2 * half - 1)) >= half
            mid = jnp.where(upper, first, pltpu.roll(first, c - half, 0))
            ql = jnp.where(upper, q * jnp.exp(jnp.minimum(b - mid, 0.0)), 0.0).astype(BF16)
            kl = jnp.where(upper, 0.0, k * jnp.exp(jnp.minimum(mid - b, 0.0))).astype(BF16)
            same = (ri & -(2 * half)) == (cj & -(2 * half))
            scores = scores + jnp.where(same, _dot_nt(ql, kl), 0.0)
            first = jnp.where(upper, pltpu.roll(first, half, 0), first)
            half *= 2
        st = st_ref[hd]
        o = _dot(scores.astype(BF16), v) + _dot_nt((q * jnp.exp(b)).astype(BF16), st.astype(BF16))
        b_last = b[c - 1:c, :]
        kd = (k * jnp.exp(b_last - b)).astype(BF16)
        st_ref[hd] = st * jnp.exp(b_last) + _dot_tn(v, kd)
        on = o * lax.rsqrt(jnp.mean(o * o, axis=-1, keepdims=True) + EPS)
        gate = g_ref[0, :, hd * dv:(hd + 1) * dv].astype(F32)
        o_ref[0, :, hd * dv:(hd + 1) * dv] = (gate * (on * w_ref[:, hd * dv:(hd + 1) * dv])).astype(o_ref.dtype)

    @pl.when(ci == pl.num_programs(1) - 1)
    def _():
        so_ref[0] = st_ref[...]


def _gla(qk, vg, la, s0t, norm_w, *, heads, dk, dv):
    s, t, _ = qk.shape
    c = _tile(t, CHUNK_GLA, 2 * SUBLANES)
    assert c & (c - 1) == 0, "chunk length must be a power of two"
    wk, wv = heads * dk, heads * dv
    return pl.pallas_call(
        functools.partial(_gla_kernel, heads=heads, dk=dk, dv=dv, c=c),
        out_shape=(jax.ShapeDtypeStruct((s, t, wv), BF16), jax.ShapeDtypeStruct(s0t.shape, F32)),
        grid=(s, t // c),
        in_specs=[
            pl.BlockSpec((1, c, wk), lambda b, i: (b, i, 0)),
            pl.BlockSpec((1, c, wk), lambda b, i: (b, i, 1)),
            pl.BlockSpec((1, c, wk), lambda b, i: (b, i, 0)),
            pl.BlockSpec((1, c, wv), lambda b, i: (b, i, 0)),
            pl.BlockSpec((1, c, wv), lambda b, i: (b, i, 1)),
            pl.BlockSpec((1, heads, dv, dk), lambda b, i: (b, 0, 0, 0)),
            pl.BlockSpec((1, wv), lambda b, i: (0, 0)),
        ],
        out_specs=(
            pl.BlockSpec((1, c, wv), lambda b, i: (b, i, 0)),
            pl.BlockSpec((1, heads, dv, dk), lambda b, i: (b, 0, 0, 0)),
        ),
        scratch_shapes=[pltpu.VMEM((heads, dv, dk), F32)],
        compiler_params=_params("arbitrary", "arbitrary"),
        name="gla",
    )(qk, qk, la, vg, vg, s0t, norm_w.reshape(1, wv))


def _merge_kernel(br_ref, bg_ref, h_ref, wr_ref, wg_ref, wm0_ref, wm1_ref, bm0_ref, bm1_ref, o_ref):
    h = h_ref[...]
    g0 = _sigmoid(_dot(h, wm0_ref[...]) + bm0_ref[...])
    g1 = _sigmoid(_dot(h, wm1_ref[...]) + bm1_ref[...])
    y = g0 * _dot(br_ref[...], wr_ref[...]) + g1 * _dot(bg_ref[...], wg_ref[...])
    o_ref[...] = y.astype(o_ref.dtype)


def _merge(o_ret, o_gla, h2d, w_branch, w_merge, b_merge, *, tm):
    n, d = h2d.shape
    wdt = o_ret.shape[1]
    tn = _tile(d, 512, LANES)
    nj = d // tn
    b2 = b_merge.reshape(1, 2 * d)
    return pl.pallas_call(
        _merge_kernel,
        out_shape=jax.ShapeDtypeStruct((n, d), BF16),
        grid=(n // tm, nj),
        in_specs=[
            pl.BlockSpec((tm, wdt), lambda i, j: (i, 0)),
            pl.BlockSpec((tm, wdt), lambda i, j: (i, 0)),
            pl.BlockSpec((tm, d), lambda i, j: (i, 0)),
            pl.BlockSpec((None, wdt, tn), lambda i, j: (0, 0, j)),
            pl.BlockSpec((None, wdt, tn), lambda i, j: (1, 0, j)),
            pl.BlockSpec((d, tn), lambda i, j: (0, j)),
            pl.BlockSpec((d, tn), lambda i, j: (0, j + nj)),
            pl.BlockSpec((1, tn), lambda i, j: (0, j)),
            pl.BlockSpec((1, tn), lambda i, j: (0, j + nj)),
        ],
        out_specs=pl.BlockSpec((tm, tn), lambda i, j: (i, j)),
        compiler_params=_params("arbitrary", "arbitrary"),
        name="branch_merge",
    )(o_ret, o_gla, h2d, w_branch, w_branch, w_merge, w_merge, b2, b2)


def _pack_bf16_pairs(h):
    half = h.shape[-1] // 2
    a = lax.bitcast_convert_type(h[:, :half].astype(BF16).astype(F32), U32)
    b = lax.bitcast_convert_type(h[:, half:].astype(BF16).astype(F32), U32)
    return a | (b >> 16)


def _unpack_bf16_pairs(w):
    a = lax.bitcast_convert_type(w & jnp.uint32(0xFFFF0000), F32)
    b = lax.bitcast_convert_type(w << 16, F32)
    return jnp.concatenate([a, b], axis=-1).astype(BF16)


def _route(logits, n_groups, per_group):
    lane = lax.broadcasted_iota(I32, logits.shape, 1).astype(F32)
    neg = jnp.float32(-jnp.inf)

    def first_max(mask):
        m = jnp.max(jnp.where(mask, logits, neg), axis=-1, keepdims=True)
        idx = jnp.min(jnp.where(mask & (logits == m), lane, float(LANES)), axis=-1, keepdims=True)
        return m, idx

    gmask = lane < n_groups
    gmax, gidx = first_max(gmask)
    p_group = 1.0 / jnp.sum(jnp.where(gmask, jnp.exp(logits - gmax), 0.0), axis=-1, keepdims=True)
    lo = n_groups + gidx * per_group
    emask = (lane >= lo) & (lane < lo + per_group)
    m1, i1 = first_max(emask)
    m2, i2 = first_max(emask & (lane != i1))
    w1 = 1.0 / (1.0 + jnp.exp(m2 - m1))
    w2 = 1.0 - w1
    return jnp.where(lane == 0.0, i1 - n_groups, jnp.where(lane == 1.0, i2 - n_groups, jnp.where(
        lane == 2.0, p_group * w1, jnp.where(lane == 3.0, p_group * w2, 0.0))))


def _out_proj_kernel(m_ref, x_ref, g1_ref, sc_ref, sh_ref, nw_ref, wo_ref, wr_hi_ref, wr_lo_ref, br_ref,
                     x1_ref, hp_ref, rt_ref, *, n_groups, per_group):
    bs, tt, d = x_ref.shape
    y = _dot(m_ref[...].reshape(bs * tt, d), wo_ref[...]).reshape(bs, tt, d)
    x1 = x_ref[...] + g1_ref[...] * y
    x1_ref[...] = x1
    hn = x1 * lax.rsqrt(jnp.mean(x1 * x1, axis=-1, keepdims=True) + EPS) * nw_ref[...]
    h = (hn * (1.0 + sc_ref[...]) + sh_ref[...]).reshape(bs * tt, d)
    hp_ref[...] = _pack_bf16_pairs(h).reshape(bs, tt, d // 2)
    logits = _dot_split(h, wr_hi_ref[...], wr_lo_ref[...]) + br_ref[...]
    rt_ref[...] = _route(logits, n_groups, per_group).reshape(bs, tt, LANES)


def _out_proj(merged, x, g1, sc2, sh2, norm_w, w_o, wr_hi, wr_lo, b_r, *, n_groups, per_group):
    s, t, d = x.shape
    bs, tt = _seq_tiles(s, t, ROWS_OUT_PROJ)
    tok = lambda i, j: (i, j, 0)
    seq = lambda i, j: (i, 0, 0)
    const = lambda i, j: (0, 0)
    return pl.pallas_call(
        functools.partial(_out_proj_kernel, n_groups=n_groups, per_group=per_group),
        out_shape=(
            jax.ShapeDtypeStruct((s, t, d), F32),
            jax.ShapeDtypeStruct((s, t, d // 2), U32),
            jax.ShapeDtypeStruct((s, t, LANES), F32),
        ),
        grid=(s // bs, t // tt),
        in_specs=[
            pl.BlockSpec((bs, tt, d), tok),
            pl.BlockSpec((bs, tt, d), tok),
            pl.BlockSpec((bs, 1, d), seq),
            pl.BlockSpec((bs, 1, d), seq),
            pl.BlockSpec((bs, 1, d), seq),
            pl.BlockSpec((1, d), const),
            pl.BlockSpec((d, d), const),
            pl.BlockSpec((d, LANES), const),
            pl.BlockSpec((d, LANES), const),
            pl.BlockSpec((1, LANES), const),
        ],
        out_specs=(
            pl.BlockSpec((bs, tt, d), tok),
            pl.BlockSpec((bs, tt, d // 2), tok),
            pl.BlockSpec((bs, tt, LANES), tok),
        ),
        compiler_params=_params("arbitrary", "arbitrary"),
        name="out_proj_router",
    )(merged, x, g1, sc2, sh2, norm_w.reshape(1, d), w_o, wr_hi, wr_lo, b_r)


def _rank_kernel(rt_ref, rank_ref, cnt_ref, carry_ref):
    i = pl.program_id(0)

    @pl.when(i == 0)
    def _():
        carry_ref[...] = jnp.zeros_like(carry_ref)

    r = rt_ref[...]
    tr = r.shape[0]
    lane = lax.broadcasted_iota(I32, r.shape, 1).astype(F32)
    a1 = lane == r[:, 0:1]
    a2 = lane == r[:, 1:2]
    hit = jnp.where(a1 | a2, 1.0, 0.0)
    ri = lax.broadcasted_iota(I32, (tr, tr), 0)
    cj = lax.broadcasted_iota(I32, (tr, tr), 1)
    before = _dot(jnp.where(ri > cj, 1.0, 0.0).astype(BF16), hit.astype(BF16)) + carry_ref[...]
    k1 = jnp.sum(jnp.where(a1, before, 0.0), axis=-1, keepdims=True)
    k2 = jnp.sum(jnp.where(a2, before, 0.0), axis=-1, keepdims=True)
    rank_ref[...] = jnp.where(lane == 0.0, k1, jnp.where(lane == 1.0, k2, 0.0))
    carry_ref[...] += jnp.sum(hit, axis=0, keepdims=True)
    cnt_ref[...] = carry_ref[...]


def _rank(route2d):
    n = route2d.shape[0]
    tr = _tile(n, ROWS_RANK, SUBLANES)
    return pl.pallas_call(
        _rank_kernel,
        out_shape=(jax.ShapeDtypeStruct((n, LANES), F32), jax.ShapeDtypeStruct((1, LANES), F32)),
        grid=(n // tr,),
        in_specs=[pl.BlockSpec((tr, LANES), lambda i: (i, 0))],
        out_specs=(pl.BlockSpec((tr, LANES), lambda i: (i, 0)), pl.BlockSpec((1, LANES), lambda i: (0, 0))),
        scratch_shapes=[pltpu.VMEM((1, LANES), F32)],
        compiler_params=_params("arbitrary"),
        name="moe_rank",
    )(route2d)


def _row_copy(src_ref, src_row, dst_ref, dst_row, sem):
    return pltpu.make_async_copy(src_ref.at[pl.ds(src_row, 1)], dst_ref.at[pl.ds(dst_row, 1)], sem)


def _ffn_kernel(te_ref, valid_ref, src_ref, hp_ref, wg_ref, wu_ref, wd_ref, ys_ref, xbuf, sems, *, tf):
    del te_ref
    i = pl.program_id(0)

    def gather(tile, slot):
        def issue(r, carry):
            _row_copy(hp_ref, src_ref[tile * tf + r], xbuf.at[slot], r, sems.at[slot]).start()
            return carry

        lax.fori_loop(0, tf, issue, 0, unroll=GATHER_UNROLL)

    def drain(slot):
        def wait(r, carry):
            _row_copy(hp_ref, 0, xbuf.at[slot], 0, sems.at[slot]).wait()
            return carry

        lax.fori_loop(0, tf, wait, 0, unroll=GATHER_UNROLL)

    @pl.when((i == 0) & (valid_ref[0] != 0))
    def _():
        gather(0, 0)

    nxt = jnp.minimum(i + 1, pl.num_programs(0) - 1)

    @pl.when((i + 1 < pl.num_programs(0)) & (valid_ref[nxt] != 0))
    def _():
        gather(i + 1, (i + 1) % 2)

    @pl.when(valid_ref[i] != 0)
    def _():
        slot = i % 2
        drain(slot)
        x = _unpack_bf16_pairs(xbuf[slot])
        act = (_silu(_dot(x, wg_ref[...])) * _dot(x, wu_ref[...])).astype(BF16)
        ys_ref[...] = _dot(act, wd_ref[...])

    @pl.when(valid_ref[i] == 0)
    def _():
        ys_ref[...] = jnp.zeros_like(ys_ref)


def _ffn(tile_expert, tile_valid, src, hp2d, wg, wu, wd, *, tf):
    n_rows = src.shape[0]
    half = hp2d.shape[1]
    _, d, f = wg.shape
    return pl.pallas_call(
        functools.partial(_ffn_kernel, tf=tf),
        out_shape=jax.ShapeDtypeStruct((n_rows, d), F32),
        grid_spec=pltpu.PrefetchScalarGridSpec(
            num_scalar_prefetch=3,
            grid=(n_rows // tf,),
            in_specs=[
                pl.BlockSpec(memory_space=pl.ANY),
                pl.BlockSpec((None, d, f), lambda i, te, tv, sr: (te[i], 0, 0)),
                pl.BlockSpec((None, d, f), lambda i, te, tv, sr: (te[i], 0, 0)),
                pl.BlockSpec((None, f, d), lambda i, te, tv, sr: (te[i], 0, 0)),
            ],
            out_specs=pl.BlockSpec((tf, d), lambda i, te, tv, sr: (i, 0)),
            scratch_shapes=[pltpu.VMEM((2, tf, half), hp2d.dtype), pltpu.SemaphoreType.DMA((2,))],
        ),
        compiler_params=_params("arbitrary"),
        name="moe_ffn",
    )(tile_expert, tile_valid, src, hp2d, wg, wu, wd)


def _combine_kernel(pos_ref, x_ref, rt_ref, g2_ref, fw_ref, ys_ref, o_ref, ybuf, sems, *, final_norm):
    bs, tt, d = x_ref.shape
    rows = bs * tt
    step = pl.program_id(0) * pl.num_programs(1) + pl.program_id(1)
    n_steps = pl.num_programs(0) * pl.num_programs(1)

    def gather(stp, slot):
        def issue(r, carry):
            for kk in range(TOP_K):
                _row_copy(ys_ref, pos_ref[TOP_K * (stp * rows + r) + kk], ybuf.at[slot, kk], r, sems.at[slot]).start()
            return carry

        lax.fori_loop(0, rows, issue, 0, unroll=GATHER_UNROLL)

    @pl.when(step == 0)
    def _():
        gather(0, 0)

    @pl.when(step + 1 < n_steps)
    def _():
        gather(step + 1, (step + 1) % 2)

    slot = step % 2

    def wait(r, carry):
        for kk in range(TOP_K):
            _row_copy(ys_ref, 0, ybuf.at[slot, kk], 0, sems.at[slot]).wait()
        return carry

    lax.fori_loop(0, rows, wait, 0, unroll=GATHER_UNROLL)

    rt = rt_ref[...].reshape(rows, LANES)
    y = rt[:, 2:3] * ybuf[slot, 0] + rt[:, 3:4] * ybuf[slot, 1]
    xn = x_ref[...] + g2_ref[...] * y.reshape(bs, tt, d)
    if final_norm:
        xn = xn * lax.rsqrt(jnp.mean(xn * xn, axis=-1, keepdims=True) + EPS) * fw_ref[...]
    o_ref[...] = xn


def _combine(pos, x1, route, g2, final_w, ys, *, final_norm):
    s, t, d = x1.shape
    bs, tt = _seq_tiles(s, t, ROWS_COMBINE)
    tok = lambda i, j, p: (i, j, 0)
    return pl.pallas_call(
        functools.partial(_combine_kernel, final_norm=final_norm),
        out_shape=jax.ShapeDtypeStruct((s, t, d), F32),
        grid_spec=pltpu.PrefetchScalarGridSpec(
            num_scalar_prefetch=1,
            grid=(s // bs, t // tt),
            in_specs=[
                pl.BlockSpec((bs, tt, d), tok),
                pl.BlockSpec((bs, tt, LANES), tok),
                pl.BlockSpec((bs, 1, d), lambda i, j, p: (i, 0, 0)),
                pl.BlockSpec((1, d), lambda i, j, p: (0, 0)),
                pl.BlockSpec(memory_space=pl.ANY),
            ],
            out_specs=pl.BlockSpec((bs, tt, d), tok),
            scratch_shapes=[pltpu.VMEM((2, TOP_K, bs * tt, d), F32), pltpu.SemaphoreType.DMA((2,))],
        ),
        compiler_params=_params("arbitrary", "arbitrary"),
        name="moe_combine",
    )(pos, x1, route, g2, final_w.reshape(1, d), ys)


def _moe(x1, hp, route, g2, wg, wu, wd, final_w, *, final_norm):
    s, t, d = x1.shape
    n = s * t
    n_exp = wg.shape[0]
    rank, counts = _rank(route.reshape(n, LANES))
    tf = 2 * SUBLANES
    while tf < ROWS_FFN and tf < 2 * TOP_K * n // n_exp:
        tf *= 2
    n_tiles = (TOP_K * n) // tf + n_exp
    cnt = counts[0, :n_exp].astype(I32)
    padded = ((cnt + tf - 1) // tf) * tf
    ends = jnp.cumsum(padded)
    starts = ends - padded
    expert = route.reshape(n, LANES)[:, :TOP_K].astype(I32)
    start_of = jnp.sum(jnp.where(expert[..., None] == jnp.arange(n_exp, dtype=I32), starts, 0), axis=-1)
    pos = (start_of + rank[:, :TOP_K].astype(I32)).reshape(TOP_K * n)
    tile_start = jnp.arange(n_tiles, dtype=I32) * tf
    tile_valid = (tile_start < ends[-1]).astype(I32)
    tile_expert = jnp.minimum(jnp.sum((ends[None, :] <= tile_start[:, None]).astype(I32), axis=1), n_exp - 1)
    last_used = jnp.max(jnp.where(tile_valid != 0, tile_expert, 0))
    tile_expert = jnp.where(tile_valid != 0, tile_expert, last_used)
    token_of = jnp.arange(TOP_K * n, dtype=I32) // TOP_K
    src = jnp.zeros((n_tiles * tf,), I32).at[pos].set(token_of, unique_indices=True)
    ys = _ffn(tile_expert, tile_valid, src, hp.reshape(n, d // 2), wg, wu, wd, tf=tf)
    return _combine(pos, x1, route, g2, final_w, ys, final_norm=final_norm)


def _rotary_tables(pos0, t, dk, reps):
    half = dk // 2
    inv = ROPE_BASE ** (-jnp.arange(half, dtype=F32) / half)
    ang = (pos0 + jnp.arange(t)).astype(F32)[:, None] * inv[None, :]
    cos = jnp.cos(ang)
    sin = jnp.sin(ang)
    cos2 = jnp.concatenate([cos, cos], axis=-1)
    sin2 = jnp.concatenate([-sin, sin], axis=-1)
    return jnp.tile(cos2, (reps, 1)), jnp.tile(sin2, (reps, 1))


def _trunk(x, mods, pos0, s_ret_in, s_gla_in_t, wts, dims):
    s, t, d = x.shape
    n = s * t
    depth = len(wts)
    rh, rdk, rdv, gh, gdk, gdv = dims["ret_heads"], dims["ret_dk"], dims["ret_dv"], dims["gla_heads"], dims["gla_dk"], dims["gla_dv"]
    bs, tt = _seq_tiles(s, t, ROWS_MATMUL)
    tm = bs * tt
    cos, sin = _rotary_tables(pos0, t, rdk, bs)
    ret_states, gla_states = [], []
    for l in range(depth):
        w = wts[l]
        sh1, sc1, g1, sh2, sc2, g2 = (m[l] for m in mods)
        h2d = _norm(x, w["norm_mix"], sc1, sh1).reshape(n, d)
        qkvg = _proj_ret(h2d, w["w_ret"], cos, sin, heads=rh, dk=rdk, tm=tm)
        qk, vg, la = _proj_gla(h2d, w["w_gla"], w["wup_hi"], w["wup_lo"], w["b_gla"],
                               hdk=gh * gdk, hdv=gh * gdv, dk=gdk, tm=tm)
        o_ret, s_r = _retention(qkvg.reshape(s, t, -1), s_ret_in[l], w["ret_norm"], heads=rh, dk=rdk, dv=rdv)
        o_gla, s_g = _gla(qk.reshape(s, t, -1), vg.reshape(s, t, -1), la.reshape(s, t, -1), s_gla_in_t[l],
                          w["gla_norm"], heads=gh, dk=gdk, dv=gdv)
        merged = _merge(o_ret.reshape(n, -1), o_gla.reshape(n, -1), h2d, w["w_branch"], w["w_merge"], w["b_merge"], tm=tm)
        x1, hp, route = _out_proj(merged.reshape(s, t, d), x, g1, sc2, sh2, w["norm_ffn"], w["w_o"],
                                  w["wr_hi"], w["wr_lo"], w["b_r"], n_groups=dims["n_groups"], per_group=dims["per_group"])
        x = _moe(x1, hp, route, g2, w["w_exp_gate"], w["w_exp_up"], w["w_exp_down"], w["final_norm"],
                 final_norm=(l == depth - 1))
        ret_states.append(s_r)
        gla_states.append(jnp.swapaxes(s_g, -1, -2))
    return x, jnp.stack(ret_states), jnp.stack(gla_states)


def kernel(x_prompt, x_sample, state_ret, state_gla, c_prompt, c_sample, w_ada, b_ada, norm_mix_w, norm_ffn_w, w_in, w_gla_up, b_gla, ret_norm_w, gla_norm_w, w_branch, w_merge, b_merge, w_o, w_router_group, b_router_group, w_router_expert, b_router_expert, w_exp_gate, w_exp_up, w_exp_down, final_norm_w):
    depth, d, _ = w_in.shape
    _, _, rh, rdk, rdv = state_ret.shape
    _, _, gh, gdk, gdv = state_gla.shape
    low_rank = w_gla_up.shape[1]
    n_groups = w_router_group.shape[-1]
    n_exp = w_router_expert.shape[-1]
    assert low_rank <= LANES and n_groups + n_exp <= LANES and n_exp % n_groups == 0
    dims = dict(ret_heads=rh, ret_dk=rdk, ret_dv=rdv, gla_heads=gh, gla_dk=gdk, gla_dv=gdv,
                n_groups=n_groups, per_group=n_exp // n_groups)

    n_ret = 2 * rh * rdk + 2 * rh * rdv
    n_gla = 2 * gh * gdk + 2 * gh * gdv
    assert w_in.shape[-1] == n_ret + n_gla + low_rank
    code_pad = gh * gdk - low_rank
    w_router = jnp.concatenate([w_router_group, w_router_expert], axis=-1)
    w_router = jnp.pad(w_router, ((0, 0), (0, 0), (0, LANES - n_groups - n_exp)))
    b_router = jnp.pad(jnp.concatenate([b_router_group, b_router_expert], axis=-1), ((0, 0), (0, LANES - n_groups - n_exp)))
    wup = jnp.pad(w_gla_up, ((0, 0), (0, LANES - low_rank), (0, 0)))
    wts = []
    for l in range(depth):
        wr_hi, wr_lo = _split_bf16(w_router[l])
        wup_hi, wup_lo = _split_bf16(wup[l])
        wts.append(dict(
            norm_mix=norm_mix_w[l], norm_ffn=norm_ffn_w[l],
            w_ret=w_in[l, :, :n_ret].astype(BF16),
            w_gla=jnp.pad(w_in[l, :, n_ret:], ((0, 0), (0, code_pad))).astype(BF16),
            wup_hi=wup_hi, wup_lo=wup_lo, b_gla=b_gla[l].reshape(1, -1),
            ret_norm=ret_norm_w[l], gla_norm=gla_norm_w[l],
            w_branch=w_branch[l].astype(BF16), w_merge=w_merge[l].astype(BF16), b_merge=b_merge[l],
            w_o=w_o[l].astype(BF16), wr_hi=wr_hi, wr_lo=wr_lo, b_r=b_router[l].reshape(1, LANES),
            w_exp_gate=w_exp_gate[l].astype(BF16), w_exp_up=w_exp_up[l].astype(BF16),
            w_exp_down=w_exp_down[l].astype(BF16), final_norm=final_norm_w,
        ))

    nb, ns = c_prompt.shape[0], c_sample.shape[0]
    mod = _ada(jnp.concatenate([c_prompt, c_sample], axis=0), w_ada, b_ada)
    mod = mod.reshape(depth, nb + ns, 6, d)
    mods_p = [mod[:, :nb, i][:, :, None, :] for i in range(6)]
    mods_s = [mod[:, nb:, i][:, :, None, :] for i in range(6)]

    zero_ret = jnp.zeros((depth, nb, rh, rdk, rdv), F32)
    zero_gla_t = jnp.zeros((depth, nb, gh, gdv, gdk), F32)
    y_p, ret_p, gla_p = _trunk(x_prompt, mods_p, 0, zero_ret, zero_gla_t, wts, dims)
    y_s, ret_s, gla_s = _trunk(x_sample, mods_s, PAST_LEN, state_ret.astype(F32),
                               jnp.swapaxes(state_gla.astype(F32), -1, -2), wts, dims)
    return (y_p, y_s, ret_p, gla_p, ret_s, gla_s)
```

```python
import functools

import jax
import jax.numpy as jnp
from jax import lax
from jax.experimental import pallas as pl
from jax.experimental.pallas import tpu as pltpu

F32 = jnp.float32
BF16 = jnp.bfloat16
U32 = jnp.uint32
I32 = jnp.int32

PAST_LEN = 4096
GLA_TAU = 16.0
ROPE_BASE = 10000.0
EPS = 1e-6
RET_DECAY_LOG2_BASE = -5.0
TOP_K = 2

LANES = 128
SUBLANES = 8
V7X_VMEM_BYTES = 64 * 1024 * 1024
VMEM_LIMIT = V7X_VMEM_BYTES - 8 * 1024 * 1024

ROWS_MATMUL = 1024
ROWS_PROJ = 512
ROWS_OUT_PROJ = 512
ROWS_NORM = 512
ROWS_COMBINE = 256
ROWS_RANK = 512
GATHER_UNROLL = 8
CAST_BLOCK_COLS = 1024
CAST_BLOCK_ELEMS = 2 * 1024 * 1024
ROWS_FFN = 512
CHUNK_RET = 256
CHUNK_GLA = 256


def _params(*sem):
    return pltpu.CompilerParams(dimension_semantics=sem, vmem_limit_bytes=VMEM_LIMIT)


def _tile(n, pref, align):
    if n <= pref:
        return n
    t = (pref // align) * align
    while t >= align:
        if n % t == 0:
            return t
        t -= align
    return n


def _seq_tiles(s, t, rows):
    if t >= rows:
        return 1, _tile(t, rows, 2 * SUBLANES)
    return _tile(s, max(rows // t, 1), 1), t


def _dot(a, b):
    return jnp.dot(a, b, preferred_element_type=F32)


def _dot_nt(a, b):
    return lax.dot_general(a, b, (((1,), (1,)), ((), ())), preferred_element_type=F32)


def _dot_tn(a, b):
    return lax.dot_general(a, b, (((0,), (0,)), ((), ())), preferred_element_type=F32)


def _sigmoid(x):
    return 0.5 * jnp.tanh(0.5 * x) + 0.5


def _silu(x):
    return x * _sigmoid(x)


def _log_sigmoid(z):
    return jnp.minimum(z, 0.0) - jnp.log(1.0 + jnp.exp(-jnp.abs(z)))


def _split_bf16(x):
    hi = x.astype(BF16)
    lo = (x - hi.astype(F32)).astype(BF16)
    return hi, lo


def _dot_split(a, w_hi, w_lo):
    a_hi, a_lo = _split_bf16(a)
    return _dot(a_hi, w_hi) + _dot(a_lo, w_hi) + _dot(a_hi, w_lo)


def _ada_kernel(c_ref, w_ref, b_ref, o_ref):
    a = _silu(c_ref[...]).astype(BF16)
    o_ref[...] = _dot(a, w_ref[...].astype(BF16)) + b_ref[...]


def _ada(c_all, w_ada, b_ada):
    depth, d, n6 = w_ada.shape
    r = c_all.shape[0]
    tn = _tile(n6, 1024, LANES)
    return pl.pallas_call(
        _ada_kernel,
        out_shape=jax.ShapeDtypeStruct((depth, r, n6), F32),
        grid=(depth, n6 // tn),
        in_specs=[
            pl.BlockSpec((r, d), lambda l, j: (0, 0)),
            pl.BlockSpec((None, d, tn), lambda l, j: (l, 0, j)),
            pl.BlockSpec((None, 1, tn), lambda l, j: (l, 0, j)),
        ],
        out_specs=pl.BlockSpec((None, r, tn), lambda l, j: (l, 0, j)),
        compiler_params=_params("arbitrary", "arbitrary"),
        name="ada_mod",
    )(c_all, w_ada, b_ada.reshape(depth, 1, n6))


def _norm_kernel(x_ref, w_ref, sc_ref, sh_ref, o_ref):
    x = x_ref[...]
    y = x * lax.rsqrt(jnp.mean(x * x, axis=-1, keepdims=True) + EPS) * w_ref[...]
    o_ref[...] = (y * (1.0 + sc_ref[...]) + sh_ref[...]).astype(o_ref.dtype)


def _norm(x, w, sc, sh):
    s, t, d = x.shape
    bs, tt = _seq_tiles(s, t, ROWS_NORM)
    return pl.pallas_call(
        _norm_kernel,
        out_shape=jax.ShapeDtypeStruct((s, t, d), BF16),
        grid=(s // bs, t // tt),
        in_specs=[
            pl.BlockSpec((bs, tt, d), lambda i, j: (i, j, 0)),
            pl.BlockSpec((1, d), lambda i, j: (0, 0)),
            pl.BlockSpec((bs, 1, d), lambda i, j: (i, 0, 0)),
            pl.BlockSpec((bs, 1, d), lambda i, j: (i, 0, 0)),
        ],
        out_specs=pl.BlockSpec((bs, tt, d), lambda i, j: (i, j, 0)),
        compiler_params=_params("arbitrary", "arbitrary"),
        name="mod_norm",
    )(x, w.reshape(1, d), sc, sh)


def _resident(shape, layer=None):
    if layer is None:
        return pl.BlockSpec(shape, lambda i: (0,) * len(shape), pipeline_mode=pl.Buffered(1))
    return pl.BlockSpec((None,) + tuple(shape), lambda i: (layer,) + (0,) * len(shape), pipeline_mode=pl.Buffered(1))


def _cast_kernel(x_ref, o_ref, *, valid_cols):
    x = x_ref[...]
    if valid_cols is not None:
        x = jnp.where(lax.broadcasted_iota(I32, x.shape, 1) < valid_cols, x, 0.0)
    o_ref[...] = x.astype(o_ref.dtype)


def _to_bf16(w, col0=0, ncols=None, valid_cols=None):
    depth, rows, cols = w.shape
    ncols = cols - col0 if ncols is None else ncols
    tc = LANES
    while tc * 2 <= CAST_BLOCK_COLS and col0 % (tc * 2) == 0 and ncols % (tc * 2) == 0:
        tc *= 2
    assert col0 % tc == 0 and ncols % tc == 0 and (valid_cols is None or ncols == tc)
    tr = _tile(rows, max(CAST_BLOCK_ELEMS // tc, 2 * SUBLANES), 2 * SUBLANES)
    c0 = col0 // tc
    return pl.pallas_call(
        functools.partial(_cast_kernel, valid_cols=valid_cols),
        out_shape=jax.ShapeDtypeStruct((depth, rows, ncols), BF16),
        grid=(depth, rows // tr, ncols // tc),
        in_specs=[pl.BlockSpec((None, tr, tc), lambda l, i, j: (l, i, c0 + j))],
        out_specs=pl.BlockSpec((None, tr, tc), lambda l, i, j: (l, i, j)),
        compiler_params=_params("arbitrary", "arbitrary", "arbitrary"),
        name="cast_bf16",
    )(w)


def _proj_ret_kernel(h_ref, w_ref, cos_ref, sin_ref, o_ref, *, heads, dk, kscale):
    h = h_ref[...]
    tn = heads * dk
    cos = cos_ref[...]
    sin = sin_ref[...]
    for sec, mult in ((0, 1.0), (1, kscale)):
        acc = _dot(h, w_ref[:, sec * tn:(sec + 1) * tn])
        for hd in range(heads):
            a = acc[:, hd * dk:(hd + 1) * dk]
            r = a * cos + pltpu.roll(a, dk // 2, 1) * sin
            o_ref[:, sec * tn + hd * dk:sec * tn + (hd + 1) * dk] = (r * mult).astype(o_ref.dtype)
    o_ref[:, 2 * tn:3 * tn] = _dot(h, w_ref[:, 2 * tn:3 * tn]).astype(o_ref.dtype)
    o_ref[:, 3 * tn:] = _silu(_dot(h, w_ref[:, 3 * tn:])).astype(o_ref.dtype)


def _proj_ret(h2d, w, layer, cos, sin, *, heads, dk, tm):
    n, d = h2d.shape
    tn = heads * dk
    assert w.shape[1:] == (d, 4 * tn) and cos.shape[0] % tm == 0
    nt = cos.shape[0] // tm
    return pl.pallas_call(
        functools.partial(_proj_ret_kernel, heads=heads, dk=dk, kscale=dk ** -0.5),
        out_shape=jax.ShapeDtypeStruct((n, 4 * tn), BF16),
        grid=(n // tm,),
        in_specs=[
            pl.BlockSpec((tm, d), lambda i: (i, 0)),
            _resident((d, 4 * tn), layer),
            pl.BlockSpec((tm, dk), lambda i: (i % nt, 0)),
            pl.BlockSpec((tm, dk), lambda i: (i % nt, 0)),
        ],
        out_specs=pl.BlockSpec((tm, 4 * tn), lambda i: (i, 0)),
        compiler_params=_params("arbitrary"),
        name="proj_ret",
    )(h2d, w, cos, sin)


def _proj_gla_kernel(h_ref, w_ref, wc_ref, wup_hi_ref, wup_lo_ref, bup_ref, qk_ref, vg_ref, la_ref, *, hdk, hdv, qscale):
    h = h_ref[...]
    qk = _dot(h, w_ref[:, :2 * hdk])
    qk_ref[:, :hdk] = qk[:, :hdk] * qscale
    qk_ref[:, hdk:] = qk[:, hdk:]
    vg_ref[:, :hdv] = _dot(h, w_ref[:, 2 * hdk:2 * hdk + hdv]).astype(vg_ref.dtype)
    vg_ref[:, hdv:] = _silu(_dot(h, w_ref[:, 2 * hdk + hdv:])).astype(vg_ref.dtype)
    z = _dot_split(_dot(h, wc_ref[...]), wup_hi_ref[...], wup_lo_ref[...])
    la_ref[...] = _log_sigmoid(z + bup_ref[...]) * (1.0 / GLA_TAU)


def _proj_gla(h2d, w, wc, layer, wup_hi, wup_lo, bup, *, hdk, hdv, dk, tm):
    n, d = h2d.shape
    assert w.shape[1:] == (d, 2 * hdk + 2 * hdv) and wc.shape[1:] == (d, LANES) and wup_hi.shape == (LANES, hdk)
    return pl.pallas_call(
        functools.partial(_proj_gla_kernel, hdk=hdk, hdv=hdv, qscale=dk ** -0.5),
        out_shape=(
            jax.ShapeDtypeStruct((n, 2 * hdk), F32),
            jax.ShapeDtypeStruct((n, 2 * hdv), BF16),
            jax.ShapeDtypeStruct((n, hdk), F32),
        ),
        grid=(n // tm,),
        in_specs=[
            pl.BlockSpec((tm, d), lambda i: (i, 0)),
            _resident(w.shape[1:], layer),
            _resident(wc.shape[1:], layer),
            _resident(wup_hi.shape),
            _resident(wup_lo.shape),
            _resident(bup.shape),
        ],
        out_specs=(
            pl.BlockSpec((tm, 2 * hdk), lambda i: (i, 0)),
            pl.BlockSpec((tm, 2 * hdv), lambda i: (i, 0)),
            pl.BlockSpec((tm, hdk), lambda i: (i, 0)),
        ),
        compiler_params=_params("arbitrary"),
        name="proj_gla",
    )(h2d, w, wc, wup_hi, wup_lo, bup)


def _ret_kernel(q_ref, k_ref, v_ref, g_ref, s0_ref, dm_ref, ind_ref, sd_ref, cd_ref, w_ref,
                o_ref, so_ref, st_ref, *, heads, dk, dv):
    c = pl.program_id(1)

    @pl.when(c == 0)
    def _():
        st_ref[...] = s0_ref[0]

    for hd in range(heads):
        q = q_ref[0, :, hd * dk:(hd + 1) * dk]
        k = k_ref[0, :, hd * dk:(hd + 1) * dk]
        v = v_ref[0, :, hd * dv:(hd + 1) * dv]
        s = st_ref[hd]
        p = (_dot_nt(q, k) * dm_ref[hd]).astype(BF16)
        o = _dot(p, v) + _dot(q, s.astype(BF16)) * ind_ref[hd]
        ks = (k.astype(F32) * sd_ref[hd]).astype(BF16)
        st_ref[hd] = s * cd_ref[hd] + _dot_tn(ks, v)
        oc = o - jnp.mean(o, axis=-1, keepdims=True)
        on = oc * lax.rsqrt(jnp.mean(oc * oc, axis=-1, keepdims=True) + EPS)
        gate = g_ref[0, :, hd * dv:(hd + 1) * dv].astype(F32)
        o_ref[0, :, hd * dv:(hd + 1) * dv] = (gate * (on * w_ref[:, hd * dv:(hd + 1) * dv])).astype(o_ref.dtype)

    @pl.when(c == pl.num_programs(1) - 1)
    def _():
        so_ref[0] = st_ref[...]


def _ret_tables(heads, c, dk, dv):
    log_gamma = jnp.log1p(-jnp.exp2(RET_DECAY_LOG2_BASE - jnp.arange(heads, dtype=F32)))
    idx = jnp.arange(c, dtype=F32)
    diff = idx[:, None] - idx[None, :]
    causal = diff >= 0
    dmask = jnp.where(causal[None], jnp.exp(log_gamma[:, None, None] * jnp.where(causal, diff, 0.0)[None]), 0.0)
    inner = jnp.exp(log_gamma[:, None] * (idx + 1.0))
    sdecay = jnp.exp(log_gamma[:, None] * (c - 1.0 - idx))
    cdecay = jnp.exp(log_gamma * c)
    return (dmask,
            jnp.broadcast_to(inner[:, :, None], (heads, c, dv)),
            jnp.broadcast_to(sdecay[:, :, None], (heads, c, dk)),
            jnp.broadcast_to(cdecay[:, None, None], (heads, 1, dv)))


def _retention(qkvg, s0, norm_w, *, heads, dk, dv):
    s, t, _ = qkvg.shape
    assert dk == dv
    c = _tile(t, CHUNK_RET, 2 * SUBLANES)
    w = heads * dk
    dm, ind, sd, cd = _ret_tables(heads, c, dk, dv)
    const3 = lambda b, i: (0, 0, 0)
    return pl.pallas_call(
        functools.partial(_ret_kernel, heads=heads, dk=dk, dv=dv),
        out_shape=(jax.ShapeDtypeStruct((s, t, w), BF16), jax.ShapeDtypeStruct(s0.shape, F32)),
        grid=(s, t // c),
        in_specs=[
            pl.BlockSpec((1, c, w), lambda b, i: (b, i, 0)),
            pl.BlockSpec((1, c, w), lambda b, i: (b, i, 1)),
            pl.BlockSpec((1, c, w), lambda b, i: (b, i, 2)),
            pl.BlockSpec((1, c, w), lambda b, i: (b, i, 3)),
            pl.BlockSpec((1, heads, dk, dv), lambda b, i: (b, 0, 0, 0)),
            pl.BlockSpec((heads, c, c), const3),
            pl.BlockSpec((heads, c, dv), const3),
            pl.BlockSpec((heads, c, dk), const3),
            pl.BlockSpec((heads, 1, dv), const3),
            pl.BlockSpec((1, w), lambda b, i: (0, 0)),
        ],
        out_specs=(
            pl.BlockSpec((1, c, w), lambda b, i: (b, i, 0)),
            pl.BlockSpec((1, heads, dk, dv), lambda b, i: (b, 0, 0, 0)),
        ),
        scratch_shapes=[pltpu.VMEM((heads, dk, dv), F32)],
        compiler_params=_params("arbitrary", "arbitrary"),
        name="retention",
    )(qkvg, qkvg, qkvg, qkvg, s0, dm, ind, sd, cd, norm_w.reshape(1, w))


def _gla_kernel(q_ref, k_ref, la_ref, v_ref, g_ref, s0_ref, w_ref, o_ref, so_ref, st_ref, *, heads, dk, dv, c):
    ci = pl.program_id(1)

    @pl.when(ci == 0)
    def _():
        st_ref[...] = s0_ref[0]

    row = lax.broadcasted_iota(I32, (c, dk), 0)
    ri = lax.broadcasted_iota(I32, (c, c), 0)
    cj = lax.broadcasted_iota(I32, (c, c), 1)
    for hd in range(heads):
        q = q_ref[0, :, hd * dk:(hd + 1) * dk]
        k = k_ref[0, :, hd * dk:(hd + 1) * dk]
        v = v_ref[0, :, hd * dv:(hd + 1) * dv]
        b = la_ref[0, :, hd * dk:(hd + 1) * dk]
        sh = 1
        while sh < c:
            b = b + jnp.where(row >= sh, pltpu.roll(b, sh, 0), 0.0)
            sh *= 2
        scores = jnp.where(ri == cj, _dot_nt(q.astype(BF16), k.astype(BF16)), 0.0)
        first = b
        half = 1
        while half < c:
            upper = (row & (2 * half - 1)) >= half
            mid = jnp.where(upper, first, pltpu.roll(first, c - half, 0))
            ql = jnp.where(upper, q * jnp.exp(jnp.minimum(b - mid, 0.0)), 0.0).astype(BF16)
            kl = jnp.where(upper, 0.0, k * jnp.exp(jnp.minimum(mid - b, 0.0))).astype(BF16)
            same = (ri & -(2 * half)) == (cj & -(2 * half))
            scores = scores + jnp.where(same, _dot_nt(ql, kl), 0.0)
            first = jnp.where(upper, pltpu.roll(first, half, 0), first)
            half *= 2
        st = st_ref[hd]
        o = _dot(scores.astype(BF16), v) + _dot_nt((q * jnp.exp(b)).astype(BF16), st.astype(BF16))
        b_last = b[c - 1:c, :]
        kd = (k * jnp.exp(b_last - b)).astype(BF16)
        st_ref[hd] = st * jnp.exp(b_last) + _dot_tn(v, kd)
        on = o * lax.rsqrt(jnp.mean(o * o, axis=-1, keepdims=True) + EPS)
        gate = g_ref[0, :, hd * dv:(hd + 1) * dv].astype(F32)
        o_ref[0, :, hd * dv:(hd + 1) * dv] = (gate * (on * w_ref[:, hd * dv:(hd + 1) * dv])).astype(o_ref.dtype)

    @pl.when(ci == pl.num_programs(1) - 1)
    def _():
        so_ref[0] = st_ref[...]


def _gla(qk, vg, la, s0t, norm_w, *, heads, dk, dv):
    s, t, _ = qk.shape
    c = _tile(t, CHUNK_GLA, 2 * SUBLANES)
    assert c & (c - 1) == 0, "chunk length must be a power of two"
    wk, wv = heads * dk, heads * dv
    return pl.pallas_call(
        functools.partial(_gla_kernel, heads=heads, dk=dk, dv=dv, c=c),
        out_shape=(jax.ShapeDtypeStruct((s, t, wv), BF16), jax.ShapeDtypeStruct(s0t.shape, F32)),
        grid=(s, t // c),
        in_specs=[
            pl.BlockSpec((1, c, wk), lambda b, i: (b, i, 0)),
            pl.BlockSpec((1, c, wk), lambda b, i: (b, i, 1)),
            pl.BlockSpec((1, c, wk), lambda b, i: (b, i, 0)),
            pl.BlockSpec((1, c, wv), lambda b, i: (b, i, 0)),
            pl.BlockSpec((1, c, wv), lambda b, i: (b, i, 1)),
            pl.BlockSpec((1, heads, dv, dk), lambda b, i: (b, 0, 0, 0)),
            pl.BlockSpec((1, wv), lambda b, i: (0, 0)),
        ],
        out_specs=(
            pl.BlockSpec((1, c, wv), lambda b, i: (b, i, 0)),
            pl.BlockSpec((1, heads, dv, dk), lambda b, i: (b, 0, 0, 0)),
        ),
        scratch_shapes=[pltpu.VMEM((heads, dv, dk), F32)],
        compiler_params=_params("arbitrary", "arbitrary"),
        name="gla",
    )(qk, qk, la, vg, vg, s0t, norm_w.reshape(1, wv))


def _merge_kernel(br_ref, bg_ref, h_ref, wr_ref, wg_ref, wm0_ref, wm1_ref, bm0_ref, bm1_ref, o_ref):
    h = h_ref[...]
    g0 = _sigmoid(_dot(h, wm0_ref[...]) + bm0_ref[...])
    g1 = _sigmoid(_dot(h, wm1_ref[...]) + bm1_ref[...])
    y = g0 * _dot(br_ref[...], wr_ref[...]) + g1 * _dot(bg_ref[...], wg_ref[...])
    o_ref[...] = y.astype(o_ref.dtype)


def _merge(o_ret, o_gla, h2d, w_branch, w_merge, layer, b_merge, *, tm):
    n, d = h2d.shape
    wdt = o_ret.shape[1]
    tn = _tile(d, 512, LANES)
    nj = d // tn
    b2 = b_merge.reshape(1, 2 * d)
    return pl.pallas_call(
        _merge_kernel,
        out_shape=jax.ShapeDtypeStruct((n, d), BF16),
        grid=(n // tm, nj),
        in_specs=[
            pl.BlockSpec((tm, wdt), lambda i, j: (i, 0)),
            pl.BlockSpec((tm, wdt), lambda i, j: (i, 0)),
            pl.BlockSpec((tm, d), lambda i, j: (i, 0)),
            pl.BlockSpec((None, wdt, tn), lambda i, j: (2 * layer, 0, j)),
            pl.BlockSpec((None, wdt, tn), lambda i, j: (2 * layer + 1, 0, j)),
            pl.BlockSpec((None, d, tn), lambda i, j: (layer, 0, j)),
            pl.BlockSpec((None, d, tn), lambda i, j: (layer, 0, j + nj)),
            pl.BlockSpec((1, tn), lambda i, j: (0, j)),
            pl.BlockSpec((1, tn), lambda i, j: (0, j + nj)),
        ],
        out_specs=pl.BlockSpec((tm, tn), lambda i, j: (i, j)),
        compiler_params=_params("arbitrary", "arbitrary"),
        name="branch_merge",
    )(o_ret, o_gla, h2d, w_branch, w_branch, w_merge, w_merge, b2, b2)


def _pack_bf16_pairs(h):
    half = h.shape[-1] // 2
    a = lax.bitcast_convert_type(h[:, :half].astype(BF16).astype(F32), U32)
    b = lax.bitcast_convert_type(h[:, half:].astype(BF16).astype(F32), U32)
    return a | (b >> 16)


def _unpack_bf16_pairs(w):
    a = lax.bitcast_convert_type(w & jnp.uint32(0xFFFF0000), F32)
    b = lax.bitcast_convert_type(w << 16, F32)
    return jnp.concatenate([a, b], axis=-1).astype(BF16)


def _route(logits, n_groups, per_group):
    lane = lax.broadcasted_iota(I32, logits.shape, 1).astype(F32)
    neg = jnp.float32(-jnp.inf)

    def first_max(mask):
        m = jnp.max(jnp.where(mask, logits, neg), axis=-1, keepdims=True)
        idx = jnp.min(jnp.where(mask & (logits == m), lane, float(LANES)), axis=-1, keepdims=True)
        return m, idx

    gmask = lane < n_groups
    gmax, gidx = first_max(gmask)
    p_group = 1.0 / jnp.sum(jnp.where(gmask, jnp.exp(logits - gmax), 0.0), axis=-1, keepdims=True)
    lo = n_groups + gidx * per_group
    emask = (lane >= lo) & (lane < lo + per_group)
    m1, i1 = first_max(emask)
    m2, i2 = first_max(emask & (lane != i1))
    w1 = 1.0 / (1.0 + jnp.exp(m2 - m1))
    w2 = 1.0 - w1
    return jnp.where(lane == 0.0, i1 - n_groups, jnp.where(lane == 1.0, i2 - n_groups, jnp.where(
        lane == 2.0, p_group * w1, jnp.where(lane == 3.0, p_group * w2, 0.0))))


def _out_proj_kernel(m_ref, x_ref, g1_ref, sc_ref, sh_ref, nw_ref, wo_ref, wr_ref, br_ref,
                     x1_ref, hp_ref, rt_ref, *, n_groups, per_group):
    bs, tt, d = x_ref.shape
    y = _dot(m_ref[...].reshape(bs * tt, d), wo_ref[...]).reshape(bs, tt, d)
    x1 = x_ref[...] + g1_ref[...] * y
    x1_ref[...] = x1
    hn = x1 * lax.rsqrt(jnp.mean(x1 * x1, axis=-1, keepdims=True) + EPS) * nw_ref[...]
    h = (hn * (1.0 + sc_ref[...]) + sh_ref[...]).reshape(bs * tt, d)
    hp_ref[...] = _pack_bf16_pairs(h).reshape(bs, tt, d // 2)
    h_hi, h_lo = _split_bf16(h)
    p = _dot(h_hi, wr_ref[...])
    logits = p[:, :LANES] + p[:, LANES:] + _dot(h_lo, wr_ref[:, :LANES]) + br_ref[...]
    rt_ref[...] = _route(logits, n_groups, per_group).reshape(bs, tt, LANES)


def _out_proj(merged, x, g1, sc2, sh2, norm_w, w_o, layer, wr, b_r, *, n_groups, per_group):
    s, t, d = x.shape
    bs, tt = _seq_tiles(s, t, ROWS_OUT_PROJ)
    tok = lambda i, j: (i, j, 0)
    seq = lambda i, j: (i, 0, 0)
    const = lambda i, j: (0, 0)
    return pl.pallas_call(
        functools.partial(_out_proj_kernel, n_groups=n_groups, per_group=per_group),
        out_shape=(
            jax.ShapeDtypeStruct((s, t, d), F32),
            jax.ShapeDtypeStruct((s, t, d // 2), U32),
            jax.ShapeDtypeStruct((s, t, LANES), F32),
        ),
        grid=(s // bs, t // tt),
        in_specs=[
            pl.BlockSpec((bs, tt, d), tok),
            pl.BlockSpec((bs, tt, d), tok),
            pl.BlockSpec((bs, 1, d), seq),
            pl.BlockSpec((bs, 1, d), seq),
            pl.BlockSpec((bs, 1, d), seq),
            pl.BlockSpec((1, d), const),
            pl.BlockSpec((None, d, d), lambda i, j: (layer, 0, 0), pipeline_mode=pl.Buffered(1)),
            pl.BlockSpec((d, 2 * LANES), const, pipeline_mode=pl.Buffered(1)),
            pl.BlockSpec((1, LANES), const),
        ],
        out_specs=(
            pl.BlockSpec((bs, tt, d), tok),
            pl.BlockSpec((bs, tt, d // 2), tok),
            pl.BlockSpec((bs, tt, LANES), tok),
        ),
        compiler_params=_params("arbitrary", "arbitrary"),
        name="out_proj_router",
    )(merged, x, g1, sc2, sh2, norm_w.reshape(1, d), w_o, wr, b_r)


def _rank_kernel(rt_ref, rank_ref, cnt_ref, carry_ref):
    i = pl.program_id(0)

    @pl.when(i == 0)
    def _():
        carry_ref[...] = jnp.zeros_like(carry_ref)

    r = rt_ref[...]
    tr = r.shape[0]
    lane = lax.broadcasted_iota(I32, r.shape, 1).astype(F32)
    a1 = lane == r[:, 0:1]
    a2 = lane == r[:, 1:2]
    hit = jnp.where(a1 | a2, 1.0, 0.0)
    ri = lax.broadcasted_iota(I32, (tr, tr), 0)
    cj = lax.broadcasted_iota(I32, (tr, tr), 1)
    before = _dot(jnp.where(ri > cj, 1.0, 0.0).astype(BF16), hit.astype(BF16)) + carry_ref[...]
    k1 = jnp.sum(jnp.where(a1, before, 0.0), axis=-1, keepdims=True)
    k2 = jnp.sum(jnp.where(a2, before, 0.0), axis=-1, keepdims=True)
    rank_ref[...] = jnp.where(lane == 0.0, k1, jnp.where(lane == 1.0, k2, 0.0))
    carry_ref[...] += jnp.sum(hit, axis=0, keepdims=True)
    cnt_ref[...] = carry_ref[...]


def _rank(route2d):
    n = route2d.shape[0]
    tr = _tile(n, ROWS_RANK, SUBLANES)
    return pl.pallas_call(
        _rank_kernel,
        out_shape=(jax.ShapeDtypeStruct((n, LANES), F32), jax.ShapeDtypeStruct((1, LANES), F32)),
        grid=(n // tr,),
        in_specs=[pl.BlockSpec((tr, LANES), lambda i: (i, 0))],
        out_specs=(pl.BlockSpec((tr, LANES), lambda i: (i, 0)), pl.BlockSpec((1, LANES), lambda i: (0, 0))),
        scratch_shapes=[pltpu.VMEM((1, LANES), F32)],
        compiler_params=_params("arbitrary"),
        name="moe_rank",
    )(route2d)


def _row_copy(src_ref, src_row, dst_ref, dst_row, sem):
    return pltpu.make_async_copy(src_ref.at[pl.ds(src_row, 1)], dst_ref.at[pl.ds(dst_row, 1)], sem)


def _ffn_kernel(te_ref, valid_ref, src_ref, hp_ref, wg_ref, wu_ref, wd_ref, ys_ref, xbuf, sems, *, tf):
    del te_ref
    i = pl.program_id(0)
    last = pl.num_programs(0) - 1
    slot = i % 2
    nxt = jnp.where(i < last, i + 1, 0)

    def start(tile, slt, r):
        _row_copy(hp_ref, src_ref[tile * tf + r], xbuf.at[slt], r, sems.at[slt]).start()

    def gather_rolled(tile, slt):
        def issue(r, carry):
            start(tile, slt, r)
            return carry

        lax.fori_loop(0, tf, issue, 0, unroll=GATHER_UNROLL)

    def drain(slt):
        def wait(r, carry):
            _row_copy(hp_ref, 0, xbuf.at[slt], 0, sems.at[slt]).wait()
            return carry

        lax.fori_loop(0, tf, wait, 0, unroll=GATHER_UNROLL)

    @pl.when(i == 0)
    def _():
        gather_rolled(0, 0)

    drain(slot)

    @pl.when(valid_ref[i] != 0)
    def _():
        def starts(part):
            for r in range(part * tf // 3, (part + 1) * tf // 3):
                start(nxt, 1 - slot, r)

        x = _unpack_bf16_pairs(xbuf[slot])
        starts(0)
        gate = _silu(_dot(x, wg_ref[...]))
        starts(1)
        act = (gate * _dot(x, wu_ref[...])).astype(BF16)
        starts(2)
        ys_ref[...] = _dot(act, wd_ref[...])

    @pl.when(valid_ref[i] == 0)
    def _():
        gather_rolled(nxt, 1 - slot)
        ys_ref[...] = jnp.zeros_like(ys_ref)

    @pl.when(i == last)
    def _():
        drain(1 - slot)


def _ffn(tile_expert, tile_valid, src, hp2d, wg, wu, wd, *, tf):
    n_rows = src.shape[0]
    half = hp2d.shape[1]
    _, d, f = wg.shape
    return pl.pallas_call(
        functools.partial(_ffn_kernel, tf=tf),
        out_shape=jax.ShapeDtypeStruct((n_rows, d), F32),
        grid_spec=pltpu.PrefetchScalarGridSpec(
            num_scalar_prefetch=3,
            grid=(n_rows // tf,),
            in_specs=[
                pl.BlockSpec(memory_space=pl.ANY),
                pl.BlockSpec((None, d, f), lambda i, te, tv, sr: (te[i], 0, 0)),
                pl.BlockSpec((None, d, f), lambda i, te, tv, sr: (te[i], 0, 0)),
                pl.BlockSpec((None, f, d), lambda i, te, tv, sr: (te[i], 0, 0)),
            ],
            out_specs=pl.BlockSpec((tf, d), lambda i, te, tv, sr: (i, 0)),
            scratch_shapes=[pltpu.VMEM((2, tf, half), hp2d.dtype), pltpu.SemaphoreType.DMA((2,))],
        ),
        compiler_params=_params("arbitrary"),
        name="moe_ffn",
    )(tile_expert, tile_valid, src, hp2d, wg, wu, wd)


def _combine_kernel(pos_ref, x_ref, rt_ref, g2_ref, fw_ref, ys_ref, o_ref, ybuf, sems, *, final_norm):
    bs, tt, d = x_ref.shape
    rows = bs * tt
    step = pl.program_id(0) * pl.num_programs(1) + pl.program_id(1)
    n_steps = pl.num_programs(0) * pl.num_programs(1)

    def start(stp, slt, r, kk):
        _row_copy(ys_ref, pos_ref[TOP_K * (stp * rows + r) + kk], ybuf.at[slt, kk], r, sems.at[slt]).start()

    @pl.when(step == 0)
    def _():
        def issue(r, carry):
            for kk in range(TOP_K):
                start(0, 0, r, kk)
            return carry

        lax.fori_loop(0, rows, issue, 0, unroll=GATHER_UNROLL)

    slot = step % 2

    @pl.when(step + 1 < n_steps)
    def _():
        for r in range(rows):
            for kk in range(TOP_K):
                start(step + 1, 1 - slot, r, kk)

    def wait(r, carry):
        for kk in range(TOP_K):
            _row_copy(ys_ref, 0, ybuf.at[slot, kk], 0, sems.at[slot]).wait()
        return carry

    lax.fori_loop(0, rows, wait, 0, unroll=GATHER_UNROLL)

    rt = rt_ref[...].reshape(rows, LANES)
    y = rt[:, 2:3] * ybuf[slot, 0] + rt[:, 3:4] * ybuf[slot, 1]
    xn = x_ref[...] + g2_ref[...] * y.reshape(bs, tt, d)
    if final_norm:
        xn = xn * lax.rsqrt(jnp.mean(xn * xn, axis=-1, keepdims=True) + EPS) * fw_ref[...]
    o_ref[...] = xn


def _combine(pos, x1, route, g2, final_w, ys, *, final_norm):
    s, t, d = x1.shape
    bs, tt = _seq_tiles(s, t, ROWS_COMBINE)
    tok = lambda i, j, p: (i, j, 0)
    return pl.pallas_call(
        functools.partial(_combine_kernel, final_norm=final_norm),
        out_shape=jax.ShapeDtypeStruct((s, t, d), F32),
        grid_spec=pltpu.PrefetchScalarGridSpec(
            num_scalar_prefetch=1,
            grid=(s // bs, t // tt),
            in_specs=[
                pl.BlockSpec((bs, tt, d), tok),
                pl.BlockSpec((bs, tt, LANES), tok),
                pl.BlockSpec((bs, 1, d), lambda i, j, p: (i, 0, 0)),
                pl.BlockSpec((1, d), lambda i, j, p: (0, 0)),
                pl.BlockSpec(memory_space=pl.ANY),
            ],
            out_specs=pl.BlockSpec((bs, tt, d), tok),
            scratch_shapes=[pltpu.VMEM((2, TOP_K, bs * tt, d), F32), pltpu.SemaphoreType.DMA((2,))],
        ),
        compiler_params=_params("arbitrary", "arbitrary"),
        name="moe_combine",
    )(pos, x1, route, g2, final_w.reshape(1, d), ys)


def _moe(x1, hp, route, g2, wg, wu, wd, layer, n_exp, final_w, *, final_norm):
    s, t, d = x1.shape
    n = s * t
    rank, counts = _rank(route.reshape(n, LANES))
    tf = 2 * SUBLANES
    while tf < ROWS_FFN and tf < 2 * TOP_K * n // n_exp:
        tf *= 2
    n_tiles = (TOP_K * n) // tf + n_exp
    cnt = counts[0, :n_exp].astype(I32)
    padded = ((cnt + tf - 1) // tf) * tf
    ends = jnp.cumsum(padded)
    starts = ends - padded
    expert = route.reshape(n, LANES)[:, :TOP_K].astype(I32)
    start_of = jnp.sum(jnp.where(expert[..., None] == jnp.arange(n_exp, dtype=I32), starts, 0), axis=-1)
    pos = (start_of + rank[:, :TOP_K].astype(I32)).reshape(TOP_K * n)
    tile_start = jnp.arange(n_tiles, dtype=I32) * tf
    tile_valid = (tile_start < ends[-1]).astype(I32)
    tile_expert = jnp.minimum(jnp.sum((ends[None, :] <= tile_start[:, None]).astype(I32), axis=1), n_exp - 1)
    last_used = jnp.max(jnp.where(tile_valid != 0, tile_expert, 0))
    tile_expert = jnp.where(tile_valid != 0, tile_expert, last_used)
    token_of = jnp.arange(TOP_K * n, dtype=I32) // TOP_K
    src = jnp.zeros((n_tiles * tf,), I32).at[pos].set(token_of, unique_indices=True)
    ys = _ffn(tile_expert + layer * n_exp, tile_valid, src, hp.reshape(n, d // 2), wg, wu, wd, tf=tf)
    return _combine(pos, x1, route, g2, final_w, ys, final_norm=final_norm)


def _rotary_tables(pos0, t, dk, reps):
    half = dk // 2
    inv = ROPE_BASE ** (-jnp.arange(half, dtype=F32) / half)
    ang = (pos0 + jnp.arange(t)).astype(F32)[:, None] * inv[None, :]
    cos = jnp.cos(ang)
    sin = jnp.sin(ang)
    cos2 = jnp.concatenate([cos, cos], axis=-1)
    sin2 = jnp.concatenate([-sin, sin], axis=-1)
    return jnp.tile(cos2, (reps, 1)), jnp.tile(sin2, (reps, 1))


def _trunk(x, mods, pos0, s_ret_in, s_gla_in_t, big, wts, dims):
    s, t, d = x.shape
    n = s * t
    depth = len(wts)
    rh, rdk, rdv, gh, gdk, gdv = dims["ret_heads"], dims["ret_dk"], dims["ret_dv"], dims["gla_heads"], dims["gla_dk"], dims["gla_dv"]
    bs, tt = _seq_tiles(s, t, ROWS_PROJ)
    tp = bs * tt
    tm = _tile(n, ROWS_MATMUL, 2 * SUBLANES)
    cos, sin = _rotary_tables(pos0, t, rdk, bs)
    ret_states, gla_states = [], []
    for l in range(depth):
        w = wts[l]
        sh1, sc1, g1, sh2, sc2, g2 = (m[l] for m in mods)
        h2d = _norm(x, w["norm_mix"], sc1, sh1).reshape(n, d)
        qkvg = _proj_ret(h2d, big["w_ret"], l, cos, sin, heads=rh, dk=rdk, tm=tp)
        qk, vg, la = _proj_gla(h2d, big["w_gla"], big["w_code"], l, w["wup_hi"], w["wup_lo"], w["b_gla"],
                               hdk=gh * gdk, hdv=gh * gdv, dk=gdk, tm=tp)
        o_ret, s_r = _retention(qkvg.reshape(s, t, -1), s_ret_in[l], w["ret_norm"], heads=rh, dk=rdk, dv=rdv)
        o_gla, s_g = _gla(qk.reshape(s, t, -1), vg.reshape(s, t, -1), la.reshape(s, t, -1), s_gla_in_t[l],
                          w["gla_norm"], heads=gh, dk=gdk, dv=gdv)
        merged = _merge(o_ret.reshape(n, -1), o_gla.reshape(n, -1), h2d, big["w_branch"], big["w_merge"], l,
                        w["b_merge"], tm=tm)
        x1, hp, route = _out_proj(merged.reshape(s, t, d), x, g1, sc2, sh2, w["norm_ffn"], big["w_o"], l,
                                  w["w_router"], w["b_r"], n_groups=dims["n_groups"], per_group=dims["per_group"])
        x = _moe(x1, hp, route, g2, big["w_exp_gate"], big["w_exp_up"], big["w_exp_down"], l,
                 dims["n_groups"] * dims["per_group"], w["final_norm"], final_norm=(l == depth - 1))
        ret_states.append(s_r)
        gla_states.append(jnp.swapaxes(s_g, -1, -2))
    return x, jnp.stack(ret_states), jnp.stack(gla_states)


def kernel(x_prompt, x_sample, state_ret, state_gla, c_prompt, c_sample, w_ada, b_ada, norm_mix_w, norm_ffn_w, w_in, w_gla_up, b_gla, ret_norm_w, gla_norm_w, w_branch, w_merge, b_merge, w_o, w_router_group, b_router_group, w_router_expert, b_router_expert, w_exp_gate, w_exp_up, w_exp_down, final_norm_w):
    depth, d, _ = w_in.shape
    _, _, rh, rdk, rdv = state_ret.shape
    _, _, gh, gdk, gdv = state_gla.shape
    low_rank = w_gla_up.shape[1]
    n_groups = w_router_group.shape[-1]
    n_exp = w_router_expert.shape[-1]
    assert low_rank <= LANES and n_groups + n_exp <= LANES and n_exp % n_groups == 0
    dims = dict(ret_heads=rh, ret_dk=rdk, ret_dv=rdv, gla_heads=gh, gla_dk=gdk, gla_dv=gdv,
                n_groups=n_groups, per_group=n_exp // n_groups)

    n_ret = 2 * rh * rdk + 2 * rh * rdv
    n_gla = 2 * gh * gdk + 2 * gh * gdv
    assert w_in.shape[-1] == n_ret + n_gla + low_rank
    f = w_exp_gate.shape[-1]
    big = dict(
        w_ret=_to_bf16(w_in, 0, n_ret),
        w_gla=_to_bf16(w_in, n_ret, n_gla),
        w_code=_to_bf16(w_in, n_ret + n_gla, LANES, valid_cols=low_rank),
        w_branch=_to_bf16(w_branch.reshape(depth, -1, d)).reshape(depth * w_branch.shape[1], -1, d),
        w_merge=_to_bf16(w_merge),
        w_o=_to_bf16(w_o),
        w_exp_gate=_to_bf16(w_exp_gate.reshape(depth, n_exp * d, f)).reshape(depth * n_exp, d, f),
        w_exp_up=_to_bf16(w_exp_up.reshape(depth, n_exp * d, f)).reshape(depth * n_exp, d, f),
        w_exp_down=_to_bf16(w_exp_down.reshape(depth, n_exp * f, d)).reshape(depth * n_exp, f, d),
    )
    w_router = jnp.concatenate([w_router_group, w_router_expert], axis=-1)
    w_router = jnp.pad(w_router, ((0, 0), (0, 0), (0, LANES - n_groups - n_exp)))
    b_router = jnp.pad(jnp.concatenate([b_router_group, b_router_expert], axis=-1), ((0, 0), (0, LANES - n_groups - n_exp)))
    wup = jnp.pad(w_gla_up, ((0, 0), (0, LANES - low_rank), (0, 0)))
    wts = []
    for l in range(depth):
        wr_hi, wr_lo = _split_bf16(w_router[l])
        wup_hi, wup_lo = _split_bf16(wup[l])
        wts.append(dict(
            norm_mix=norm_mix_w[l], norm_ffn=norm_ffn_w[l],
            wup_hi=wup_hi, wup_lo=wup_lo, b_gla=b_gla[l].reshape(1, -1),
            ret_norm=ret_norm_w[l], gla_norm=gla_norm_w[l], b_merge=b_merge[l],
            w_router=jnp.concatenate([wr_hi, wr_lo], axis=-1), b_r=b_router[l].reshape(1, LANES),
            final_norm=final_norm_w,
        ))

    nb, ns = c_prompt.shape[0], c_sample.shape[0]
    mod = _ada(jnp.concatenate([c_prompt, c_sample], axis=0), w_ada, b_ada)
    mod = mod.reshape(depth, nb + ns, 6, d)
    mods_p = [mod[:, :nb, i][:, :, None, :] for i in range(6)]
    mods_s = [mod[:, nb:, i][:, :, None, :] for i in range(6)]

    zero_ret = jnp.zeros((depth, nb, rh, rdk, rdv), F32)
    zero_gla_t = jnp.zeros((depth, nb, gh, gdv, gdk), F32)
    y_p, ret_p, gla_p = _trunk(x_prompt, mods_p, 0, zero_ret, zero_gla_t, big, wts, dims)
    y_s, ret_s, gla_s = _trunk(x_sample, mods_s, PAST_LEN, state_ret.astype(F32),
                               jnp.swapaxes(state_gla.astype(F32), -1, -2), big, wts, dims)
    return (y_p, y_s, ret_p, gla_p, ret_s, gla_s)
```

```python
import functools

import jax
import jax.numpy as jnp
from jax import lax
from jax.experimental import pallas as pl
from jax.experimental.pallas import tpu as pltpu

F32 = jnp.float32
BF16 = jnp.bfloat16
U32 = jnp.uint32
I32 = jnp.int32

PAST_LEN = 4096
GLA_TAU = 16.0
ROPE_BASE = 10000.0
EPS = 1e-6
RET_DECAY_LOG2_BASE = -5.0
TOP_K = 2

LANES = 128
SUBLANES = 8
V7X_VMEM_BYTES = 64 * 1024 * 1024
VMEM_LIMIT = V7X_VMEM_BYTES - 8 * 1024 * 1024

ROWS_MATMUL = 1024
ROWS_PROJ = 512
ROWS_OUT_PROJ = 512
ROWS_NORM = 512
ROWS_COMBINE = 256
GATHER_UNROLL = 8
INVERT_UNROLL = 16
CAST_BLOCK_COLS = 1024
CAST_BLOCK_ELEMS = 2 * 1024 * 1024
ROWS_FFN = 512
CHUNK_RET = 256
CHUNK_GLA = 256


def _params(*sem):
    return pltpu.CompilerParams(dimension_semantics=sem, vmem_limit_bytes=VMEM_LIMIT)


def _tile(n, pref, align):
    if n <= pref:
        return n
    t = (pref // align) * align
    while t >= align:
        if n % t == 0:
            return t
        t -= align
    return n


def _seq_tiles(s, t, rows):
    if t >= rows:
        return 1, _tile(t, rows, 2 * SUBLANES)
    return _tile(s, max(rows // t, 1), 1), t


def _dot(a, b):
    return jnp.dot(a, b, preferred_element_type=F32)


def _dot_nt(a, b):
    return lax.dot_general(a, b, (((1,), (1,)), ((), ())), preferred_element_type=F32)


def _dot_tn(a, b):
    return lax.dot_general(a, b, (((0,), (0,)), ((), ())), preferred_element_type=F32)


def _sigmoid(x):
    return 0.5 * jnp.tanh(0.5 * x) + 0.5


def _silu(x):
    return x * _sigmoid(x)


def _log_sigmoid(z):
    return jnp.minimum(z, 0.0) - jnp.log(1.0 + jnp.exp(-jnp.abs(z)))


def _split_bf16(x):
    hi = x.astype(BF16)
    lo = (x - hi.astype(F32)).astype(BF16)
    return hi, lo


def _dot_split(a, w_hi, w_lo):
    a_hi, a_lo = _split_bf16(a)
    return _dot(a_hi, w_hi) + _dot(a_lo, w_hi) + _dot(a_hi, w_lo)


def _ada_kernel(c_ref, w_ref, b_ref, o_ref):
    a = _silu(c_ref[...]).astype(BF16)
    o_ref[...] = _dot(a, w_ref[...].astype(BF16)) + b_ref[...]


def _ada(c_all, w_ada, b_ada):
    depth, d, n6 = w_ada.shape
    r = c_all.shape[0]
    tn = _tile(n6, 1024, LANES)
    return pl.pallas_call(
        _ada_kernel,
        out_shape=jax.ShapeDtypeStruct((depth, r, n6), F32),
        grid=(depth, n6 // tn),
        in_specs=[
            pl.BlockSpec((r, d), lambda l, j: (0, 0)),
            pl.BlockSpec((None, d, tn), lambda l, j: (l, 0, j)),
            pl.BlockSpec((None, 1, tn), lambda l, j: (l, 0, j)),
        ],
        out_specs=pl.BlockSpec((None, r, tn), lambda l, j: (l, 0, j)),
        compiler_params=_params("arbitrary", "arbitrary"),
        name="ada_mod",
    )(c_all, w_ada, b_ada.reshape(depth, 1, n6))


def _norm_kernel(x_ref, w_ref, sc_ref, sh_ref, o_ref):
    x = x_ref[...]
    y = x * lax.rsqrt(jnp.mean(x * x, axis=-1, keepdims=True) + EPS) * w_ref[...]
    o_ref[...] = (y * (1.0 + sc_ref[...]) + sh_ref[...]).astype(o_ref.dtype)


def _norm(x, w, sc, sh):
    s, t, d = x.shape
    bs, tt = _seq_tiles(s, t, ROWS_NORM)
    return pl.pallas_call(
        _norm_kernel,
        out_shape=jax.ShapeDtypeStruct((s, t, d), BF16),
        grid=(s // bs, t // tt),
        in_specs=[
            pl.BlockSpec((bs, tt, d), lambda i, j: (i, j, 0)),
            pl.BlockSpec((1, d), lambda i, j: (0, 0)),
            pl.BlockSpec((bs, 1, d), lambda i, j: (i, 0, 0)),
            pl.BlockSpec((bs, 1, d), lambda i, j: (i, 0, 0)),
        ],
        out_specs=pl.BlockSpec((bs, tt, d), lambda i, j: (i, j, 0)),
        compiler_params=_params("arbitrary", "arbitrary"),
        name="mod_norm",
    )(x, w.reshape(1, d), sc, sh)


def _resident(shape, layer=None):
    if layer is None:
        return pl.BlockSpec(shape, lambda i: (0,) * len(shape), pipeline_mode=pl.Buffered(1))
    return pl.BlockSpec((None,) + tuple(shape), lambda i: (layer,) + (0,) * len(shape), pipeline_mode=pl.Buffered(1))


def _cast_kernel(x_ref, o_ref, *, valid_cols):
    x = x_ref[...]
    if valid_cols is not None:
        x = jnp.where(lax.broadcasted_iota(I32, x.shape, 1) < valid_cols, x, 0.0)
    o_ref[...] = x.astype(o_ref.dtype)


def _to_bf16(w, col0=0, ncols=None, valid_cols=None):
    depth, rows, cols = w.shape
    ncols = cols - col0 if ncols is None else ncols
    tc = LANES
    while tc * 2 <= CAST_BLOCK_COLS and col0 % (tc * 2) == 0 and ncols % (tc * 2) == 0:
        tc *= 2
    assert col0 % tc == 0 and ncols % tc == 0 and (valid_cols is None or ncols == tc)
    tr = _tile(rows, max(CAST_BLOCK_ELEMS // tc, 2 * SUBLANES), 2 * SUBLANES)
    c0 = col0 // tc
    return pl.pallas_call(
        functools.partial(_cast_kernel, valid_cols=valid_cols),
        out_shape=jax.ShapeDtypeStruct((depth, rows, ncols), BF16),
        grid=(depth, rows // tr, ncols // tc),
        in_specs=[pl.BlockSpec((None, tr, tc), lambda l, i, j: (l, i, c0 + j))],
        out_specs=pl.BlockSpec((None, tr, tc), lambda l, i, j: (l, i, j)),
        compiler_params=_params("arbitrary", "arbitrary", "arbitrary"),
        name="cast_bf16",
    )(w)


def _proj_ret_kernel(h_ref, w_ref, cos_ref, sin_ref, o_ref, *, heads, dk, kscale):
    h = h_ref[...]
    tn = heads * dk
    cos = cos_ref[...]
    sin = sin_ref[...]
    for sec, mult in ((0, 1.0), (1, kscale)):
        acc = _dot(h, w_ref[:, sec * tn:(sec + 1) * tn])
        for hd in range(heads):
            a = acc[:, hd * dk:(hd + 1) * dk]
            r = a * cos + pltpu.roll(a, dk // 2, 1) * sin
            o_ref[:, sec * tn + hd * dk:sec * tn + (hd + 1) * dk] = (r * mult).astype(o_ref.dtype)
    o_ref[:, 2 * tn:3 * tn] = _dot(h, w_ref[:, 2 * tn:3 * tn]).astype(o_ref.dtype)
    o_ref[:, 3 * tn:] = _silu(_dot(h, w_ref[:, 3 * tn:])).astype(o_ref.dtype)


def _proj_ret(h2d, w, layer, cos, sin, *, heads, dk, tm):
    n, d = h2d.shape
    tn = heads * dk
    assert w.shape[1:] == (d, 4 * tn) and cos.shape[0] % tm == 0
    nt = cos.shape[0] // tm
    return pl.pallas_call(
        functools.partial(_proj_ret_kernel, heads=heads, dk=dk, kscale=dk ** -0.5),
        out_shape=jax.ShapeDtypeStruct((n, 4 * tn), BF16),
        grid=(n // tm,),
        in_specs=[
            pl.BlockSpec((tm, d), lambda i: (i, 0)),
            _resident((d, 4 * tn), layer),
            pl.BlockSpec((tm, dk), lambda i: (i % nt, 0)),
            pl.BlockSpec((tm, dk), lambda i: (i % nt, 0)),
        ],
        out_specs=pl.BlockSpec((tm, 4 * tn), lambda i: (i, 0)),
        compiler_params=_params("arbitrary"),
        name="proj_ret",
    )(h2d, w, cos, sin)


def _proj_gla_kernel(h_ref, w_ref, wc_ref, wup_hi_ref, wup_lo_ref, bup_ref, qk_ref, vg_ref, la_ref, *, hdk, hdv, qscale):
    h = h_ref[...]
    qk = _dot(h, w_ref[:, :2 * hdk])
    qk_ref[:, :hdk] = qk[:, :hdk] * qscale
    qk_ref[:, hdk:] = qk[:, hdk:]
    vg_ref[:, :hdv] = _dot(h, w_ref[:, 2 * hdk:2 * hdk + hdv]).astype(vg_ref.dtype)
    vg_ref[:, hdv:] = _silu(_dot(h, w_ref[:, 2 * hdk + hdv:])).astype(vg_ref.dtype)
    z = _dot_split(_dot(h, wc_ref[...]), wup_hi_ref[...], wup_lo_ref[...])
    la_ref[...] = _log_sigmoid(z + bup_ref[...]) * (1.0 / GLA_TAU)


def _proj_gla(h2d, w, wc, layer, wup_hi, wup_lo, bup, *, hdk, hdv, dk, tm):
    n, d = h2d.shape
    assert w.shape[1:] == (d, 2 * hdk + 2 * hdv) and wc.shape[1:] == (d, LANES) and wup_hi.shape == (LANES, hdk)
    return pl.pallas_call(
        functools.partial(_proj_gla_kernel, hdk=hdk, hdv=hdv, qscale=dk ** -0.5),
        out_shape=(
            jax.ShapeDtypeStruct((n, 2 * hdk), F32),
            jax.ShapeDtypeStruct((n, 2 * hdv), BF16),
            jax.ShapeDtypeStruct((n, hdk), F32),
        ),
        grid=(n // tm,),
        in_specs=[
            pl.BlockSpec((tm, d), lambda i: (i, 0)),
            _resident(w.shape[1:], layer),
            _resident(wc.shape[1:], layer),
            _resident(wup_hi.shape),
            _resident(wup_lo.shape),
            _resident(bup.shape),
        ],
        out_specs=(
            pl.BlockSpec((tm, 2 * hdk), lambda i: (i, 0)),
            pl.BlockSpec((tm, 2 * hdv), lambda i: (i, 0)),
            pl.BlockSpec((tm, hdk), lambda i: (i, 0)),
        ),
        compiler_params=_params("arbitrary"),
        name="proj_gla",
    )(h2d, w, wc, wup_hi, wup_lo, bup)


def _ret_kernel(q_ref, k_ref, v_ref, g_ref, s0_ref, dm_ref, ind_ref, sd_ref, cd_ref, w_ref,
                o_ref, so_ref, st_ref, *, heads, dk, dv):
    c = pl.program_id(1)

    @pl.when(c == 0)
    def _():
        st_ref[...] = s0_ref[0]

    for hd in range(heads):
        q = q_ref[0, :, hd * dk:(hd + 1) * dk]
        k = k_ref[0, :, hd * dk:(hd + 1) * dk]
        v = v_ref[0, :, hd * dv:(hd + 1) * dv]
        s = st_ref[hd]
        p = (_dot_nt(q, k) * dm_ref[hd]).astype(BF16)
        o = _dot(p, v) + _dot(q, s.astype(BF16)) * ind_ref[hd]
        ks = (k.astype(F32) * sd_ref[hd]).astype(BF16)
        st_ref[hd] = s * cd_ref[hd] + _dot_tn(ks, v)
        oc = o - jnp.mean(o, axis=-1, keepdims=True)
        on = oc * lax.rsqrt(jnp.mean(oc * oc, axis=-1, keepdims=True) + EPS)
        gate = g_ref[0, :, hd * dv:(hd + 1) * dv].astype(F32)
        o_ref[0, :, hd * dv:(hd + 1) * dv] = (gate * (on * w_ref[:, hd * dv:(hd + 1) * dv])).astype(o_ref.dtype)

    @pl.when(c == pl.num_programs(1) - 1)
    def _():
        so_ref[0] = st_ref[...]


def _ret_tables(heads, c, dk, dv):
    log_gamma = jnp.log1p(-jnp.exp2(RET_DECAY_LOG2_BASE - jnp.arange(heads, dtype=F32)))
    idx = jnp.arange(c, dtype=F32)
    diff = idx[:, None] - idx[None, :]
    causal = diff >= 0
    dmask = jnp.where(causal[None], jnp.exp(log_gamma[:, None, None] * jnp.where(causal, diff, 0.0)[None]), 0.0)
    inner = jnp.exp(log_gamma[:, None] * (idx + 1.0))
    sdecay = jnp.exp(log_gamma[:, None] * (c - 1.0 - idx))
    cdecay = jnp.exp(log_gamma * c)
    return (dmask,
            jnp.broadcast_to(inner[:, :, None], (heads, c, dv)),
            jnp.broadcast_to(sdecay[:, :, None], (heads, c, dk)),
            jnp.broadcast_to(cdecay[:, None, None], (heads, 1, dv)))


def _retention(qkvg, s0, norm_w, *, heads, dk, dv):
    s, t, _ = qkvg.shape
    assert dk == dv
    c = _tile(t, CHUNK_RET, 2 * SUBLANES)
    w = heads * dk
    dm, ind, sd, cd = _ret_tables(heads, c, dk, dv)
    const3 = lambda b, i: (0, 0, 0)
    return pl.pallas_call(
        functools.partial(_ret_kernel, heads=heads, dk=dk, dv=dv),
        out_shape=(jax.ShapeDtypeStruct((s, t, w), BF16), jax.ShapeDtypeStruct(s0.shape, F32)),
        grid=(s, t // c),
        in_specs=[
            pl.BlockSpec((1, c, w), lambda b, i: (b, i, 0)),
            pl.BlockSpec((1, c, w), lambda b, i: (b, i, 1)),
            pl.BlockSpec((1, c, w), lambda b, i: (b, i, 2)),
            pl.BlockSpec((1, c, w), lambda b, i: (b, i, 3)),
            pl.BlockSpec((1, heads, dk, dv), lambda b, i: (b, 0, 0, 0)),
            pl.BlockSpec((heads, c, c), const3),
            pl.BlockSpec((heads, c, dv), const3),
            pl.BlockSpec((heads, c, dk), const3),
            pl.BlockSpec((heads, 1, dv), const3),
            pl.BlockSpec((1, w), lambda b, i: (0, 0)),
        ],
        out_specs=(
            pl.BlockSpec((1, c, w), lambda b, i: (b, i, 0)),
            pl.BlockSpec((1, heads, dk, dv), lambda b, i: (b, 0, 0, 0)),
        ),
        scratch_shapes=[pltpu.VMEM((heads, dk, dv), F32)],
        compiler_params=_params("arbitrary", "arbitrary"),
        name="retention",
    )(qkvg, qkvg, qkvg, qkvg, s0, dm, ind, sd, cd, norm_w.reshape(1, w))


def _gla_kernel(q_ref, k_ref, la_ref, v_ref, g_ref, s0_ref, w_ref, o_ref, so_ref, st_ref, *, heads, dk, dv, c):
    ci = pl.program_id(1)

    @pl.when(ci == 0)
    def _():
        st_ref[...] = s0_ref[0]

    row = lax.broadcasted_iota(I32, (c, dk), 0)
    ri = lax.broadcasted_iota(I32, (c, c), 0)
    cj = lax.broadcasted_iota(I32, (c, c), 1)
    for hd in range(heads):
        q = q_ref[0, :, hd * dk:(hd + 1) * dk]
        k = k_ref[0, :, hd * dk:(hd + 1) * dk]
        v = v_ref[0, :, hd * dv:(hd + 1) * dv]
        b = la_ref[0, :, hd * dk:(hd + 1) * dk]
        sh = 1
        while sh < c:
            b = b + jnp.where(row >= sh, pltpu.roll(b, sh, 0), 0.0)
            sh *= 2
        scores = jnp.where(ri == cj, _dot_nt(q.astype(BF16), k.astype(BF16)), 0.0)
        first = b
        half = 1
        while half < c:
            upper = (row & (2 * half - 1)) >= half
            mid = jnp.where(upper, first, pltpu.roll(first, c - half, 0))
            ql = jnp.where(upper, q * jnp.exp(jnp.minimum(b - mid, 0.0)), 0.0).astype(BF16)
            kl = jnp.where(upper, 0.0, k * jnp.exp(jnp.minimum(mid - b, 0.0))).astype(BF16)
            same = (ri & -(2 * half)) == (cj & -(2 * half))
            scores = scores + jnp.where(same, _dot_nt(ql, kl), 0.0)
            first = jnp.where(upper, pltpu.roll(first, half, 0), first)
            half *= 2
        st = st_ref[hd]
        o = _dot(scores.astype(BF16), v) + _dot_nt((q * jnp.exp(b)).astype(BF16), st.astype(BF16))
        b_last = b[c - 1:c, :]
        kd = (k * jnp.exp(b_last - b)).astype(BF16)
        st_ref[hd] = st * jnp.exp(b_last) + _dot_tn(v, kd)
        on = o * lax.rsqrt(jnp.mean(o * o, axis=-1, keepdims=True) + EPS)
        gate = g_ref[0, :, hd * dv:(hd + 1) * dv].astype(F32)
        o_ref[0, :, hd * dv:(hd + 1) * dv] = (gate * (on * w_ref[:, hd * dv:(hd + 1) * dv])).astype(o_ref.dtype)

    @pl.when(ci == pl.num_programs(1) - 1)
    def _():
        so_ref[0] = st_ref[...]


def _gla(qk, vg, la, s0t, norm_w, *, heads, dk, dv):
    s, t, _ = qk.shape
    c = _tile(t, CHUNK_GLA, 2 * SUBLANES)
    assert c & (c - 1) == 0, "chunk length must be a power of two"
    wk, wv = heads * dk, heads * dv
    return pl.pallas_call(
        functools.partial(_gla_kernel, heads=heads, dk=dk, dv=dv, c=c),
        out_shape=(jax.ShapeDtypeStruct((s, t, wv), BF16), jax.ShapeDtypeStruct(s0t.shape, F32)),
        grid=(s, t // c),
        in_specs=[
            pl.BlockSpec((1, c, wk), lambda b, i: (b, i, 0)),
            pl.BlockSpec((1, c, wk), lambda b, i: (b, i, 1)),
            pl.BlockSpec((1, c, wk), lambda b, i: (b, i, 0)),
            pl.BlockSpec((1, c, wv), lambda b, i: (b, i, 0)),
            pl.BlockSpec((1, c, wv), lambda b, i: (b, i, 1)),
            pl.BlockSpec((1, heads, dv, dk), lambda b, i: (b, 0, 0, 0)),
            pl.BlockSpec((1, wv), lambda b, i: (0, 0)),
        ],
        out_specs=(
            pl.BlockSpec((1, c, wv), lambda b, i: (b, i, 0)),
            pl.BlockSpec((1, heads, dv, dk), lambda b, i: (b, 0, 0, 0)),
        ),
        scratch_shapes=[pltpu.VMEM((heads, dv, dk), F32)],
        compiler_params=_params("arbitrary", "arbitrary"),
        name="gla",
    )(qk, qk, la, vg, vg, s0t, norm_w.reshape(1, wv))


def _merge_kernel(br_ref, bg_ref, h_ref, wr_ref, wg_ref, wm0_ref, wm1_ref, bm0_ref, bm1_ref, o_ref):
    h = h_ref[...]
    g0 = _sigmoid(_dot(h, wm0_ref[...]) + bm0_ref[...])
    g1 = _sigmoid(_dot(h, wm1_ref[...]) + bm1_ref[...])
    y = g0 * _dot(br_ref[...], wr_ref[...]) + g1 * _dot(bg_ref[...], wg_ref[...])
    o_ref[...] = y.astype(o_ref.dtype)


def _merge(o_ret, o_gla, h2d, w_branch, w_merge, layer, b_merge, *, tm):
    n, d = h2d.shape
    wdt = o_ret.shape[1]
    tn = _tile(d, 512, LANES)
    nj = d // tn
    b2 = b_merge.reshape(1, 2 * d)
    return pl.pallas_call(
        _merge_kernel,
        out_shape=jax.ShapeDtypeStruct((n, d), BF16),
        grid=(n // tm, nj),
        in_specs=[
            pl.BlockSpec((tm, wdt), lambda i, j: (i, 0)),
            pl.BlockSpec((tm, wdt), lambda i, j: (i, 0)),
            pl.BlockSpec((tm, d), lambda i, j: (i, 0)),
            pl.BlockSpec((None, wdt, tn), lambda i, j: (2 * layer, 0, j)),
            pl.BlockSpec((None, wdt, tn), lambda i, j: (2 * layer + 1, 0, j)),
            pl.BlockSpec((None, d, tn), lambda i, j: (layer, 0, j)),
            pl.BlockSpec((None, d, tn), lambda i, j: (layer, 0, j + nj)),
            pl.BlockSpec((1, tn), lambda i, j: (0, j)),
            pl.BlockSpec((1, tn), lambda i, j: (0, j + nj)),
        ],
        out_specs=pl.BlockSpec((tm, tn), lambda i, j: (i, j)),
        compiler_params=_params("arbitrary", "arbitrary"),
        name="branch_merge",
    )(o_ret, o_gla, h2d, w_branch, w_branch, w_merge, w_merge, b2, b2)


def _pack_bf16_pairs(h):
    half = h.shape[-1] // 2
    a = lax.bitcast_convert_type(h[:, :half].astype(BF16).astype(F32), U32)
    b = lax.bitcast_convert_type(h[:, half:].astype(BF16).astype(F32), U32)
    return a | (b >> 16)


def _unpack_bf16_pairs(w):
    a = lax.bitcast_convert_type(w & jnp.uint32(0xFFFF0000), F32)
    b = lax.bitcast_convert_type(w << 16, F32)
    return jnp.concatenate([a, b], axis=-1).astype(BF16)


def _route(logits, n_groups, per_group):
    lane = lax.broadcasted_iota(I32, logits.shape, 1).astype(F32)
    neg = jnp.float32(-jnp.inf)

    def first_max(mask):
        m = jnp.max(jnp.where(mask, logits, neg), axis=-1, keepdims=True)
        idx = jnp.min(jnp.where(mask & (logits == m), lane, float(LANES)), axis=-1, keepdims=True)
        return m, idx

    gmask = lane < n_groups
    gmax, gidx = first_max(gmask)
    p_group = 1.0 / jnp.sum(jnp.where(gmask, jnp.exp(logits - gmax), 0.0), axis=-1, keepdims=True)
    lo = n_groups + gidx * per_group
    emask = (lane >= lo) & (lane < lo + per_group)
    m1, i1 = first_max(emask)
    m2, i2 = first_max(emask & (lane != i1))
    w1 = 1.0 / (1.0 + jnp.exp(m2 - m1))
    w2 = 1.0 - w1
    return jnp.where(lane == 0.0, i1 - n_groups, jnp.where(lane == 1.0, i2 - n_groups, jnp.where(
        lane == 2.0, p_group * w1, jnp.where(lane == 3.0, p_group * w2, 0.0))))


def _with_ranks(rt, carry_ref):
    tr = rt.shape[0]
    lane = lax.broadcasted_iota(I32, rt.shape, 1).astype(F32)
    a1 = lane == rt[:, 0:1]
    a2 = lane == rt[:, 1:2]
    hit = jnp.where(a1 | a2, 1.0, 0.0)
    ri = lax.broadcasted_iota(I32, (tr, tr), 0)
    cj = lax.broadcasted_iota(I32, (tr, tr), 1)
    before = _dot(jnp.where(ri > cj, 1.0, 0.0).astype(BF16), hit.astype(BF16)) + carry_ref[...]
    k1 = jnp.sum(jnp.where(a1, before, 0.0), axis=-1, keepdims=True)
    k2 = jnp.sum(jnp.where(a2, before, 0.0), axis=-1, keepdims=True)
    carry_ref[...] += jnp.sum(hit, axis=0, keepdims=True)
    return jnp.where(lane == 4.0, k1, jnp.where(lane == 5.0, k2, rt))


def _out_proj_kernel(m_ref, x_ref, g1_ref, sc_ref, sh_ref, nw_ref, wo_ref, wr_ref, br_ref,
                     x1_ref, hp_ref, rt_ref, cnt_ref, carry_ref, *, n_groups, per_group):
    bs, tt, d = x_ref.shape

    @pl.when((pl.program_id(0) == 0) & (pl.program_id(1) == 0))
    def _():
        carry_ref[...] = jnp.zeros_like(carry_ref)

    y = _dot(m_ref[...].reshape(bs * tt, d), wo_ref[...]).reshape(bs, tt, d)
    x1 = x_ref[...] + g1_ref[...] * y
    x1_ref[...] = x1
    hn = x1 * lax.rsqrt(jnp.mean(x1 * x1, axis=-1, keepdims=True) + EPS) * nw_ref[...]
    h = (hn * (1.0 + sc_ref[...]) + sh_ref[...]).reshape(bs * tt, d)
    hp_ref[...] = _pack_bf16_pairs(h).reshape(bs, tt, d // 2)
    h_hi, h_lo = _split_bf16(h)
    p = _dot(h_hi, wr_ref[...])
    logits = p[:, :LANES] + p[:, LANES:] + _dot(h_lo, wr_ref[:, :LANES]) + br_ref[...]
    rt = _with_ranks(_route(logits, n_groups, per_group), carry_ref)
    rt_ref[...] = rt.reshape(bs, tt, LANES)
    cnt_ref[...] = carry_ref[...]


def _out_proj(merged, x, g1, sc2, sh2, norm_w, w_o, layer, wr, b_r, *, n_groups, per_group):
    s, t, d = x.shape
    bs, tt = _seq_tiles(s, t, ROWS_OUT_PROJ)
    tok = lambda i, j: (i, j, 0)
    seq = lambda i, j: (i, 0, 0)
    const = lambda i, j: (0, 0)
    return pl.pallas_call(
        functools.partial(_out_proj_kernel, n_groups=n_groups, per_group=per_group),
        out_shape=(
            jax.ShapeDtypeStruct((s, t, d), F32),
            jax.ShapeDtypeStruct((s, t, d // 2), U32),
            jax.ShapeDtypeStruct((s, t, LANES), F32),
            jax.ShapeDtypeStruct((1, LANES), F32),
        ),
        grid=(s // bs, t // tt),
        in_specs=[
            pl.BlockSpec((bs, tt, d), tok),
            pl.BlockSpec((bs, tt, d), tok),
            pl.BlockSpec((bs, 1, d), seq),
            pl.BlockSpec((bs, 1, d), seq),
            pl.BlockSpec((bs, 1, d), seq),
            pl.BlockSpec((1, d), const),
            pl.BlockSpec((None, d, d), lambda i, j: (layer, 0, 0), pipeline_mode=pl.Buffered(1)),
            pl.BlockSpec((d, 2 * LANES), const, pipeline_mode=pl.Buffered(1)),
            pl.BlockSpec((1, LANES), const),
        ],
        out_specs=(
            pl.BlockSpec((bs, tt, d), tok),
            pl.BlockSpec((bs, tt, d // 2), tok),
            pl.BlockSpec((bs, tt, LANES), tok),
            pl.BlockSpec((1, LANES), const),
        ),
        scratch_shapes=[pltpu.VMEM((1, LANES), F32)],
        compiler_params=_params("arbitrary", "arbitrary"),
        name="out_proj_router",
    )(merged, x, g1, sc2, sh2, norm_w.reshape(1, d), w_o, wr, b_r)


def _invert_kernel(pos_ref, src_ref, *, n_tokens, n_rows):
    def clear(j, carry):
        src_ref[j] = 0
        return carry

    lax.fori_loop(0, n_rows, clear, 0, unroll=INVERT_UNROLL)

    def put(t, carry):
        for kk in range(TOP_K):
            src_ref[pos_ref[TOP_K * t + kk]] = t
        return carry

    lax.fori_loop(0, n_tokens, put, 0, unroll=INVERT_UNROLL)


def _invert(pos, n_rows):
    n_tokens = pos.shape[0] // TOP_K
    return pl.pallas_call(
        functools.partial(_invert_kernel, n_tokens=n_tokens, n_rows=n_rows),
        out_shape=jax.ShapeDtypeStruct((n_rows,), I32),
        grid_spec=pltpu.PrefetchScalarGridSpec(
            num_scalar_prefetch=1,
            grid=(1,),
            in_specs=[],
            out_specs=pl.BlockSpec(memory_space=pltpu.SMEM),
        ),
        compiler_params=_params("arbitrary"),
        name="moe_invert",
    )(pos)


def _row_copy(src_ref, src_row, dst_ref, dst_row, sem):
    return pltpu.make_async_copy(src_ref.at[pl.ds(src_row, 1)], dst_ref.at[pl.ds(dst_row, 1)], sem)


def _ffn_kernel(te_ref, valid_ref, src_ref, hp_ref, wg_ref, wu_ref, wd_ref, ys_ref, xbuf, sems, *, tf):
    del te_ref
    i = pl.program_id(0)
    last = pl.num_programs(0) - 1
    slot = i % 2
    nxt = jnp.where(i < last, i + 1, 0)

    def start(tile, slt, r):
        _row_copy(hp_ref, src_ref[tile * tf + r], xbuf.at[slt], r, sems.at[slt]).start()

    def drain(slt):
        def wait(r, carry):
            _row_copy(hp_ref, 0, xbuf.at[slt], 0, sems.at[slt]).wait()
            return carry

        lax.fori_loop(0, tf, wait, 0, unroll=GATHER_UNROLL)

    @pl.when(i == 0)
    def _():
        def issue(r, carry):
            start(0, 0, r)
            return carry

        lax.fori_loop(0, tf, issue, 0, unroll=GATHER_UNROLL)

    for r in range(tf):
        start(nxt, 1 - slot, r)

    drain(slot)

    @pl.when(valid_ref[i] != 0)
    def _():
        x = _unpack_bf16_pairs(xbuf[slot])
        act = (_silu(_dot(x, wg_ref[...])) * _dot(x, wu_ref[...])).astype(BF16)
        ys_ref[...] = _dot(act, wd_ref[...])

    @pl.when(valid_ref[i] == 0)
    def _():
        ys_ref[...] = jnp.zeros_like(ys_ref)

    @pl.when(i == last)
    def _():
        drain(1 - slot)


def _ffn(tile_expert, tile_valid, src, hp2d, wg, wu, wd, *, tf):
    n_rows = src.shape[0]
    half = hp2d.shape[1]
    _, d, f = wg.shape
    return pl.pallas_call(
        functools.partial(_ffn_kernel, tf=tf),
        out_shape=jax.ShapeDtypeStruct((n_rows, d), F32),
        grid_spec=pltpu.PrefetchScalarGridSpec(
            num_scalar_prefetch=3,
            grid=(n_rows // tf,),
            in_specs=[
                pl.BlockSpec(memory_space=pl.ANY),
                pl.BlockSpec((None, d, f), lambda i, te, tv, sr: (te[i], 0, 0)),
                pl.BlockSpec((None, d, f), lambda i, te, tv, sr: (te[i], 0, 0)),
                pl.BlockSpec((None, f, d), lambda i, te, tv, sr: (te[i], 0, 0)),
            ],
            out_specs=pl.BlockSpec((tf, d), lambda i, te, tv, sr: (i, 0)),
            scratch_shapes=[pltpu.VMEM((2, tf, half), hp2d.dtype), pltpu.SemaphoreType.DMA((2,))],
        ),
        compiler_params=_params("arbitrary"),
        name="moe_ffn",
    )(tile_expert, tile_valid, src, hp2d, wg, wu, wd)


def _combine_kernel(pos_ref, x_ref, rt_ref, g2_ref, nw_ref, sc_ref, sh_ref, ys_ref, *rest, last):
    if last:
        o_ref, ybuf, sems = rest
    else:
        o_ref, h_ref, ybuf, sems = rest
    bs, tt, d = x_ref.shape
    rows = bs * tt
    step = pl.program_id(0) * pl.num_programs(1) + pl.program_id(1)
    n_steps = pl.num_programs(0) * pl.num_programs(1)

    def start(stp, slt, r, kk):
        _row_copy(ys_ref, pos_ref[TOP_K * (stp * rows + r) + kk], ybuf.at[slt, kk], r, sems.at[slt]).start()

    @pl.when(step == 0)
    def _():
        def issue(r, carry):
            for kk in range(TOP_K):
                start(0, 0, r, kk)
            return carry

        lax.fori_loop(0, rows, issue, 0, unroll=GATHER_UNROLL)

    slot = step % 2

    @pl.when(step + 1 < n_steps)
    def _():
        for r in range(rows):
            for kk in range(TOP_K):
                start(step + 1, 1 - slot, r, kk)

    def wait(r, carry):
        for kk in range(TOP_K):
            _row_copy(ys_ref, 0, ybuf.at[slot, kk], 0, sems.at[slot]).wait()
        return carry

    lax.fori_loop(0, rows, wait, 0, unroll=GATHER_UNROLL)

    rt = rt_ref[...].reshape(rows, LANES)
    y = rt[:, 2:3] * ybuf[slot, 0] + rt[:, 3:4] * ybuf[slot, 1]
    xn = x_ref[...] + g2_ref[...] * y.reshape(bs, tt, d)
    normed = xn * lax.rsqrt(jnp.mean(xn * xn, axis=-1, keepdims=True) + EPS) * nw_ref[...]
    if last:
        o_ref[...] = normed
    else:
        o_ref[...] = xn
        h_ref[...] = (normed * (1.0 + sc_ref[...]) + sh_ref[...]).astype(h_ref.dtype)


def _combine(pos, x1, route, g2, norm_w, sc, sh, ys, *, last):
    s, t, d = x1.shape
    bs, tt = _seq_tiles(s, t, ROWS_COMBINE)
    tok = lambda i, j, p: (i, j, 0)
    seq = lambda i, j, p: (i, 0, 0)
    x_shape = jax.ShapeDtypeStruct((s, t, d), F32)
    x_spec = pl.BlockSpec((bs, tt, d), tok)
    return pl.pallas_call(
        functools.partial(_combine_kernel, last=last),
        out_shape=x_shape if last else (x_shape, jax.ShapeDtypeStruct((s, t, d), BF16)),
        grid_spec=pltpu.PrefetchScalarGridSpec(
            num_scalar_prefetch=1,
            grid=(s // bs, t // tt),
            in_specs=[
                x_spec,
                pl.BlockSpec((bs, tt, LANES), tok),
                pl.BlockSpec((bs, 1, d), seq),
                pl.BlockSpec((1, d), lambda i, j, p: (0, 0)),
                pl.BlockSpec((bs, 1, d), seq),
                pl.BlockSpec((bs, 1, d), seq),
                pl.BlockSpec(memory_space=pl.ANY),
            ],
            out_specs=x_spec if last else (x_spec, x_spec),
            scratch_shapes=[pltpu.VMEM((2, TOP_K, bs * tt, d), F32), pltpu.SemaphoreType.DMA((2,))],
        ),
        compiler_params=_params("arbitrary", "arbitrary"),
        name="moe_combine",
    )(pos, x1, route, g2, norm_w.reshape(1, d), sc, sh, ys)


def _moe(x1, hp, route, counts, g2, wg, wu, wd, layer, n_exp, norm_w, sc, sh, *, last):
    s, t, d = x1.shape
    n = s * t
    route2d = route.reshape(n, LANES)
    tf = 2 * SUBLANES
    while tf < ROWS_FFN and tf < 2 * TOP_K * n // n_exp:
        tf *= 2
    n_tiles = (TOP_K * n) // tf + n_exp
    cnt = counts[0, :n_exp].astype(I32)
    padded = ((cnt + tf - 1) // tf) * tf
    ends = jnp.cumsum(padded)
    starts = ends - padded
    expert = route2d[:, :TOP_K].astype(I32)
    rank = route2d[:, 4:4 + TOP_K].astype(I32)
    start_of = jnp.sum(jnp.where(expert[..., None] == jnp.arange(n_exp, dtype=I32), starts, 0), axis=-1)
    pos = (start_of + rank).reshape(TOP_K * n)
    tile_start = jnp.arange(n_tiles, dtype=I32) * tf
    tile_valid = (tile_start < ends[-1]).astype(I32)
    tile_expert = jnp.minimum(jnp.sum((ends[None, :] <= tile_start[:, None]).astype(I32), axis=1), n_exp - 1)
    last_used = jnp.max(jnp.where(tile_valid != 0, tile_expert, 0))
    tile_expert = jnp.where(tile_valid != 0, tile_expert, last_used)
    src = _invert(pos, n_tiles * tf)
    ys = _ffn(tile_expert + layer * n_exp, tile_valid, src, hp.reshape(n, d // 2), wg, wu, wd, tf=tf)
    return _combine(pos, x1, route, g2, norm_w, sc, sh, ys, last=last)


def _rotary_tables(pos0, t, dk, reps):
    half = dk // 2
    inv = ROPE_BASE ** (-jnp.arange(half, dtype=F32) / half)
    ang = (pos0 + jnp.arange(t)).astype(F32)[:, None] * inv[None, :]
    cos = jnp.cos(ang)
    sin = jnp.sin(ang)
    cos2 = jnp.concatenate([cos, cos], axis=-1)
    sin2 = jnp.concatenate([-sin, sin], axis=-1)
    return jnp.tile(cos2, (reps, 1)), jnp.tile(sin2, (reps, 1))


def _trunk(x, mods, pos0, s_ret_in, s_gla_in_t, big, wts, dims):
    s, t, d = x.shape
    n = s * t
    depth = len(wts)
    rh, rdk, rdv, gh, gdk, gdv = dims["ret_heads"], dims["ret_dk"], dims["ret_dv"], dims["gla_heads"], dims["gla_dk"], dims["gla_dv"]
    bs, tt = _seq_tiles(s, t, ROWS_PROJ)
    tp = bs * tt
    tm = _tile(n, ROWS_MATMUL, 2 * SUBLANES)
    cos, sin = _rotary_tables(pos0, t, rdk, bs)
    ret_states, gla_states = [], []
    h = _norm(x, wts[0]["norm_mix"], mods[1][0], mods[0][0])
    for l in range(depth):
        w = wts[l]
        sh1, sc1, g1, sh2, sc2, g2 = (m[l] for m in mods)
        h2d = h.reshape(n, d)
        qkvg = _proj_ret(h2d, big["w_ret"], l, cos, sin, heads=rh, dk=rdk, tm=tp)
        qk, vg, la = _proj_gla(h2d, big["w_gla"], big["w_code"], l, w["wup_hi"], w["wup_lo"], w["b_gla"],
                               hdk=gh * gdk, hdv=gh * gdv, dk=gdk, tm=tp)
        o_ret, s_r = _retention(qkvg.reshape(s, t, -1), s_ret_in[l], w["ret_norm"], heads=rh, dk=rdk, dv=rdv)
        o_gla, s_g = _gla(qk.reshape(s, t, -1), vg.reshape(s, t, -1), la.reshape(s, t, -1), s_gla_in_t[l],
                          w["gla_norm"], heads=gh, dk=gdk, dv=gdv)
        merged = _merge(o_ret.reshape(n, -1), o_gla.reshape(n, -1), h2d, big["w_branch"], big["w_merge"], l,
                        w["b_merge"], tm=tm)
        x1, hp, route, counts = _out_proj(merged.reshape(s, t, d), x, g1, sc2, sh2, w["norm_ffn"], big["w_o"], l,
                                  w["w_router"], w["b_r"], n_groups=dims["n_groups"], per_group=dims["per_group"])
        n_exp = dims["n_groups"] * dims["per_group"]
        experts = (big["w_exp_gate"], big["w_exp_up"], big["w_exp_down"], l, n_exp)
        if l == depth - 1:
            x = _moe(x1, hp, route, counts, g2, *experts, w["final_norm"], sc1, sh1, last=True)
        else:
            x, h = _moe(x1, hp, route, counts, g2, *experts, wts[l + 1]["norm_mix"], mods[1][l + 1], mods[0][l + 1], last=False)
        ret_states.append(s_r)
        gla_states.append(jnp.swapaxes(s_g, -1, -2))
    return x, jnp.stack(ret_states), jnp.stack(gla_states)


def kernel(x_prompt, x_sample, state_ret, state_gla, c_prompt, c_sample, w_ada, b_ada, norm_mix_w, norm_ffn_w, w_in, w_gla_up, b_gla, ret_norm_w, gla_norm_w, w_branch, w_merge, b_merge, w_o, w_router_group, b_router_group, w_router_expert, b_router_expert, w_exp_gate, w_exp_up, w_exp_down, final_norm_w):
    depth, d, _ = w_in.shape
    _, _, rh, rdk, rdv = state_ret.shape
    _, _, gh, gdk, gdv = state_gla.shape
    low_rank = w_gla_up.shape[1]
    n_groups = w_router_group.shape[-1]
    n_exp = w_router_expert.shape[-1]
    assert low_rank <= LANES and n_groups + n_exp <= LANES and n_exp % n_groups == 0
    dims = dict(ret_heads=rh, ret_dk=rdk, ret_dv=rdv, gla_heads=gh, gla_dk=gdk, gla_dv=gdv,
                n_groups=n_groups, per_group=n_exp // n_groups)

    n_ret = 2 * rh * rdk + 2 * rh * rdv
    n_gla = 2 * gh * gdk + 2 * gh * gdv
    assert w_in.shape[-1] == n_ret + n_gla + low_rank
    f = w_exp_gate.shape[-1]
    big = dict(
        w_ret=_to_bf16(w_in, 0, n_ret),
        w_gla=_to_bf16(w_in, n_ret, n_gla),
        w_code=_to_bf16(w_in, n_ret + n_gla, LANES, valid_cols=low_rank),
        w_branch=_to_bf16(w_branch.reshape(depth, -1, d)).reshape(depth * w_branch.shape[1], -1, d),
        w_merge=_to_bf16(w_merge),
        w_o=_to_bf16(w_o),
        w_exp_gate=_to_bf16(w_exp_gate.reshape(depth, n_exp * d, f)).reshape(depth * n_exp, d, f),
        w_exp_up=_to_bf16(w_exp_up.reshape(depth, n_exp * d, f)).reshape(depth * n_exp, d, f),
        w_exp_down=_to_bf16(w_exp_down.reshape(depth, n_exp * f, d)).reshape(depth * n_exp, f, d),
    )
    w_router = jnp.concatenate([w_router_group, w_router_expert], axis=-1)
    w_router = jnp.pad(w_router, ((0, 0), (0, 0), (0, LANES - n_groups - n_exp)))
    b_router = jnp.pad(jnp.concatenate([b_router_group, b_router_expert], axis=-1), ((0, 0), (0, LANES - n_groups - n_exp)))
    wup = jnp.pad(w_gla_up, ((0, 0), (0, LANES - low_rank), (0, 0)))
    wts = []
    for l in range(depth):
        wr_hi, wr_lo = _split_bf16(w_router[l])
        wup_hi, wup_lo = _split_bf16(wup[l])
        wts.append(dict(
            norm_mix=norm_mix_w[l], norm_ffn=norm_ffn_w[l],
            wup_hi=wup_hi, wup_lo=wup_lo, b_gla=b_gla[l].reshape(1, -1),
            ret_norm=ret_norm_w[l], gla_norm=gla_norm_w[l], b_merge=b_merge[l],
            w_router=jnp.concatenate([wr_hi, wr_lo], axis=-1), b_r=b_router[l].reshape(1, LANES),
            final_norm=final_norm_w,
        ))

    nb, ns = c_prompt.shape[0], c_sample.shape[0]
    mod = _ada(jnp.concatenate([c_prompt, c_sample], axis=0), w_ada, b_ada)
    mod = mod.reshape(depth, nb + ns, 6, d)
    mods_p = [mod[:, :nb, i][:, :, None, :] for i in range(6)]
    mods_s = [mod[:, nb:, i][:, :, None, :] for i in range(6)]

    zero_ret = jnp.zeros((depth, nb, rh, rdk, rdv), F32)
    zero_gla_t = jnp.zeros((depth, nb, gh, gdv, gdk), F32)
    y_p, ret_p, gla_p = _trunk(x_prompt, mods_p, 0, zero_ret, zero_gla_t, big, wts, dims)
    y_s, ret_s, gla_s = _trunk(x_sample, mods_s, PAST_LEN, state_ret.astype(F32),
                               jnp.swapaxes(state_gla.astype(F32), -1, -2), big, wts, dims)
    return (y_p, y_s, ret_p, gla_p, ret_s, gla_s)
```

```python
import functools

import jax
import jax.numpy as jnp
from jax import lax
from jax.experimental import pallas as pl
from jax.experimental.pallas import tpu as pltpu

F32 = jnp.float32
BF16 = jnp.bfloat16
U32 = jnp.uint32
I32 = jnp.int32

PAST_LEN = 4096
GLA_TAU = 16.0
ROPE_BASE = 10000.0
EPS = 1e-6
RET_DECAY_LOG2_BASE = -5.0
TOP_K = 2
LOG2_E = 1.4426950408889634

LANES = 128
SUBLANES = 8
V7X_VMEM_BYTES = 64 * 1024 * 1024
VMEM_LIMIT = V7X_VMEM_BYTES - 8 * 1024 * 1024

ROWS_MATMUL = 1024
ROWS_PROJ = 512
ROWS_OUT_PROJ = 512
ROWS_NORM = 512
ROWS_COMBINE = 256
GATHER_UNROLL = 8
INVERT_UNROLL = 16
CAST_BLOCK_COLS = 1024
CAST_BLOCK_ELEMS = 2 * 1024 * 1024
ROWS_FFN = 512
CHUNK_RET = 256
CHUNK_GLA = 256


def _params(*sem):
    return pltpu.CompilerParams(dimension_semantics=sem, vmem_limit_bytes=VMEM_LIMIT)


def _tile(n, pref, align):
    if n <= pref:
        return n
    t = (pref // align) * align
    while t >= align:
        if n % t == 0:
            return t
        t -= align
    return n


def _seq_tiles(s, t, rows):
    if t >= rows:
        return 1, _tile(t, rows, 2 * SUBLANES)
    return _tile(s, max(rows // t, 1), 1), t


def _dot(a, b):
    return jnp.dot(a, b, preferred_element_type=F32)


def _dot_nt(a, b):
    return lax.dot_general(a, b, (((1,), (1,)), ((), ())), preferred_element_type=F32)


def _dot_tn(a, b):
    return lax.dot_general(a, b, (((0,), (0,)), ((), ())), preferred_element_type=F32)


def _sigmoid(x):
    return 0.5 * jnp.tanh(0.5 * x) + 0.5


def _silu(x):
    return x * _sigmoid(x)


def _log_sigmoid(z):
    return jnp.minimum(z, 0.0) - jnp.log(1.0 + jnp.exp(-jnp.abs(z)))


def _split_bf16(x):
    hi = x.astype(BF16)
    lo = (x - hi.astype(F32)).astype(BF16)
    return hi, lo


def _dot_split(a, w_hi, w_lo):
    a_hi, a_lo = _split_bf16(a)
    return _dot(a_hi, w_hi) + _dot(a_lo, w_hi) + _dot(a_hi, w_lo)


def _ada_kernel(c_ref, w_ref, b_ref, o_ref):
    a = _silu(c_ref[...]).astype(BF16)
    o_ref[...] = _dot(a, w_ref[...].astype(BF16)) + b_ref[...]


def _ada(c_all, w_ada, b_ada):
    depth, d, n6 = w_ada.shape
    r = c_all.shape[0]
    tn = _tile(n6, 1024, LANES)
    return pl.pallas_call(
        _ada_kernel,
        out_shape=jax.ShapeDtypeStruct((depth, r, n6), F32),
        grid=(depth, n6 // tn),
        in_specs=[
            pl.BlockSpec((r, d), lambda l, j: (0, 0)),
            pl.BlockSpec((None, d, tn), lambda l, j: (l, 0, j)),
            pl.BlockSpec((None, 1, tn), lambda l, j: (l, 0, j)),
        ],
        out_specs=pl.BlockSpec((None, r, tn), lambda l, j: (l, 0, j)),
        compiler_params=_params("arbitrary", "arbitrary"),
        name="ada_mod",
    )(c_all, w_ada, b_ada.reshape(depth, 1, n6))


def _norm_kernel(x_ref, w_ref, sc_ref, sh_ref, o_ref):
    x = x_ref[...]
    y = x * lax.rsqrt(jnp.mean(x * x, axis=-1, keepdims=True) + EPS) * w_ref[...]
    o_ref[...] = (y * (1.0 + sc_ref[...]) + sh_ref[...]).astype(o_ref.dtype)


def _norm(x, w, sc, sh):
    s, t, d = x.shape
    bs, tt = _seq_tiles(s, t, ROWS_NORM)
    return pl.pallas_call(
        _norm_kernel,
        out_shape=jax.ShapeDtypeStruct((s, t, d), BF16),
        grid=(s // bs, t // tt),
        in_specs=[
            pl.BlockSpec((bs, tt, d), lambda i, j: (i, j, 0)),
            pl.BlockSpec((1, d), lambda i, j: (0, 0)),
            pl.BlockSpec((bs, 1, d), lambda i, j: (i, 0, 0)),
            pl.BlockSpec((bs, 1, d), lambda i, j: (i, 0, 0)),
        ],
        out_specs=pl.BlockSpec((bs, tt, d), lambda i, j: (i, j, 0)),
        compiler_params=_params("arbitrary", "arbitrary"),
        name="mod_norm",
    )(x, w.reshape(1, d), sc, sh)


def _resident(shape, layer=None):
    if layer is None:
        return pl.BlockSpec(shape, lambda i: (0,) * len(shape), pipeline_mode=pl.Buffered(1))
    return pl.BlockSpec((None,) + tuple(shape), lambda i: (layer,) + (0,) * len(shape), pipeline_mode=pl.Buffered(1))


def _cast_kernel(x_ref, o_ref, *, valid_cols):
    x = x_ref[...]
    if valid_cols is not None:
        x = jnp.where(lax.broadcasted_iota(I32, x.shape, 1) < valid_cols, x, 0.0)
    o_ref[...] = x.astype(o_ref.dtype)


def _to_bf16(w, col0=0, ncols=None, valid_cols=None):
    depth, rows, cols = w.shape
    ncols = cols - col0 if ncols is None else ncols
    tc = LANES
    while tc * 2 <= CAST_BLOCK_COLS and col0 % (tc * 2) == 0 and ncols % (tc * 2) == 0:
        tc *= 2
    assert col0 % tc == 0 and ncols % tc == 0 and (valid_cols is None or ncols == tc)
    tr = _tile(rows, max(CAST_BLOCK_ELEMS // tc, 2 * SUBLANES), 2 * SUBLANES)
    c0 = col0 // tc
    return pl.pallas_call(
        functools.partial(_cast_kernel, valid_cols=valid_cols),
        out_shape=jax.ShapeDtypeStruct((depth, rows, ncols), BF16),
        grid=(depth, rows // tr, ncols // tc),
        in_specs=[pl.BlockSpec((None, tr, tc), lambda l, i, j: (l, i, c0 + j))],
        out_specs=pl.BlockSpec((None, tr, tc), lambda l, i, j: (l, i, j)),
        compiler_params=_params("arbitrary", "arbitrary", "arbitrary"),
        name="cast_bf16",
    )(w)


def _proj_ret_kernel(h_ref, w_ref, cos_ref, sin_ref, o_ref, *, heads, dk, kscale):
    h = h_ref[...]
    tn = heads * dk
    cos = cos_ref[...]
    sin = sin_ref[...]
    for sec, mult in ((0, 1.0), (1, kscale)):
        acc = _dot(h, w_ref[:, sec * tn:(sec + 1) * tn])
        for hd in range(heads):
            a = acc[:, hd * dk:(hd + 1) * dk]
            r = a * cos + pltpu.roll(a, dk // 2, 1) * sin
            o_ref[:, sec * tn + hd * dk:sec * tn + (hd + 1) * dk] = (r * mult).astype(o_ref.dtype)
    o_ref[:, 2 * tn:3 * tn] = _dot(h, w_ref[:, 2 * tn:3 * tn]).astype(o_ref.dtype)
    o_ref[:, 3 * tn:] = _silu(_dot(h, w_ref[:, 3 * tn:])).astype(o_ref.dtype)


def _proj_ret(h2d, w, layer, cos, sin, *, heads, dk, tm):
    n, d = h2d.shape
    tn = heads * dk
    assert w.shape[1:] == (d, 4 * tn) and cos.shape[0] % tm == 0
    nt = cos.shape[0] // tm
    return pl.pallas_call(
        functools.partial(_proj_ret_kernel, heads=heads, dk=dk, kscale=dk ** -0.5),
        out_shape=jax.ShapeDtypeStruct((n, 4 * tn), BF16),
        grid=(n // tm,),
        in_specs=[
            pl.BlockSpec((tm, d), lambda i: (i, 0)),
            _resident((d, 4 * tn), layer),
            pl.BlockSpec((tm, dk), lambda i: (i % nt, 0)),
            pl.BlockSpec((tm, dk), lambda i: (i % nt, 0)),
        ],
        out_specs=pl.BlockSpec((tm, 4 * tn), lambda i: (i, 0)),
        compiler_params=_params("arbitrary"),
        name="proj_ret",
    )(h2d, w, cos, sin)


def _proj_gla_kernel(h_ref, w_ref, wc_ref, wup_hi_ref, wup_lo_ref, bup_ref, qk_ref, vg_ref, la_ref, *, hdk, hdv, qscale):
    h = h_ref[...]
    qk = _dot(h, w_ref[:, :2 * hdk])
    qk_ref[:, :hdk] = qk[:, :hdk] * qscale
    qk_ref[:, hdk:] = qk[:, hdk:]
    vg_ref[:, :hdv] = _dot(h, w_ref[:, 2 * hdk:2 * hdk + hdv]).astype(vg_ref.dtype)
    vg_ref[:, hdv:] = _silu(_dot(h, w_ref[:, 2 * hdk + hdv:])).astype(vg_ref.dtype)
    z = _dot_split(_dot(h, wc_ref[...]), wup_hi_ref[...], wup_lo_ref[...])
    la_ref[...] = _log_sigmoid(z + bup_ref[...]) * (1.0 / GLA_TAU)


def _proj_gla(h2d, w, wc, layer, wup_hi, wup_lo, bup, *, hdk, hdv, dk, tm):
    n, d = h2d.shape
    assert w.shape[1:] == (d, 2 * hdk + 2 * hdv) and wc.shape[1:] == (d, LANES) and wup_hi.shape == (LANES, hdk)
    return pl.pallas_call(
        functools.partial(_proj_gla_kernel, hdk=hdk, hdv=hdv, qscale=dk ** -0.5),
        out_shape=(
            jax.ShapeDtypeStruct((n, 2 * hdk), F32),
            jax.ShapeDtypeStruct((n, 2 * hdv), BF16),
            jax.ShapeDtypeStruct((n, hdk), F32),
        ),
        grid=(n // tm,),
        in_specs=[
            pl.BlockSpec((tm, d), lambda i: (i, 0)),
            _resident(w.shape[1:], layer),
            _resident(wc.shape[1:], layer),
            _resident(wup_hi.shape),
            _resident(wup_lo.shape),
            _resident(bup.shape),
        ],
        out_specs=(
            pl.BlockSpec((tm, 2 * hdk), lambda i: (i, 0)),
            pl.BlockSpec((tm, 2 * hdv), lambda i: (i, 0)),
            pl.BlockSpec((tm, hdk), lambda i: (i, 0)),
        ),
        compiler_params=_params("arbitrary"),
        name="proj_gla",
    )(h2d, w, wc, wup_hi, wup_lo, bup)


def _ret_kernel(q_ref, k_ref, v_ref, g_ref, s0_ref, dm_ref, ind_ref, sd_ref, cd_ref, w_ref,
                o_ref, so_ref, st_ref, *, heads, dk, dv):
    c = pl.program_id(1)

    @pl.when(c == 0)
    def _():
        st_ref[...] = s0_ref[0]

    for hd in range(heads):
        q = q_ref[0, :, hd * dk:(hd + 1) * dk]
        k = k_ref[0, :, hd * dk:(hd + 1) * dk]
        v = v_ref[0, :, hd * dv:(hd + 1) * dv]
        s = st_ref[hd]
        p = (_dot_nt(q, k) * dm_ref[hd]).astype(BF16)
        o = _dot(p, v) + _dot(q, s.astype(BF16)) * ind_ref[hd]
        ks = (k.astype(F32) * sd_ref[hd]).astype(BF16)
        st_ref[hd] = s * cd_ref[hd] + _dot_tn(ks, v)
        oc = o - jnp.mean(o, axis=-1, keepdims=True)
        on = oc * lax.rsqrt(jnp.mean(oc * oc, axis=-1, keepdims=True) + EPS)
        gate = g_ref[0, :, hd * dv:(hd + 1) * dv].astype(F32)
        o_ref[0, :, hd * dv:(hd + 1) * dv] = (gate * (on * w_ref[:, hd * dv:(hd + 1) * dv])).astype(o_ref.dtype)

    @pl.when(c == pl.num_programs(1) - 1)
    def _():
        so_ref[0] = st_ref[...]


def _ret_tables(heads, c, dk, dv):
    log_gamma = jnp.log1p(-jnp.exp2(RET_DECAY_LOG2_BASE - jnp.arange(heads, dtype=F32)))
    idx = jnp.arange(c, dtype=F32)
    diff = idx[:, None] - idx[None, :]
    causal = diff >= 0
    dmask = jnp.where(causal[None], jnp.exp(log_gamma[:, None, None] * jnp.where(causal, diff, 0.0)[None]), 0.0)
    inner = jnp.exp(log_gamma[:, None] * (idx + 1.0))
    sdecay = jnp.exp(log_gamma[:, None] * (c - 1.0 - idx))
    cdecay = jnp.exp(log_gamma * c)
    return (dmask,
            jnp.broadcast_to(inner[:, :, None], (heads, c, dv)),
            jnp.broadcast_to(sdecay[:, :, None], (heads, c, dk)),
            jnp.broadcast_to(cdecay[:, None, None], (heads, 1, dv)))


def _retention(qkvg, s0, norm_w, *, heads, dk, dv):
    s, t, _ = qkvg.shape
    assert dk == dv
    c = _tile(t, CHUNK_RET, 2 * SUBLANES)
    w = heads * dk
    dm, ind, sd, cd = _ret_tables(heads, c, dk, dv)
    const3 = lambda b, i: (0, 0, 0)
    return pl.pallas_call(
        functools.partial(_ret_kernel, heads=heads, dk=dk, dv=dv),
        out_shape=(jax.ShapeDtypeStruct((s, t, w), BF16), jax.ShapeDtypeStruct(s0.shape, F32)),
        grid=(s, t // c),
        in_specs=[
            pl.BlockSpec((1, c, w), lambda b, i: (b, i, 0)),
            pl.BlockSpec((1, c, w), lambda b, i: (b, i, 1)),
            pl.BlockSpec((1, c, w), lambda b, i: (b, i, 2)),
            pl.BlockSpec((1, c, w), lambda b, i: (b, i, 3)),
            pl.BlockSpec((1, heads, dk, dv), lambda b, i: (b, 0, 0, 0)),
            pl.BlockSpec((heads, c, c), const3),
            pl.BlockSpec((heads, c, dv), const3),
            pl.BlockSpec((heads, c, dk), const3),
            pl.BlockSpec((heads, 1, dv), const3),
            pl.BlockSpec((1, w), lambda b, i: (0, 0)),
        ],
        out_specs=(
            pl.BlockSpec((1, c, w), lambda b, i: (b, i, 0)),
            pl.BlockSpec((1, heads, dk, dv), lambda b, i: (b, 0, 0, 0)),
        ),
        scratch_shapes=[pltpu.VMEM((heads, dk, dv), F32)],
        compiler_params=_params("arbitrary", "arbitrary"),
        name="retention",
    )(qkvg, qkvg, qkvg, qkvg, s0, dm, ind, sd, cd, norm_w.reshape(1, w))


def _gla_kernel(q_ref, k_ref, la_ref, v_ref, g_ref, s0_ref, w_ref, o_ref, so_ref, st_ref, *, heads, dk, dv, c):
    ci = pl.program_id(1)

    @pl.when(ci == 0)
    def _():
        st_ref[...] = s0_ref[0]

    row = lax.broadcasted_iota(I32, (c, dk), 0)
    differ = lax.broadcasted_iota(I32, (c, c), 0) ^ lax.broadcasted_iota(I32, (c, c), 1)
    owner = jnp.full((c, c), -1, I32)
    uppers, signs = [], []
    half = 1
    while half < c:
        owner = owner + (differ >= half).astype(I32)
        up = (row & (2 * half - 1)) >= half
        uppers.append(up)
        signs.append(jnp.where(up, LOG2_E, -LOG2_E))
        half *= 2
    for hd in range(heads):
        q = q_ref[0, :, hd * dk:(hd + 1) * dk]
        k = k_ref[0, :, hd * dk:(hd + 1) * dk]
        v = v_ref[0, :, hd * dv:(hd + 1) * dv]
        b = la_ref[0, :, hd * dk:(hd + 1) * dk]
        sh = 1
        while sh < c:
            b = b + jnp.where(row >= sh, pltpu.roll(b, sh, 0), 0.0)
            sh *= 2
        scores = jnp.where(owner < 0, _dot_nt(q.astype(BF16), k.astype(BF16)), 0.0)
        first = b
        for lvl, (upper, sign) in enumerate(zip(uppers, signs)):
            half = 1 << lvl
            mid = jnp.where(upper, first, pltpu.roll(first, c - half, 0))
            scaled = jnp.where(upper, q, k) * jnp.exp2(jnp.minimum((b - mid) * sign, 0.0))
            ql = jnp.where(upper, scaled, 0.0).astype(BF16)
            kl = jnp.where(upper, 0.0, scaled).astype(BF16)
            scores = jnp.where(owner == lvl, _dot_nt(ql, kl), scores)
            first = jnp.where(upper, pltpu.roll(first, half, 0), first)
        st = st_ref[hd]
        o = _dot(scores.astype(BF16), v) + _dot_nt((q * jnp.exp(b)).astype(BF16), st.astype(BF16))
        b_last = b[c - 1:c, :]
        kd = (k * jnp.exp(b_last - b)).astype(BF16)
        st_ref[hd] = st * jnp.exp(b_last) + _dot_tn(v, kd)
        on = o * lax.rsqrt(jnp.mean(o * o, axis=-1, keepdims=True) + EPS)
        gate = g_ref[0, :, hd * dv:(hd + 1) * dv].astype(F32)
        o_ref[0, :, hd * dv:(hd + 1) * dv] = (gate * (on * w_ref[:, hd * dv:(hd + 1) * dv])).astype(o_ref.dtype)

    @pl.when(ci == pl.num_programs(1) - 1)
    def _():
        so_ref[0] = st_ref[...]


def _gla(qk, vg, la, s0t, norm_w, *, heads, dk, dv):
    s, t, _ = qk.shape
    c = _tile(t, CHUNK_GLA, 2 * SUBLANES)
    assert c & (c - 1) == 0, "chunk length must be a power of two"
    wk, wv = heads * dk, heads * dv
    return pl.pallas_call(
        functools.partial(_gla_kernel, heads=heads, dk=dk, dv=dv, c=c),
        out_shape=(jax.ShapeDtypeStruct((s, t, wv), BF16), jax.ShapeDtypeStruct(s0t.shape, F32)),
        grid=(s, t // c),
        in_specs=[
            pl.BlockSpec((1, c, wk), lambda b, i: (b, i, 0)),
            pl.BlockSpec((1, c, wk), lambda b, i: (b, i, 1)),
            pl.BlockSpec((1, c, wk), lambda b, i: (b, i, 0)),
            pl.BlockSpec((1, c, wv), lambda b, i: (b, i, 0)),
            pl.BlockSpec((1, c, wv), lambda b, i: (b, i, 1)),
            pl.BlockSpec((1, heads, dv, dk), lambda b, i: (b, 0, 0, 0)),
            pl.BlockSpec((1, wv), lambda b, i: (0, 0)),
        ],
        out_specs=(
            pl.BlockSpec((1, c, wv), lambda b, i: (b, i, 0)),
            pl.BlockSpec((1, heads, dv, dk), lambda b, i: (b, 0, 0, 0)),
        ),
        scratch_shapes=[pltpu.VMEM((heads, dv, dk), F32)],
        compiler_params=_params("arbitrary", "arbitrary"),
        name="gla",
    )(qk, qk, la, vg, vg, s0t, norm_w.reshape(1, wv))


def _merge_kernel(br_ref, bg_ref, h_ref, wr_ref, wg_ref, wm0_ref, wm1_ref, bm0_ref, bm1_ref, o_ref):
    h = h_ref[...]
    g0 = _sigmoid(_dot(h, wm0_ref[...]) + bm0_ref[...])
    g1 = _sigmoid(_dot(h, wm1_ref[...]) + bm1_ref[...])
    y = g0 * _dot(br_ref[...], wr_ref[...]) + g1 * _dot(bg_ref[...], wg_ref[...])
    o_ref[...] = y.astype(o_ref.dtype)


def _merge(o_ret, o_gla, h2d, w_branch, w_merge, layer, b_merge, *, tm):
    n, d = h2d.shape
    wdt = o_ret.shape[1]
    tn = _tile(d, 512, LANES)
    nj = d // tn
    b2 = b_merge.reshape(1, 2 * d)
    return pl.pallas_call(
        _merge_kernel,
        out_shape=jax.ShapeDtypeStruct((n, d), BF16),
        grid=(n // tm, nj),
        in_specs=[
            pl.BlockSpec((tm, wdt), lambda i, j: (i, 0)),
            pl.BlockSpec((tm, wdt), lambda i, j: (i, 0)),
            pl.BlockSpec((tm, d), lambda i, j: (i, 0)),
            pl.BlockSpec((None, wdt, tn), lambda i, j: (2 * layer, 0, j)),
            pl.BlockSpec((None, wdt, tn), lambda i, j: (2 * layer + 1, 0, j)),
            pl.BlockSpec((None, d, tn), lambda i, j: (layer, 0, j)),
            pl.BlockSpec((None, d, tn), lambda i, j: (layer, 0, j + nj)),
            pl.BlockSpec((1, tn), lambda i, j: (0, j)),
            pl.BlockSpec((1, tn), lambda i, j: (0, j + nj)),
        ],
        out_specs=pl.BlockSpec((tm, tn), lambda i, j: (i, j)),
        compiler_params=_params("arbitrary", "arbitrary"),
        name="branch_merge",
    )(o_ret, o_gla, h2d, w_branch, w_branch, w_merge, w_merge, b2, b2)


def _pack_bf16_pairs(h):
    half = h.shape[-1] // 2
    a = lax.bitcast_convert_type(h[:, :half].astype(BF16).astype(F32), U32)
    b = lax.bitcast_convert_type(h[:, half:].astype(BF16).astype(F32), U32)
    return a | (b >> 16)


def _unpack_bf16_pairs(w):
    a = lax.bitcast_convert_type(w & jnp.uint32(0xFFFF0000), F32)
    b = lax.bitcast_convert_type(w << 16, F32)
    return jnp.concatenate([a, b], axis=-1).astype(BF16)


def _route(logits, n_groups, per_group):
    lane = lax.broadcasted_iota(I32, logits.shape, 1).astype(F32)
    neg = jnp.float32(-jnp.inf)

    def first_max(mask):
        m = jnp.max(jnp.where(mask, logits, neg), axis=-1, keepdims=True)
        idx = jnp.min(jnp.where(mask & (logits == m), lane, float(LANES)), axis=-1, keepdims=True)
        return m, idx

    gmask = lane < n_groups
    gmax, gidx = first_max(gmask)
    p_group = 1.0 / jnp.sum(jnp.where(gmask, jnp.exp(logits - gmax), 0.0), axis=-1, keepdims=True)
    lo = n_groups + gidx * per_group
    emask = (lane >= lo) & (lane < lo + per_group)
    m1, i1 = first_max(emask)
    m2, i2 = first_max(emask & (lane != i1))
    w1 = 1.0 / (1.0 + jnp.exp(m2 - m1))
    w2 = 1.0 - w1
    return jnp.where(lane == 0.0, i1 - n_groups, jnp.where(lane == 1.0, i2 - n_groups, jnp.where(
        lane == 2.0, p_group * w1, jnp.where(lane == 3.0, p_group * w2, 0.0))))


def _with_ranks(rt, carry_ref):
    tr = rt.shape[0]
    lane = lax.broadcasted_iota(I32, rt.shape, 1).astype(F32)
    a1 = lane == rt[:, 0:1]
    a2 = lane == rt[:, 1:2]
    hit = jnp.where(a1 | a2, 1.0, 0.0)
    ri = lax.broadcasted_iota(I32, (tr, tr), 0)
    cj = lax.broadcasted_iota(I32, (tr, tr), 1)
    before = _dot(jnp.where(ri > cj, 1.0, 0.0).astype(BF16), hit.astype(BF16)) + carry_ref[...]
    k1 = jnp.sum(jnp.where(a1, before, 0.0), axis=-1, keepdims=True)
    k2 = jnp.sum(jnp.where(a2, before, 0.0), axis=-1, keepdims=True)
    carry_ref[...] += jnp.sum(hit, axis=0, keepdims=True)
    return jnp.where(lane == 4.0, k1, jnp.where(lane == 5.0, k2, rt))


def _out_proj_kernel(m_ref, x_ref, g1_ref, sc_ref, sh_ref, nw_ref, wo_ref, wr_ref, br_ref,
                     x1_ref, hp_ref, rt_ref, cnt_ref, carry_ref, *, n_groups, per_group):
    bs, tt, d = x_ref.shape

    @pl.when((pl.program_id(0) == 0) & (pl.program_id(1) == 0))
    def _():
        carry_ref[...] = jnp.zeros_like(carry_ref)

    y = _dot(m_ref[...].reshape(bs * tt, d), wo_ref[...]).reshape(bs, tt, d)
    x1 = x_ref[...] + g1_ref[...] * y
    x1_ref[...] = x1
    hn = x1 * lax.rsqrt(jnp.mean(x1 * x1, axis=-1, keepdims=True) + EPS) * nw_ref[...]
    h = (hn * (1.0 + sc_ref[...]) + sh_ref[...]).reshape(bs * tt, d)
    _store_token_major(hp_ref, _pack_bf16_pairs(h))
    h_hi, h_lo = _split_bf16(h)
    p = _dot(h_hi, wr_ref[...])
    logits = p[:, :LANES] + p[:, LANES:] + _dot(h_lo, wr_ref[:, :LANES]) + br_ref[...]
    rt = _with_ranks(_route(logits, n_groups, per_group), carry_ref)
    rt_ref[...] = rt.reshape(bs, tt, LANES)
    cnt_ref[...] = carry_ref[...]


def _out_proj(merged, x, g1, sc2, sh2, norm_w, w_o, layer, wr, b_r, *, n_groups, per_group):
    s, t, d = x.shape
    bs, tt = _seq_tiles(s, t, ROWS_OUT_PROJ)
    chunks = d // 2 // LANES
    tok = lambda i, j: (i, j, 0)
    seq = lambda i, j: (i, 0, 0)
    const = lambda i, j: (0, 0)
    return pl.pallas_call(
        functools.partial(_out_proj_kernel, n_groups=n_groups, per_group=per_group),
        out_shape=(
            jax.ShapeDtypeStruct((s, t, d), F32),
            jax.ShapeDtypeStruct((s * t * chunks, LANES), U32),
            jax.ShapeDtypeStruct((s, t, LANES), F32),
            jax.ShapeDtypeStruct((1, LANES), F32),
        ),
        grid=(s // bs, t // tt),
        in_specs=[
            pl.BlockSpec((bs, tt, d), tok),
            pl.BlockSpec((bs, tt, d), tok),
            pl.BlockSpec((bs, 1, d), seq),
            pl.BlockSpec((bs, 1, d), seq),
            pl.BlockSpec((bs, 1, d), seq),
            pl.BlockSpec((1, d), const),
            pl.BlockSpec((None, d, d), lambda i, j: (layer, 0, 0), pipeline_mode=pl.Buffered(1)),
            pl.BlockSpec((d, 2 * LANES), const, pipeline_mode=pl.Buffered(1)),
            pl.BlockSpec((1, LANES), const),
        ],
        out_specs=(
            pl.BlockSpec((bs, tt, d), tok),
            pl.BlockSpec((bs * tt * chunks, LANES), lambda i, j: (i * (t // tt) + j, 0)),
            pl.BlockSpec((bs, tt, LANES), tok),
            pl.BlockSpec((1, LANES), const),
        ),
        scratch_shapes=[pltpu.VMEM((1, LANES), F32)],
        compiler_params=_params("arbitrary", "arbitrary"),
        name="out_proj_router",
    )(merged, x, g1, sc2, sh2, norm_w.reshape(1, d), w_o, wr, b_r)


def _invert_kernel(pos_ref, src_ref, *, n_tokens, n_rows):
    def clear(j, carry):
        src_ref[j] = 0
        return carry

    lax.fori_loop(0, n_rows, clear, 0, unroll=INVERT_UNROLL)

    def put(t, carry):
        for kk in range(TOP_K):
            src_ref[pos_ref[TOP_K * t + kk]] = t
        return carry

    lax.fori_loop(0, n_tokens, put, 0, unroll=INVERT_UNROLL)


def _invert(pos, n_rows):
    n_tokens = pos.shape[0] // TOP_K
    return pl.pallas_call(
        functools.partial(_invert_kernel, n_tokens=n_tokens, n_rows=n_rows),
        out_shape=jax.ShapeDtypeStruct((n_rows,), I32),
        grid_spec=pltpu.PrefetchScalarGridSpec(
            num_scalar_prefetch=1,
            grid=(1,),
            in_specs=[],
            out_specs=pl.BlockSpec(memory_space=pltpu.SMEM),
        ),
        compiler_params=_params("arbitrary"),
        name="moe_invert",
    )(pos)


def _row_copy(src_ref, src_row, dst_ref, dst_row, sem):
    return pltpu.make_async_copy(src_ref.at[pl.ds(src_row, 1)], dst_ref.at[pl.ds(dst_row, 1)], sem)


def _store_token_major(ref, value):
    chunks = value.shape[1] // LANES
    for c in range(chunks):
        ref[pl.ds(c, value.shape[0], stride=chunks), :] = value[:, c * LANES:(c + 1) * LANES]


def _load_token_major(ref, rows, chunks):
    return jnp.concatenate([ref[pl.ds(c, rows, stride=chunks), :] for c in range(chunks)], axis=-1)


def _token_copy(src_ref, src_tok, dst_ref, dst_tok, chunks, sem):
    def at(ref, tok):
        start = tok * chunks
        if not isinstance(start, int):
            start = pl.multiple_of(start, chunks)
        return ref.at[pl.ds(start, chunks)]

    return pltpu.make_async_copy(at(src_ref, src_tok), at(dst_ref, dst_tok), sem)


def _ffn_kernel(te_ref, valid_ref, src_ref, hp_ref, wg_ref, wu_ref, wd_ref, ys_ref, xbuf, sems, *, tf, chunks):
    del te_ref
    i = pl.program_id(0)
    last = pl.num_programs(0) - 1
    slot = i % 2
    nxt = jnp.where(i < last, i + 1, 0)

    def start(tile, slt, r):
        _token_copy(hp_ref, src_ref[tile * tf + r], xbuf.at[slt], r, chunks, sems.at[slt]).start()

    def drain(slt):
        def wait(r, carry):
            _token_copy(hp_ref, 0, xbuf.at[slt], 0, chunks, sems.at[slt]).wait()
            return carry

        lax.fori_loop(0, tf, wait, 0, unroll=GATHER_UNROLL)

    @pl.when(i == 0)
    def _():
        def issue(r, carry):
            start(0, 0, r)
            return carry

        lax.fori_loop(0, tf, issue, 0, unroll=GATHER_UNROLL)

    for r in range(tf):
        start(nxt, 1 - slot, r)

    drain(slot)

    @pl.when(valid_ref[i] != 0)
    def _():
        x = _unpack_bf16_pairs(_load_token_major(xbuf.at[slot], tf, chunks))
        act = (_silu(_dot(x, wg_ref[...])) * _dot(x, wu_ref[...])).astype(BF16)
        ys_ref[...] = _dot(act, wd_ref[...])

    @pl.when(valid_ref[i] == 0)
    def _():
        ys_ref[...] = jnp.zeros_like(ys_ref)

    @pl.when(i == last)
    def _():
        drain(1 - slot)


def _ffn(tile_expert, tile_valid, src, hp, wg, wu, wd, *, tf):
    n_rows = src.shape[0]
    _, d, f = wg.shape
    chunks = d // 2 // LANES
    return pl.pallas_call(
        functools.partial(_ffn_kernel, tf=tf, chunks=chunks),
        out_shape=jax.ShapeDtypeStruct((n_rows, d), F32),
        grid_spec=pltpu.PrefetchScalarGridSpec(
            num_scalar_prefetch=3,
            grid=(n_rows // tf,),
            in_specs=[
                pl.BlockSpec(memory_space=pl.ANY),
                pl.BlockSpec((None, d, f), lambda i, te, tv, sr: (te[i], 0, 0)),
                pl.BlockSpec((None, d, f), lambda i, te, tv, sr: (te[i], 0, 0)),
                pl.BlockSpec((None, f, d), lambda i, te, tv, sr: (te[i], 0, 0)),
            ],
            out_specs=pl.BlockSpec((tf, d), lambda i, te, tv, sr: (i, 0)),
            scratch_shapes=[pltpu.VMEM((2, tf * chunks, LANES), hp.dtype), pltpu.SemaphoreType.DMA((2,))],
        ),
        compiler_params=_params("arbitrary"),
        name="moe_ffn",
    )(tile_expert, tile_valid, src, hp, wg, wu, wd)


def _combine_kernel(pos_ref, x_ref, rt_ref, g2_ref, nw_ref, sc_ref, sh_ref, ys_ref, *rest, last):
    if last:
        o_ref, ybuf, sems = rest
    else:
        o_ref, h_ref, ybuf, sems = rest
    bs, tt, d = x_ref.shape
    rows = bs * tt
    step = pl.program_id(0) * pl.num_programs(1) + pl.program_id(1)
    n_steps = pl.num_programs(0) * pl.num_programs(1)

    def start(stp, slt, r, kk):
        _row_copy(ys_ref, pos_ref[TOP_K * (stp * rows + r) + kk], ybuf.at[slt, kk], r, sems.at[slt]).start()

    @pl.when(step == 0)
    def _():
        def issue(r, carry):
            for kk in range(TOP_K):
                start(0, 0, r, kk)
            return carry

        lax.fori_loop(0, rows, issue, 0, unroll=GATHER_UNROLL)

    slot = step % 2

    @pl.when(step + 1 < n_steps)
    def _():
        for r in range(rows):
            for kk in range(TOP_K):
                start(step + 1, 1 - slot, r, kk)

    def wait(r, carry):
        for kk in range(TOP_K):
            _row_copy(ys_ref, 0, ybuf.at[slot, kk], 0, sems.at[slot]).wait()
        return carry

    lax.fori_loop(0, rows, wait, 0, unroll=GATHER_UNROLL)

    rt = rt_ref[...].reshape(rows, LANES)
    y = rt[:, 2:3] * ybuf[slot, 0] + rt[:, 3:4] * ybuf[slot, 1]
    xn = x_ref[...] + g2_ref[...] * y.reshape(bs, tt, d)
    normed = xn * lax.rsqrt(jnp.mean(xn * xn, axis=-1, keepdims=True) + EPS) * nw_ref[...]
    if last:
        o_ref[...] = normed
    else:
        o_ref[...] = xn
        h_ref[...] = (normed * (1.0 + sc_ref[...]) + sh_ref[...]).astype(h_ref.dtype)


def _combine(pos, x1, route, g2, norm_w, sc, sh, ys, *, last):
    s, t, d = x1.shape
    bs, tt = _seq_tiles(s, t, ROWS_COMBINE)
    tok = lambda i, j, p: (i, j, 0)
    seq = lambda i, j, p: (i, 0, 0)
    x_shape = jax.ShapeDtypeStruct((s, t, d), F32)
    x_spec = pl.BlockSpec((bs, tt, d), tok)
    return pl.pallas_call(
        functools.partial(_combine_kernel, last=last),
        out_shape=x_shape if last else (x_shape, jax.ShapeDtypeStruct((s, t, d), BF16)),
        grid_spec=pltpu.PrefetchScalarGridSpec(
            num_scalar_prefetch=1,
            grid=(s // bs, t // tt),
            in_specs=[
                x_spec,
                pl.BlockSpec((bs, tt, LANES), tok),
                pl.BlockSpec((bs, 1, d), seq),
                pl.BlockSpec((1, d), lambda i, j, p: (0, 0)),
                pl.BlockSpec((bs, 1, d), seq),
                pl.BlockSpec((bs, 1, d), seq),
                pl.BlockSpec(memory_space=pl.ANY),
            ],
            out_specs=x_spec if last else (x_spec, x_spec),
            scratch_shapes=[pltpu.VMEM((2, TOP_K, bs * tt, d), F32), pltpu.SemaphoreType.DMA((2,))],
        ),
        compiler_params=_params("arbitrary", "arbitrary"),
        name="moe_combine",
    )(pos, x1, route, g2, norm_w.reshape(1, d), sc, sh, ys)


def _moe(x1, hp, route, counts, g2, wg, wu, wd, layer, n_exp, norm_w, sc, sh, *, last):
    s, t, d = x1.shape
    n = s * t
    route2d = route.reshape(n, LANES)
    tf = 2 * SUBLANES
    while tf < ROWS_FFN and tf < 2 * TOP_K * n // n_exp:
        tf *= 2
    n_tiles = (TOP_K * n) // tf + n_exp
    cnt = counts[0, :n_exp].astype(I32)
    padded = ((cnt + tf - 1) // tf) * tf
    ends = jnp.cumsum(padded)
    starts = ends - padded
    expert = route2d[:, :TOP_K].astype(I32)
    rank = route2d[:, 4:4 + TOP_K].astype(I32)
    start_of = jnp.sum(jnp.where(expert[..., None] == jnp.arange(n_exp, dtype=I32), starts, 0), axis=-1)
    pos = (start_of + rank).reshape(TOP_K * n)
    tile_start = jnp.arange(n_tiles, dtype=I32) * tf
    tile_valid = (tile_start < ends[-1]).astype(I32)
    tile_expert = jnp.minimum(jnp.sum((ends[None, :] <= tile_start[:, None]).astype(I32), axis=1), n_exp - 1)
    last_used = jnp.max(jnp.where(tile_valid != 0, tile_expert, 0))
    tile_expert = jnp.where(tile_valid != 0, tile_expert, last_used)
    src = _invert(pos, n_tiles * tf)
    ys = _ffn(tile_expert + layer * n_exp, tile_valid, src, hp, wg, wu, wd, tf=tf)
    return _combine(pos, x1, route, g2, norm_w, sc, sh, ys, last=last)


def _rotary_tables(pos0, t, dk, reps):
    half = dk // 2
    inv = ROPE_BASE ** (-jnp.arange(half, dtype=F32) / half)
    ang = (pos0 + jnp.arange(t)).astype(F32)[:, None] * inv[None, :]
    cos = jnp.cos(ang)
    sin = jnp.sin(ang)
    cos2 = jnp.concatenate([cos, cos], axis=-1)
    sin2 = jnp.concatenate([-sin, sin], axis=-1)
    return jnp.tile(cos2, (reps, 1)), jnp.tile(sin2, (reps, 1))


def _trunk(x, mods, pos0, s_ret_in, s_gla_in_t, big, wts, dims):
    s, t, d = x.shape
    n = s * t
    depth = len(wts)
    rh, rdk, rdv, gh, gdk, gdv = dims["ret_heads"], dims["ret_dk"], dims["ret_dv"], dims["gla_heads"], dims["gla_dk"], dims["gla_dv"]
    bs, tt = _seq_tiles(s, t, ROWS_PROJ)
    tp = bs * tt
    tm = _tile(n, ROWS_MATMUL, 2 * SUBLANES)
    cos, sin = _rotary_tables(pos0, t, rdk, bs)
    ret_states, gla_states = [], []
    h = _norm(x, wts[0]["norm_mix"], mods[1][0], mods[0][0])
    for l in range(depth):
        w = wts[l]
        sh1, sc1, g1, sh2, sc2, g2 = (m[l] for m in mods)
        h2d = h.reshape(n, d)
        qkvg = _proj_ret(h2d, big["w_ret"], l, cos, sin, heads=rh, dk=rdk, tm=tp)
        qk, vg, la = _proj_gla(h2d, big["w_gla"], big["w_code"], l, w["wup_hi"], w["wup_lo"], w["b_gla"],
                               hdk=gh * gdk, hdv=gh * gdv, dk=gdk, tm=tp)
        o_ret, s_r = _retention(qkvg.reshape(s, t, -1), s_ret_in[l], w["ret_norm"], heads=rh, dk=rdk, dv=rdv)
        o_gla, s_g = _gla(qk.reshape(s, t, -1), vg.reshape(s, t, -1), la.reshape(s, t, -1), s_gla_in_t[l],
                          w["gla_norm"], heads=gh, dk=gdk, dv=gdv)
        merged = _merge(o_ret.reshape(n, -1), o_gla.reshape(n, -1), h2d, big["w_branch"], big["w_merge"], l,
                        w["b_merge"], tm=tm)
        x1, hp, route, counts = _out_proj(merged.reshape(s, t, d), x, g1, sc2, sh2, w["norm_ffn"], big["w_o"], l,
                                  w["w_router"], w["b_r"], n_groups=dims["n_groups"], per_group=dims["per_group"])
        n_exp = dims["n_groups"] * dims["per_group"]
        experts = (big["w_exp_gate"], big["w_exp_up"], big["w_exp_down"], l, n_exp)
        if l == depth - 1:
            x = _moe(x1, hp, route, counts, g2, *experts, w["final_norm"], sc1, sh1, last=True)
        else:
            x, h = _moe(x1, hp, route, counts, g2, *experts, wts[l + 1]["norm_mix"], mods[1][l + 1], mods[0][l + 1], last=False)
        ret_states.append(s_r)
        gla_states.append(jnp.swapaxes(s_g, -1, -2))
    return x, jnp.stack(ret_states), jnp.stack(gla_states)


def kernel(x_prompt, x_sample, state_ret, state_gla, c_prompt, c_sample, w_ada, b_ada, norm_mix_w, norm_ffn_w, w_in, w_gla_up, b_gla, ret_norm_w, gla_norm_w, w_branch, w_merge, b_merge, w_o, w_router_group, b_router_group, w_router_expert, b_router_expert, w_exp_gate, w_exp_up, w_exp_down, final_norm_w):
    depth, d, _ = w_in.shape
    _, _, rh, rdk, rdv = state_ret.shape
    _, _, gh, gdk, gdv = state_gla.shape
    low_rank = w_gla_up.shape[1]
    n_groups = w_router_group.shape[-1]
    n_exp = w_router_expert.shape[-1]
    assert low_rank <= LANES and n_groups + n_exp <= LANES and n_exp % n_groups == 0
    dims = dict(ret_heads=rh, ret_dk=rdk, ret_dv=rdv, gla_heads=gh, gla_dk=gdk, gla_dv=gdv,
                n_groups=n_groups, per_group=n_exp // n_groups)

    n_ret = 2 * rh * rdk + 2 * rh * rdv
    n_gla = 2 * gh * gdk + 2 * gh * gdv
    assert w_in.shape[-1] == n_ret + n_gla + low_rank
    f = w_exp_gate.shape[-1]
    big = dict(
        w_ret=_to_bf16(w_in, 0, n_ret),
        w_gla=_to_bf16(w_in, n_ret, n_gla),
        w_code=_to_bf16(w_in, n_ret + n_gla, LANES, valid_cols=low_rank),
        w_branch=_to_bf16(w_branch.reshape(depth, -1, d)).reshape(depth * w_branch.shape[1], -1, d),
        w_merge=_to_bf16(w_merge),
        w_o=_to_bf16(w_o),
        w_exp_gate=_to_bf16(w_exp_gate.reshape(depth, n_exp * d, f)).reshape(depth * n_exp, d, f),
        w_exp_up=_to_bf16(w_exp_up.reshape(depth, n_exp * d, f)).reshape(depth * n_exp, d, f),
        w_exp_down=_to_bf16(w_exp_down.reshape(depth, n_exp * f, d)).reshape(depth * n_exp, f, d),
    )
    w_router = jnp.concatenate([w_router_group, w_router_expert], axis=-1)
    w_router = jnp.pad(w_router, ((0, 0), (0, 0), (0, LANES - n_groups - n_exp)))
    b_router = jnp.pad(jnp.concatenate([b_router_group, b_router_expert], axis=-1), ((0, 0), (0, LANES - n_groups - n_exp)))
    wup = jnp.pad(w_gla_up, ((0, 0), (0, LANES - low_rank), (0, 0)))
    wts = []
    for l in range(depth):
        wr_hi, wr_lo = _split_bf16(w_router[l])
        wup_hi, wup_lo = _split_bf16(wup[l])
        wts.append(dict(
            norm_mix=norm_mix_w[l], norm_ffn=norm_ffn_w[l],
            wup_hi=wup_hi, wup_lo=wup_lo, b_gla=b_gla[l].reshape(1, -1),
            ret_norm=ret_norm_w[l], gla_norm=gla_norm_w[l], b_merge=b_merge[l],
            w_router=jnp.concatenate([wr_hi, wr_lo], axis=-1), b_r=b_router[l].reshape(1, LANES),
            final_norm=final_norm_w,
        ))

    nb, ns = c_prompt.shape[0], c_sample.shape[0]
    mod = _ada(jnp.concatenate([c_prompt, c_sample], axis=0), w_ada, b_ada)
    mod = mod.reshape(depth, nb + ns, 6, d)
    mods_p = [mod[:, :nb, i][:, :, None, :] for i in range(6)]
    mods_s = [mod[:, nb:, i][:, :, None, :] for i in range(6)]

    zero_ret = jnp.zeros((depth, nb, rh, rdk, rdv), F32)
    zero_gla_t = jnp.zeros((depth, nb, gh, gdv, gdk), F32)
    y_p, ret_p, gla_p = _trunk(x_prompt, mods_p, 0, zero_ret, zero_gla_t, big, wts, dims)
    y_s, ret_s, gla_s = _trunk(x_sample, mods_s, PAST_LEN, state_ret.astype(F32),
                               jnp.swapaxes(state_gla.astype(F32), -1, -2), big, wts, dims)
    return (y_p, y_s, ret_p, gla_p, ret_s, gla_s)
```

```python
import functools

import jax
import jax.numpy as jnp
from jax import lax
from jax.experimental import pallas as pl
from jax.experimental.pallas import tpu as pltpu

F32 = jnp.float32
BF16 = jnp.bfloat16
U32 = jnp.uint32
I32 = jnp.int32

PAST_LEN = 4096
GLA_TAU = 16.0
ROPE_BASE = 10000.0
EPS = 1e-6
RET_DECAY_LOG2_BASE = -5.0
TOP_K = 2
LOG2_E = 1.4426950408889634

LANES = 128
SUBLANES = 8
V7X_VMEM_BYTES = 64 * 1024 * 1024
VMEM_LIMIT = V7X_VMEM_BYTES - 8 * 1024 * 1024

ROWS_MATMUL = 1024
ROWS_PROJ = 512
ROWS_OUT_PROJ = 512
ROWS_NORM = 512
ROWS_COMBINE = 256
GATHER_UNROLL = 8
INVERT_UNROLL = 16
CAST_BLOCK_COLS = 1024
CAST_BLOCK_ELEMS = 2 * 1024 * 1024
ROWS_FFN = 512
CHUNK_MIXERS = 256


def _params(*sem):
    return pltpu.CompilerParams(dimension_semantics=sem, vmem_limit_bytes=VMEM_LIMIT)


def _tile(n, pref, align):
    if n <= pref:
        return n
    t = (pref // align) * align
    while t >= align:
        if n % t == 0:
            return t
        t -= align
    return n


def _seq_tiles(s, t, rows):
    if t >= rows:
        return 1, _tile(t, rows, 2 * SUBLANES)
    return _tile(s, max(rows // t, 1), 1), t


def _dot(a, b):
    return jnp.dot(a, b, preferred_element_type=F32)


def _dot_nt(a, b):
    return lax.dot_general(a, b, (((1,), (1,)), ((), ())), preferred_element_type=F32)


def _dot_tn(a, b):
    return lax.dot_general(a, b, (((0,), (0,)), ((), ())), preferred_element_type=F32)


def _sigmoid(x):
    return 0.5 * jnp.tanh(0.5 * x) + 0.5


def _silu(x):
    return x * _sigmoid(x)


def _log_sigmoid(z):
    return jnp.minimum(z, 0.0) - jnp.log(1.0 + jnp.exp(-jnp.abs(z)))


def _split_bf16(x):
    hi = x.astype(BF16)
    lo = (x - hi.astype(F32)).astype(BF16)
    return hi, lo


def _dot_split(a, w_hi, w_lo):
    a_hi, a_lo = _split_bf16(a)
    return _dot(a_hi, w_hi) + _dot(a_lo, w_hi) + _dot(a_hi, w_lo)


def _ada_kernel(c_ref, w_ref, b_ref, o_ref):
    a = _silu(c_ref[...]).astype(BF16)
    o_ref[...] = _dot(a, w_ref[...].astype(BF16)) + b_ref[...]


def _ada(c_all, w_ada, b_ada):
    depth, d, n6 = w_ada.shape
    r = c_all.shape[0]
    tn = _tile(n6, 1024, LANES)
    return pl.pallas_call(
        _ada_kernel,
        out_shape=jax.ShapeDtypeStruct((depth, r, n6), F32),
        grid=(depth, n6 // tn),
        in_specs=[
            pl.BlockSpec((r, d), lambda l, j: (0, 0)),
            pl.BlockSpec((None, d, tn), lambda l, j: (l, 0, j)),
            pl.BlockSpec((None, 1, tn), lambda l, j: (l, 0, j)),
        ],
        out_specs=pl.BlockSpec((None, r, tn), lambda l, j: (l, 0, j)),
        compiler_params=_params("arbitrary", "arbitrary"),
        name="ada_mod",
    )(c_all, w_ada, b_ada.reshape(depth, 1, n6))


def _norm_kernel(x_ref, w_ref, sc_ref, sh_ref, o_ref):
    x = x_ref[...]
    y = x * lax.rsqrt(jnp.mean(x * x, axis=-1, keepdims=True) + EPS) * w_ref[...]
    o_ref[...] = (y * (1.0 + sc_ref[...]) + sh_ref[...]).astype(o_ref.dtype)


def _norm(x, w, sc, sh):
    s, t, d = x.shape
    bs, tt = _seq_tiles(s, t, ROWS_NORM)
    return pl.pallas_call(
        _norm_kernel,
        out_shape=jax.ShapeDtypeStruct((s, t, d), BF16),
        grid=(s // bs, t // tt),
        in_specs=[
            pl.BlockSpec((bs, tt, d), lambda i, j: (i, j, 0)),
            pl.BlockSpec((1, d), lambda i, j: (0, 0)),
            pl.BlockSpec((bs, 1, d), lambda i, j: (i, 0, 0)),
            pl.BlockSpec((bs, 1, d), lambda i, j: (i, 0, 0)),
        ],
        out_specs=pl.BlockSpec((bs, tt, d), lambda i, j: (i, j, 0)),
        compiler_params=_params("arbitrary", "arbitrary"),
        name="mod_norm",
    )(x, w.reshape(1, d), sc, sh)


def _resident(shape, layer=None):
    if layer is None:
        return pl.BlockSpec(shape, lambda i: (0,) * len(shape), pipeline_mode=pl.Buffered(1))
    return pl.BlockSpec((None,) + tuple(shape), lambda i: (layer,) + (0,) * len(shape), pipeline_mode=pl.Buffered(1))


def _cast_kernel(x_ref, o_ref, *, valid_cols):
    x = x_ref[...]
    if valid_cols is not None:
        x = jnp.where(lax.broadcasted_iota(I32, x.shape, 1) < valid_cols, x, 0.0)
    o_ref[...] = x.astype(o_ref.dtype)


def _to_bf16(w, col0=0, ncols=None, valid_cols=None):
    depth, rows, cols = w.shape
    ncols = cols - col0 if ncols is None else ncols
    tc = LANES
    while tc * 2 <= CAST_BLOCK_COLS and col0 % (tc * 2) == 0 and ncols % (tc * 2) == 0:
        tc *= 2
    assert col0 % tc == 0 and ncols % tc == 0 and (valid_cols is None or ncols == tc)
    tr = _tile(rows, max(CAST_BLOCK_ELEMS // tc, 2 * SUBLANES), 2 * SUBLANES)
    c0 = col0 // tc
    return pl.pallas_call(
        functools.partial(_cast_kernel, valid_cols=valid_cols),
        out_shape=jax.ShapeDtypeStruct((depth, rows, ncols), BF16),
        grid=(depth, rows // tr, ncols // tc),
        in_specs=[pl.BlockSpec((None, tr, tc), lambda l, i, j: (l, i, c0 + j))],
        out_specs=pl.BlockSpec((None, tr, tc), lambda l, i, j: (l, i, j)),
        compiler_params=_params("arbitrary", "arbitrary", "arbitrary"),
        name="cast_bf16",
    )(w)


def _proj_ret_kernel(h_ref, w_ref, cos_ref, sin_ref, o_ref, *, heads, dk, kscale):
    h = h_ref[...]
    tn = heads * dk
    cos = cos_ref[...]
    sin = sin_ref[...]
    for sec, mult in ((0, 1.0), (1, kscale)):
        acc = _dot(h, w_ref[:, sec * tn:(sec + 1) * tn])
        for hd in range(heads):
            a = acc[:, hd * dk:(hd + 1) * dk]
            r = a * cos + pltpu.roll(a, dk // 2, 1) * sin
            o_ref[:, sec * tn + hd * dk:sec * tn + (hd + 1) * dk] = (r * mult).astype(o_ref.dtype)
    o_ref[:, 2 * tn:3 * tn] = _dot(h, w_ref[:, 2 * tn:3 * tn]).astype(o_ref.dtype)
    o_ref[:, 3 * tn:] = _silu(_dot(h, w_ref[:, 3 * tn:])).astype(o_ref.dtype)


def _proj_ret(h2d, w, layer, cos, sin, *, heads, dk, tm):
    n, d = h2d.shape
    tn = heads * dk
    assert w.shape[1:] == (d, 4 * tn) and cos.shape[0] % tm == 0
    nt = cos.shape[0] // tm
    return pl.pallas_call(
        functools.partial(_proj_ret_kernel, heads=heads, dk=dk, kscale=dk ** -0.5),
        out_shape=jax.ShapeDtypeStruct((n, 4 * tn), BF16),
        grid=(n // tm,),
        in_specs=[
            pl.BlockSpec((tm, d), lambda i: (i, 0)),
            _resident((d, 4 * tn), layer),
            pl.BlockSpec((tm, dk), lambda i: (i % nt, 0)),
            pl.BlockSpec((tm, dk), lambda i: (i % nt, 0)),
        ],
        out_specs=pl.BlockSpec((tm, 4 * tn), lambda i: (i, 0)),
        compiler_params=_params("arbitrary"),
        name="proj_ret",
    )(h2d, w, cos, sin)


def _proj_gla_kernel(h_ref, w_ref, wc_ref, wup_hi_ref, wup_lo_ref, bup_ref, qk_ref, vg_ref, la_ref, *, hdk, hdv, qscale):
    h = h_ref[...]
    qk = _dot(h, w_ref[:, :2 * hdk])
    qk_ref[:, :hdk] = qk[:, :hdk] * qscale
    qk_ref[:, hdk:] = qk[:, hdk:]
    vg_ref[:, :hdv] = _dot(h, w_ref[:, 2 * hdk:2 * hdk + hdv]).astype(vg_ref.dtype)
    vg_ref[:, hdv:] = _silu(_dot(h, w_ref[:, 2 * hdk + hdv:])).astype(vg_ref.dtype)
    z = _dot_split(_dot(h, wc_ref[...]), wup_hi_ref[...], wup_lo_ref[...])
    la_ref[...] = _log_sigmoid(z + bup_ref[...]) * (1.0 / GLA_TAU)


def _proj_gla(h2d, w, wc, layer, wup_hi, wup_lo, bup, *, hdk, hdv, dk, tm):
    n, d = h2d.shape
    assert w.shape[1:] == (d, 2 * hdk + 2 * hdv) and wc.shape[1:] == (d, LANES) and wup_hi.shape == (LANES, hdk)
    return pl.pallas_call(
        functools.partial(_proj_gla_kernel, hdk=hdk, hdv=hdv, qscale=dk ** -0.5),
        out_shape=(
            jax.ShapeDtypeStruct((n, 2 * hdk), F32),
            jax.ShapeDtypeStruct((n, 2 * hdv), BF16),
            jax.ShapeDtypeStruct((n, hdk), F32),
        ),
        grid=(n // tm,),
        in_specs=[
            pl.BlockSpec((tm, d), lambda i: (i, 0)),
            _resident(w.shape[1:], layer),
            _resident(wc.shape[1:], layer),
            _resident(wup_hi.shape),
            _resident(wup_lo.shape),
            _resident(bup.shape),
        ],
        out_specs=(
            pl.BlockSpec((tm, 2 * hdk), lambda i: (i, 0)),
            pl.BlockSpec((tm, 2 * hdv), lambda i: (i, 0)),
            pl.BlockSpec((tm, hdk), lambda i: (i, 0)),
        ),
        compiler_params=_params("arbitrary"),
        name="proj_gla",
    )(h2d, w, wc, wup_hi, wup_lo, bup)


def _ret_heads(q_ref, k_ref, v_ref, g_ref, dm_ref, ind_ref, sd_ref, cd_ref, w_ref, o_ref, st_ref, *, heads, dk, dv):
    for hd in range(heads):
        q = q_ref[0, :, hd * dk:(hd + 1) * dk]
        k = k_ref[0, :, hd * dk:(hd + 1) * dk]
        v = v_ref[0, :, hd * dv:(hd + 1) * dv]
        s = st_ref[hd]
        p = (_dot_nt(q, k) * dm_ref[hd]).astype(BF16)
        o = _dot(p, v) + _dot(q, s.astype(BF16)) * ind_ref[hd]
        ks = (k.astype(F32) * sd_ref[hd]).astype(BF16)
        st_ref[hd] = s * cd_ref[hd] + _dot_tn(ks, v)
        oc = o - jnp.mean(o, axis=-1, keepdims=True)
        on = oc * lax.rsqrt(jnp.mean(oc * oc, axis=-1, keepdims=True) + EPS)
        gate = g_ref[0, :, hd * dv:(hd + 1) * dv].astype(F32)
        o_ref[0, :, hd * dv:(hd + 1) * dv] = (gate * (on * w_ref[:, hd * dv:(hd + 1) * dv])).astype(o_ref.dtype)


def _ret_tables(heads, c, dk, dv):
    log_gamma = jnp.log1p(-jnp.exp2(RET_DECAY_LOG2_BASE - jnp.arange(heads, dtype=F32)))
    idx = jnp.arange(c, dtype=F32)
    diff = idx[:, None] - idx[None, :]
    causal = diff >= 0
    dmask = jnp.where(causal[None], jnp.exp(log_gamma[:, None, None] * jnp.where(causal, diff, 0.0)[None]), 0.0)
    inner = jnp.exp(log_gamma[:, None] * (idx + 1.0))
    sdecay = jnp.exp(log_gamma[:, None] * (c - 1.0 - idx))
    cdecay = jnp.exp(log_gamma * c)
    return (dmask,
            jnp.broadcast_to(inner[:, :, None], (heads, c, dv)),
            jnp.broadcast_to(sdecay[:, :, None], (heads, c, dk)),
            jnp.broadcast_to(cdecay[:, None, None], (heads, 1, dv)))


def _gla_heads(q_ref, k_ref, la_ref, v_ref, g_ref, w_ref, o_ref, st_ref, *, heads, dk, dv, c):
    row = lax.broadcasted_iota(I32, (c, dk), 0)
    differ = lax.broadcasted_iota(I32, (c, c), 0) ^ lax.broadcasted_iota(I32, (c, c), 1)
    owner = jnp.full((c, c), -1, I32)
    uppers, signs = [], []
    half = 1
    while half < c:
        owner = owner + (differ >= half).astype(I32)
        up = (row & (2 * half - 1)) >= half
        uppers.append(up)
        signs.append(jnp.where(up, LOG2_E, -LOG2_E))
        half *= 2
    for hd in range(heads):
        q = q_ref[0, :, hd * dk:(hd + 1) * dk]
        k = k_ref[0, :, hd * dk:(hd + 1) * dk]
        v = v_ref[0, :, hd * dv:(hd + 1) * dv]
        b = la_ref[0, :, hd * dk:(hd + 1) * dk]
        sh = 1
        while sh < c:
            b = b + jnp.where(row >= sh, pltpu.roll(b, sh, 0), 0.0)
            sh *= 2
        scores = jnp.where(owner < 0, _dot_nt(q.astype(BF16), k.astype(BF16)), 0.0)
        first = b
        for lvl, (upper, sign) in enumerate(zip(uppers, signs)):
            half = 1 << lvl
            mid = jnp.where(upper, first, pltpu.roll(first, c - half, 0))
            scaled = jnp.where(upper, q, k) * jnp.exp2(jnp.minimum((b - mid) * sign, 0.0))
            ql = jnp.where(upper, scaled, 0.0).astype(BF16)
            kl = jnp.where(upper, 0.0, scaled).astype(BF16)
            scores = jnp.where(owner == lvl, _dot_nt(ql, kl), scores)
            first = jnp.where(upper, pltpu.roll(first, half, 0), first)
        st = st_ref[hd]
        o = _dot(scores.astype(BF16), v) + _dot_nt((q * jnp.exp(b)).astype(BF16), st.astype(BF16))
        b_last = b[c - 1:c, :]
        kd = (k * jnp.exp(b_last - b)).astype(BF16)
        st_ref[hd] = st * jnp.exp(b_last) + _dot_tn(v, kd)
        on = o * lax.rsqrt(jnp.mean(o * o, axis=-1, keepdims=True) + EPS)
        gate = g_ref[0, :, hd * dv:(hd + 1) * dv].astype(F32)
        o_ref[0, :, hd * dv:(hd + 1) * dv] = (gate * (on * w_ref[:, hd * dv:(hd + 1) * dv])).astype(o_ref.dtype)


def _mixers_kernel(rq_ref, rk_ref, rv_ref, rg_ref, rs0_ref, dm_ref, ind_ref, sd_ref, cd_ref, rw_ref,
                   gq_ref, gk_ref, la_ref, gv_ref, gg_ref, gs0_ref, gw_ref,
                   ro_ref, rso_ref, go_ref, gso_ref, rst_ref, gst_ref, *, ret, gla, c):
    ci = pl.program_id(1)

    @pl.when(ci == 0)
    def _():
        rst_ref[...] = rs0_ref[0]
        gst_ref[...] = gs0_ref[0]

    rh, rdk, rdv = ret
    gh, gdk, gdv = gla
    _ret_heads(rq_ref, rk_ref, rv_ref, rg_ref, dm_ref, ind_ref, sd_ref, cd_ref, rw_ref, ro_ref, rst_ref,
               heads=rh, dk=rdk, dv=rdv)
    _gla_heads(gq_ref, gk_ref, la_ref, gv_ref, gg_ref, gw_ref, go_ref, gst_ref, heads=gh, dk=gdk, dv=gdv, c=c)

    @pl.when(ci == pl.num_programs(1) - 1)
    def _():
        rso_ref[0] = rst_ref[...]
        gso_ref[0] = gst_ref[...]


def _mixers(qkvg, qk, vg, la, s0_ret, s0_gla_t, ret_norm_w, gla_norm_w, *, ret, gla):
    s, t, _ = qkvg.shape
    rh, rdk, rdv = ret
    gh, gdk, gdv = gla
    assert rdk == rdv
    c = _tile(t, CHUNK_MIXERS, 2 * SUBLANES)
    assert c & (c - 1) == 0, "chunk length must be a power of two"
    rw = rh * rdk
    gwk, gwv = gh * gdk, gh * gdv
    dm, ind, sd, cd = _ret_tables(rh, c, rdk, rdv)
    const3 = lambda b, i: (0, 0, 0)
    const2 = lambda b, i: (0, 0)
    chunk = lambda col: (lambda b, i: (b, i, col))
    state = lambda b, i: (b, 0, 0, 0)
    return pl.pallas_call(
        functools.partial(_mixers_kernel, ret=ret, gla=gla, c=c),
        out_shape=(
            jax.ShapeDtypeStruct((s, t, rw), BF16), jax.ShapeDtypeStruct(s0_ret.shape, F32),
            jax.ShapeDtypeStruct((s, t, gwv), BF16), jax.ShapeDtypeStruct(s0_gla_t.shape, F32),
        ),
        grid=(s, t // c),
        in_specs=[
            pl.BlockSpec((1, c, rw), chunk(0)),
            pl.BlockSpec((1, c, rw), chunk(1)),
            pl.BlockSpec((1, c, rw), chunk(2)),
            pl.BlockSpec((1, c, rw), chunk(3)),
            pl.BlockSpec((1, rh, rdk, rdv), state),
            pl.BlockSpec((rh, c, c), const3),
            pl.BlockSpec((rh, c, rdv), const3),
            pl.BlockSpec((rh, c, rdk), const3),
            pl.BlockSpec((rh, 1, rdv), const3),
            pl.BlockSpec((1, rw), const2),
            pl.BlockSpec((1, c, gwk), chunk(0)),
            pl.BlockSpec((1, c, gwk), chunk(1)),
            pl.BlockSpec((1, c, gwk), chunk(0)),
            pl.BlockSpec((1, c, gwv), chunk(0)),
            pl.BlockSpec((1, c, gwv), chunk(1)),
            pl.BlockSpec((1, gh, gdv, gdk), state),
            pl.BlockSpec((1, gwv), const2),
        ],
        out_specs=(
            pl.BlockSpec((1, c, rw), chunk(0)),
            pl.BlockSpec((1, rh, rdk, rdv), state),
            pl.BlockSpec((1, c, gwv), chunk(0)),
            pl.BlockSpec((1, gh, gdv, gdk), state),
        ),
        scratch_shapes=[pltpu.VMEM((rh, rdk, rdv), F32), pltpu.VMEM((gh, gdv, gdk), F32)],
        compiler_params=_params("arbitrary", "arbitrary"),
        name="token_mixers",
    )(qkvg, qkvg, qkvg, qkvg, s0_ret, dm, ind, sd, cd, ret_norm_w.reshape(1, rw),
      qk, qk, la, vg, vg, s0_gla_t, gla_norm_w.reshape(1, gwv))


def _merge_kernel(br_ref, bg_ref, h_ref, wr_ref, wg_ref, wm0_ref, wm1_ref, bm0_ref, bm1_ref, o_ref):
    h = h_ref[...]
    g0 = _sigmoid(_dot(h, wm0_ref[...]) + bm0_ref[...])
    g1 = _sigmoid(_dot(h, wm1_ref[...]) + bm1_ref[...])
    y = g0 * _dot(br_ref[...], wr_ref[...]) + g1 * _dot(bg_ref[...], wg_ref[...])
    o_ref[...] = y.astype(o_ref.dtype)


def _merge(o_ret, o_gla, h2d, w_branch, w_merge, layer, b_merge, *, tm):
    n, d = h2d.shape
    wdt = o_ret.shape[1]
    tn = _tile(d, 512, LANES)
    nj = d // tn
    b2 = b_merge.reshape(1, 2 * d)
    return pl.pallas_call(
        _merge_kernel,
        out_shape=jax.ShapeDtypeStruct((n, d), BF16),
        grid=(n // tm, nj),
        in_specs=[
            pl.BlockSpec((tm, wdt), lambda i, j: (i, 0)),
            pl.BlockSpec((tm, wdt), lambda i, j: (i, 0)),
            pl.BlockSpec((tm, d), lambda i, j: (i, 0)),
            pl.BlockSpec((None, wdt, tn), lambda i, j: (2 * layer, 0, j)),
            pl.BlockSpec((None, wdt, tn), lambda i, j: (2 * layer + 1, 0, j)),
            pl.BlockSpec((None, d, tn), lambda i, j: (layer, 0, j)),
            pl.BlockSpec((None, d, tn), lambda i, j: (layer, 0, j + nj)),
            pl.BlockSpec((1, tn), lambda i, j: (0, j)),
            pl.BlockSpec((1, tn), lambda i, j: (0, j + nj)),
        ],
        out_specs=pl.BlockSpec((tm, tn), lambda i, j: (i, j)),
        compiler_params=_params("arbitrary", "arbitrary"),
        name="branch_merge",
    )(o_ret, o_gla, h2d, w_branch, w_branch, w_merge, w_merge, b2, b2)


def _pack_bf16_pairs(h):
    half = h.shape[-1] // 2
    a = lax.bitcast_convert_type(h[:, :half].astype(BF16).astype(F32), U32)
    b = lax.bitcast_convert_type(h[:, half:].astype(BF16).astype(F32), U32)
    return a | (b >> 16)


def _unpack_bf16_pairs(w, dtype):
    a = lax.bitcast_convert_type(w & jnp.uint32(0xFFFF0000), F32)
    b = lax.bitcast_convert_type(w << 16, F32)
    return jnp.concatenate([a, b], axis=-1).astype(dtype)


def _route(logits, n_groups, per_group):
    lane = lax.broadcasted_iota(I32, logits.shape, 1).astype(F32)
    neg = jnp.float32(-jnp.inf)

    def first_max(mask):
        m = jnp.max(jnp.where(mask, logits, neg), axis=-1, keepdims=True)
        idx = jnp.min(jnp.where(mask & (logits == m), lane, float(LANES)), axis=-1, keepdims=True)
        return m, idx

    gmask = lane < n_groups
    gmax, gidx = first_max(gmask)
    p_group = 1.0 / jnp.sum(jnp.where(gmask, jnp.exp(logits - gmax), 0.0), axis=-1, keepdims=True)
    lo = n_groups + gidx * per_group
    emask = (lane >= lo) & (lane < lo + per_group)
    m1, i1 = first_max(emask)
    m2, i2 = first_max(emask & (lane != i1))
    w1 = 1.0 / (1.0 + jnp.exp(m2 - m1))
    w2 = 1.0 - w1
    return jnp.where(lane == 0.0, i1 - n_groups, jnp.where(lane == 1.0, i2 - n_groups, jnp.where(
        lane == 2.0, p_group * w1, jnp.where(lane == 3.0, p_group * w2, 0.0))))


def _with_ranks(rt, carry_ref):
    tr = rt.shape[0]
    lane = lax.broadcasted_iota(I32, rt.shape, 1).astype(F32)
    a1 = lane == rt[:, 0:1]
    a2 = lane == rt[:, 1:2]
    hit = jnp.where(a1 | a2, 1.0, 0.0)
    ri = lax.broadcasted_iota(I32, (tr, tr), 0)
    cj = lax.broadcasted_iota(I32, (tr, tr), 1)
    before = _dot(jnp.where(ri > cj, 1.0, 0.0).astype(BF16), hit.astype(BF16)) + carry_ref[...]
    k1 = jnp.sum(jnp.where(a1, before, 0.0), axis=-1, keepdims=True)
    k2 = jnp.sum(jnp.where(a2, before, 0.0), axis=-1, keepdims=True)
    carry_ref[...] += jnp.sum(hit, axis=0, keepdims=True)
    return jnp.where(lane == 4.0, k1, jnp.where(lane == 5.0, k2, rt))


def _out_proj_kernel(m_ref, x_ref, g1_ref, sc_ref, sh_ref, nw_ref, wo_ref, wr_ref, br_ref,
                     x1_ref, hp_ref, rt_ref, cnt_ref, carry_ref, *, n_groups, per_group):
    bs, tt, d = x_ref.shape

    @pl.when((pl.program_id(0) == 0) & (pl.program_id(1) == 0))
    def _():
        carry_ref[...] = jnp.zeros_like(carry_ref)

    y = _dot(m_ref[...].reshape(bs * tt, d), wo_ref[...]).reshape(bs, tt, d)
    x1 = x_ref[...] + g1_ref[...] * y
    x1_ref[...] = x1
    hn = x1 * lax.rsqrt(jnp.mean(x1 * x1, axis=-1, keepdims=True) + EPS) * nw_ref[...]
    h = (hn * (1.0 + sc_ref[...]) + sh_ref[...]).reshape(bs * tt, d)
    _store_token_major(hp_ref, _pack_bf16_pairs(h))
    h_hi, h_lo = _split_bf16(h)
    p = _dot(h_hi, wr_ref[...])
    logits = p[:, :LANES] + p[:, LANES:] + _dot(h_lo, wr_ref[:, :LANES]) + br_ref[...]
    rt = _with_ranks(_route(logits, n_groups, per_group), carry_ref)
    rt_ref[...] = rt.reshape(bs, tt, LANES)
    cnt_ref[...] = carry_ref[...]


def _out_proj(merged, x, g1, sc2, sh2, norm_w, w_o, layer, wr, b_r, *, n_groups, per_group):
    s, t, d = x.shape
    bs, tt = _seq_tiles(s, t, ROWS_OUT_PROJ)
    chunks = d // 2 // LANES
    tok = lambda i, j: (i, j, 0)
    seq = lambda i, j: (i, 0, 0)
    const = lambda i, j: (0, 0)
    return pl.pallas_call(
        functools.partial(_out_proj_kernel, n_groups=n_groups, per_group=per_group),
        out_shape=(
            jax.ShapeDtypeStruct((s, t, d), F32),
            jax.ShapeDtypeStruct((s * t * chunks, LANES), U32),
            jax.ShapeDtypeStruct((s, t, LANES), F32),
            jax.ShapeDtypeStruct((1, LANES), F32),
        ),
        grid=(s // bs, t // tt),
        in_specs=[
            pl.BlockSpec((bs, tt, d), tok),
            pl.BlockSpec((bs, tt, d), tok),
            pl.BlockSpec((bs, 1, d), seq),
            pl.BlockSpec((bs, 1, d), seq),
            pl.BlockSpec((bs, 1, d), seq),
            pl.BlockSpec((1, d), const),
            pl.BlockSpec((None, d, d), lambda i, j: (layer, 0, 0), pipeline_mode=pl.Buffered(1)),
            pl.BlockSpec((d, 2 * LANES), const, pipeline_mode=pl.Buffered(1)),
            pl.BlockSpec((1, LANES), const),
        ],
        out_specs=(
            pl.BlockSpec((bs, tt, d), tok),
            pl.BlockSpec((bs * tt * chunks, LANES), lambda i, j: (i * (t // tt) + j, 0)),
            pl.BlockSpec((bs, tt, LANES), tok),
            pl.BlockSpec((1, LANES), const),
        ),
        scratch_shapes=[pltpu.VMEM((1, LANES), F32)],
        compiler_params=_params("arbitrary", "arbitrary"),
        name="out_proj_router",
    )(merged, x, g1, sc2, sh2, norm_w.reshape(1, d), w_o, wr, b_r)


def _invert_kernel(pos_ref, pad_lo_ref, pad_hi_ref, src_ref, *, n_tokens, n_exp):
    def clear(j, carry):
        src_ref[j] = 0
        return carry

    for e in range(n_exp):
        lax.fori_loop(pad_lo_ref[e], pad_hi_ref[e], clear, 0)

    def put(t, carry):
        for kk in range(TOP_K):
            src_ref[pos_ref[TOP_K * t + kk]] = t
        return carry

    lax.fori_loop(0, n_tokens, put, 0, unroll=INVERT_UNROLL)


def _invert(pos, pad_lo, pad_hi, n_rows):
    n_tokens = pos.shape[0] // TOP_K
    return pl.pallas_call(
        functools.partial(_invert_kernel, n_tokens=n_tokens, n_exp=pad_lo.shape[0]),
        out_shape=jax.ShapeDtypeStruct((n_rows,), I32),
        grid_spec=pltpu.PrefetchScalarGridSpec(
            num_scalar_prefetch=3,
            grid=(1,),
            in_specs=[],
            out_specs=pl.BlockSpec(memory_space=pltpu.SMEM),
        ),
        compiler_params=_params("arbitrary"),
        name="moe_invert",
    )(pos, pad_lo, pad_hi)


def _store_token_major(ref, value):
    chunks = value.shape[1] // LANES
    for c in range(chunks):
        ref[pl.ds(c, value.shape[0], stride=chunks), :] = value[:, c * LANES:(c + 1) * LANES]


def _load_token_major(ref, rows, chunks):
    return jnp.concatenate([ref[pl.ds(c, rows, stride=chunks), :] for c in range(chunks)], axis=-1)


def _token_copy(src_ref, src_tok, dst_ref, dst_tok, chunks, sem):
    def at(ref, tok):
        start = tok * chunks
        if not isinstance(start, int):
            start = pl.multiple_of(start, chunks)
        return ref.at[pl.ds(start, chunks)]

    return pltpu.make_async_copy(at(src_ref, src_tok), at(dst_ref, dst_tok), sem)


def _ffn_kernel(te_ref, valid_ref, src_ref, hp_ref, wg_ref, wu_ref, wd_ref, ys_ref, xbuf, sems, *, tf, chunks):
    del te_ref
    i = pl.program_id(0)
    last = pl.num_programs(0) - 1
    slot = i % 2
    nxt = jnp.where(i < last, i + 1, 0)

    def start(tile, slt, r):
        _token_copy(hp_ref, src_ref[tile * tf + r], xbuf.at[slt], r, chunks, sems.at[slt]).start()

    def drain(slt):
        def wait(r, carry):
            _token_copy(hp_ref, 0, xbuf.at[slt], 0, chunks, sems.at[slt]).wait()
            return carry

        lax.fori_loop(0, tf, wait, 0, unroll=GATHER_UNROLL)

    @pl.when(i == 0)
    def _():
        def issue(r, carry):
            start(0, 0, r)
            return carry

        lax.fori_loop(0, tf, issue, 0, unroll=GATHER_UNROLL)

    for r in range(tf):
        start(nxt, 1 - slot, r)

    drain(slot)

    @pl.when(valid_ref[i] != 0)
    def _():
        x = _unpack_bf16_pairs(_load_token_major(xbuf.at[slot], tf, chunks), BF16)
        act = (_silu(_dot(x, wg_ref[...])) * _dot(x, wu_ref[...])).astype(BF16)
        _store_token_major(ys_ref, _pack_bf16_pairs(_dot(act, wd_ref[...])))

    @pl.when(valid_ref[i] == 0)
    def _():
        ys_ref[...] = jnp.zeros_like(ys_ref)

    @pl.when(i == last)
    def _():
        drain(1 - slot)


def _ffn(tile_expert, tile_valid, src, hp, wg, wu, wd, *, tf):
    n_rows = src.shape[0]
    _, d, f = wg.shape
    chunks = d // 2 // LANES
    return pl.pallas_call(
        functools.partial(_ffn_kernel, tf=tf, chunks=chunks),
        out_shape=jax.ShapeDtypeStruct((n_rows * chunks, LANES), U32),
        grid_spec=pltpu.PrefetchScalarGridSpec(
            num_scalar_prefetch=3,
            grid=(n_rows // tf,),
            in_specs=[
                pl.BlockSpec(memory_space=pl.ANY),
                pl.BlockSpec((None, d, f), lambda i, te, tv, sr: (te[i], 0, 0)),
                pl.BlockSpec((None, d, f), lambda i, te, tv, sr: (te[i], 0, 0)),
                pl.BlockSpec((None, f, d), lambda i, te, tv, sr: (te[i], 0, 0)),
            ],
            out_specs=pl.BlockSpec((tf * chunks, LANES), lambda i, te, tv, sr: (i, 0)),
            scratch_shapes=[pltpu.VMEM((2, tf * chunks, LANES), hp.dtype), pltpu.SemaphoreType.DMA((2,))],
        ),
        compiler_params=_params("arbitrary"),
        name="moe_ffn",
    )(tile_expert, tile_valid, src, hp, wg, wu, wd)


def _combine_kernel(pos_ref, x_ref, rt_ref, g2_ref, nw_ref, sc_ref, sh_ref, ys_ref, *rest, last):
    if last:
        o_ref, ybuf, sems = rest
    else:
        o_ref, h_ref, ybuf, sems = rest
    bs, tt, d = x_ref.shape
    rows = bs * tt
    chunks = d // 2 // LANES
    step = pl.program_id(0) * pl.num_programs(1) + pl.program_id(1)
    n_steps = pl.num_programs(0) * pl.num_programs(1)

    def start(stp, slt, r, kk):
        _token_copy(ys_ref, pos_ref[TOP_K * (stp * rows + r) + kk], ybuf.at[slt, kk], r, chunks,
                    sems.at[slt]).start()

    @pl.when(step == 0)
    def _():
        def issue(r, carry):
            for kk in range(TOP_K):
                start(0, 0, r, kk)
            return carry

        lax.fori_loop(0, rows, issue, 0, unroll=GATHER_UNROLL)

    slot = step % 2

    @pl.when(step + 1 < n_steps)
    def _():
        for r in range(rows):
            for kk in range(TOP_K):
                start(step + 1, 1 - slot, r, kk)

    def wait(r, carry):
        for kk in range(TOP_K):
            _token_copy(ys_ref, 0, ybuf.at[slot, kk], 0, chunks, sems.at[slot]).wait()
        return carry

    lax.fori_loop(0, rows, wait, 0, unroll=GATHER_UNROLL)

    rt = rt_ref[...].reshape(rows, LANES)
    y = (rt[:, 2:3] * _unpack_bf16_pairs(_load_token_major(ybuf.at[slot, 0], rows, chunks), F32)
         + rt[:, 3:4] * _unpack_bf16_pairs(_load_token_major(ybuf.at[slot, 1], rows, chunks), F32))
    xn = x_ref[...] + g2_ref[...] * y.reshape(bs, tt, d)
    normed = xn * lax.rsqrt(jnp.mean(xn * xn, axis=-1, keepdims=True) + EPS) * nw_ref[...]
    if last:
        o_ref[...] = normed
    else:
        o_ref[...] = xn
        h_ref[...] = (normed * (1.0 + sc_ref[...]) + sh_ref[...]).astype(h_ref.dtype)


def _combine(pos, x1, route, g2, norm_w, sc, sh, ys, *, last):
    s, t, d = x1.shape
    bs, tt = _seq_tiles(s, t, ROWS_COMBINE)
    tok = lambda i, j, p: (i, j, 0)
    seq = lambda i, j, p: (i, 0, 0)
    x_shape = jax.ShapeDtypeStruct((s, t, d), F32)
    x_spec = pl.BlockSpec((bs, tt, d), tok)
    return pl.pallas_call(
        functools.partial(_combine_kernel, last=last),
        out_shape=x_shape if last else (x_shape, jax.ShapeDtypeStruct((s, t, d), BF16)),
        grid_spec=pltpu.PrefetchScalarGridSpec(
            num_scalar_prefetch=1,
            grid=(s // bs, t // tt),
            in_specs=[
                x_spec,
                pl.BlockSpec((bs, tt, LANES), tok),
                pl.BlockSpec((bs, 1, d), seq),
                pl.BlockSpec((1, d), lambda i, j, p: (0, 0)),
                pl.BlockSpec((bs, 1, d), seq),
                pl.BlockSpec((bs, 1, d), seq),
                pl.BlockSpec(memory_space=pl.ANY),
            ],
            out_specs=x_spec if last else (x_spec, x_spec),
            scratch_shapes=[pltpu.VMEM((2, TOP_K, bs * tt * (d // 2 // LANES), LANES), U32),
                            pltpu.SemaphoreType.DMA((2,))],
        ),
        compiler_params=_params("arbitrary", "arbitrary"),
        name="moe_combine",
    )(pos, x1, route, g2, norm_w.reshape(1, d), sc, sh, ys)


def _moe(x1, hp, route, counts, g2, wg, wu, wd, layer, n_exp, norm_w, sc, sh, *, last):
    s, t, d = x1.shape
    n = s * t
    route2d = route.reshape(n, LANES)
    tf = 2 * SUBLANES
    while tf < ROWS_FFN and tf < 2 * TOP_K * n // n_exp:
        tf *= 2
    n_tiles = (TOP_K * n) // tf + n_exp
    cnt = counts[0, :n_exp].astype(I32)
    padded = ((cnt + tf - 1) // tf) * tf
    ends = jnp.cumsum(padded)
    starts = ends - padded
    expert = route2d[:, :TOP_K].astype(I32)
    rank = route2d[:, 4:4 + TOP_K].astype(I32)
    start_of = jnp.sum(jnp.where(expert[..., None] == jnp.arange(n_exp, dtype=I32), starts, 0), axis=-1)
    pos = (start_of + rank).reshape(TOP_K * n)
    tile_start = jnp.arange(n_tiles, dtype=I32) * tf
    tile_valid = (tile_start < ends[-1]).astype(I32)
    tile_expert = jnp.minimum(jnp.sum((ends[None, :] <= tile_start[:, None]).astype(I32), axis=1), n_exp - 1)
    last_used = jnp.max(jnp.where(tile_valid != 0, tile_expert, 0))
    tile_expert = jnp.where(tile_valid != 0, tile_expert, last_used)
    pad_hi = jnp.where(jnp.arange(n_exp) == n_exp - 1, n_tiles * tf, ends).astype(I32)
    src = _invert(pos, starts + cnt, pad_hi, n_tiles * tf)
    ys = _ffn(tile_expert + layer * n_exp, tile_valid, src, hp, wg, wu, wd, tf=tf)
    return _combine(pos, x1, route, g2, norm_w, sc, sh, ys, last=last)


def _rotary_tables(pos0, t, dk, reps):
    half = dk // 2
    inv = ROPE_BASE ** (-jnp.arange(half, dtype=F32) / half)
    ang = (pos0 + jnp.arange(t)).astype(F32)[:, None] * inv[None, :]
    cos = jnp.cos(ang)
    sin = jnp.sin(ang)
    cos2 = jnp.concatenate([cos, cos], axis=-1)
    sin2 = jnp.concatenate([-sin, sin], axis=-1)
    return jnp.tile(cos2, (reps, 1)), jnp.tile(sin2, (reps, 1))


def _trunk(x, mods, pos0, s_ret_in, s_gla_in_t, big, wts, dims):
    s, t, d = x.shape
    n = s * t
    depth = len(wts)
    rh, rdk, rdv, gh, gdk, gdv = dims["ret_heads"], dims["ret_dk"], dims["ret_dv"], dims["gla_heads"], dims["gla_dk"], dims["gla_dv"]
    bs, tt = _seq_tiles(s, t, ROWS_PROJ)
    tp = bs * tt
    tm = _tile(n, ROWS_MATMUL, 2 * SUBLANES)
    cos, sin = _rotary_tables(pos0, t, rdk, bs)
    ret_states, gla_states = [], []
    h = _norm(x, wts[0]["norm_mix"], mods[1][0], mods[0][0])
    for l in range(depth):
        w = wts[l]
        sh1, sc1, g1, sh2, sc2, g2 = (m[l] for m in mods)
        h2d = h.reshape(n, d)
        qkvg = _proj_ret(h2d, big["w_ret"], l, cos, sin, heads=rh, dk=rdk, tm=tp)
        qk, vg, la = _proj_gla(h2d, big["w_gla"], big["w_code"], l, w["wup_hi"], w["wup_lo"], w["b_gla"],
                               hdk=gh * gdk, hdv=gh * gdv, dk=gdk, tm=tp)
        o_ret, s_r, o_gla, s_g = _mixers(qkvg.reshape(s, t, -1), qk.reshape(s, t, -1), vg.reshape(s, t, -1),
                                         la.reshape(s, t, -1), s_ret_in[l], s_gla_in_t[l], w["ret_norm"],
                                         w["gla_norm"], ret=(rh, rdk, rdv), gla=(gh, gdk, gdv))
        merged = _merge(o_ret.reshape(n, -1), o_gla.reshape(n, -1), h2d, big["w_branch"], big["w_merge"], l,
                        w["b_merge"], tm=tm)
        x1, hp, route, counts = _out_proj(merged.reshape(s, t, d), x, g1, sc2, sh2, w["norm_ffn"], big["w_o"], l,
                                  w["w_router"], w["b_r"], n_groups=dims["n_groups"], per_group=dims["per_group"])
        n_exp = dims["n_groups"] * dims["per_group"]
        experts = (big["w_exp_gate"], big["w_exp_up"], big["w_exp_down"], l, n_exp)
        if l == depth - 1:
            x = _moe(x1, hp, route, counts, g2, *experts, w["final_norm"], sc1, sh1, last=True)
        else:
            x, h = _moe(x1, hp, route, counts, g2, *experts, wts[l + 1]["norm_mix"], mods[1][l + 1], mods[0][l + 1], last=False)
        ret_states.append(s_r)
        gla_states.append(jnp.swapaxes(s_g, -1, -2))
    return x, jnp.stack(ret_states), jnp.stack(gla_states)


def kernel(x_prompt, x_sample, state_ret, state_gla, c_prompt, c_sample, w_ada, b_ada, norm_mix_w, norm_ffn_w, w_in, w_gla_up, b_gla, ret_norm_w, gla_norm_w, w_branch, w_merge, b_merge, w_o, w_router_group, b_router_group, w_router_expert, b_router_expert, w_exp_gate, w_exp_up, w_exp_down, final_norm_w):
    depth, d, _ = w_in.shape
    _, _, rh, rdk, rdv = state_ret.shape
    _, _, gh, gdk, gdv = state_gla.shape
    low_rank = w_gla_up.shape[1]
    n_groups = w_router_group.shape[-1]
    n_exp = w_router_expert.shape[-1]
    assert low_rank <= LANES and n_groups + n_exp <= LANES and n_exp % n_groups == 0
    dims = dict(ret_heads=rh, ret_dk=rdk, ret_dv=rdv, gla_heads=gh, gla_dk=gdk, gla_dv=gdv,
                n_groups=n_groups, per_group=n_exp // n_groups)

    n_ret = 2 * rh * rdk + 2 * rh * rdv
    n_gla = 2 * gh * gdk + 2 * gh * gdv
    assert w_in.shape[-1] == n_ret + n_gla + low_rank
    f = w_exp_gate.shape[-1]
    big = dict(
        w_ret=_to_bf16(w_in, 0, n_ret),
        w_gla=_to_bf16(w_in, n_ret, n_gla),
        w_code=_to_bf16(w_in, n_ret + n_gla, LANES, valid_cols=low_rank),
        w_branch=_to_bf16(w_branch.reshape(depth, -1, d)).reshape(depth * w_branch.shape[1], -1, d),
        w_merge=_to_bf16(w_merge),
        w_o=_to_bf16(w_o),
        w_exp_gate=_to_bf16(w_exp_gate.reshape(depth, n_exp * d, f)).reshape(depth * n_exp, d, f),
        w_exp_up=_to_bf16(w_exp_up.reshape(depth, n_exp * d, f)).reshape(depth * n_exp, d, f),
        w_exp_down=_to_bf16(w_exp_down.reshape(depth, n_exp * f, d)).reshape(depth * n_exp, f, d),
    )
    w_router = jnp.concatenate([w_router_group, w_router_expert], axis=-1)
    w_router = jnp.pad(w_router, ((0, 0), (0, 0), (0, LANES - n_groups - n_exp)))
    b_router = jnp.pad(jnp.concatenate([b_router_group, b_router_expert], axis=-1), ((0, 0), (0, LANES - n_groups - n_exp)))
    wup = jnp.pad(w_gla_up, ((0, 0), (0, LANES - low_rank), (0, 0)))
    wts = []
    for l in range(depth):
        wr_hi, wr_lo = _split_bf16(w_router[l])
        wup_hi, wup_lo = _split_bf16(wup[l])
        wts.append(dict(
            norm_mix=norm_mix_w[l], norm_ffn=norm_ffn_w[l],
            wup_hi=wup_hi, wup_lo=wup_lo, b_gla=b_gla[l].reshape(1, -1),
            ret_norm=ret_norm_w[l], gla_norm=gla_norm_w[l], b_merge=b_merge[l],
            w_router=jnp.concatenate([wr_hi, wr_lo], axis=-1), b_r=b_router[l].reshape(1, LANES),
            final_norm=final_norm_w,
        ))

    nb, ns = c_prompt.shape[0], c_sample.shape[0]
    mod = _ada(jnp.concatenate([c_prompt, c_sample], axis=0), w_ada, b_ada)
    mod = mod.reshape(depth, nb + ns, 6, d)
    mods_p = [mod[:, :nb, i][:, :, None, :] for i in range(6)]
    mods_s = [mod[:, nb:, i][:, :, None, :] for i in range(6)]

    zero_ret = jnp.zeros((depth, nb, rh, rdk, rdv), F32)
    zero_gla_t = jnp.zeros((depth, nb, gh, gdv, gdk), F32)
    y_p, ret_p, gla_p = _trunk(x_prompt, mods_p, 0, zero_ret, zero_gla_t, big, wts, dims)
    y_s, ret_s, gla_s = _trunk(x_sample, mods_s, PAST_LEN, state_ret.astype(F32),
                               jnp.swapaxes(state_gla.astype(F32), -1, -2), big, wts, dims)
    return (y_p, y_s, ret_p, gla_p, ret_s, gla_s)
```

```python
import functools

import jax
import jax.numpy as jnp
from jax import lax
from jax.experimental import pallas as pl
from jax.experimental.pallas import tpu as pltpu

F32 = jnp.float32
BF16 = jnp.bfloat16
U32 = jnp.uint32
I32 = jnp.int32

PAST_LEN = 4096
GLA_TAU = 16.0
ROPE_BASE = 10000.0
EPS = 1e-6
RET_DECAY_LOG2_BASE = -5.0
TOP_K = 2
LOG2_E = 1.4426950408889634

LANES = 128
SUBLANES = 8
V7X_VMEM_BYTES = 64 * 1024 * 1024
VMEM_LIMIT = V7X_VMEM_BYTES - 8 * 1024 * 1024

ROWS_MATMUL = 1024
ROWS_PROJ = 512
ROWS_OUT_PROJ = 512
ROWS_NORM = 512
ROWS_COMBINE = 256
GATHER_UNROLL = 8
INVERT_UNROLL = 16
CAST_BLOCK_COLS = 1024
CAST_BLOCK_ELEMS = 2 * 1024 * 1024
ROWS_FFN = 512
CHUNK_MIXERS = 256


def _params(*sem):
    return pltpu.CompilerParams(dimension_semantics=sem, vmem_limit_bytes=VMEM_LIMIT)


def _tile(n, pref, align):
    if n <= pref:
        return n
    t = (pref // align) * align
    while t >= align:
        if n % t == 0:
            return t
        t -= align
    return n


def _seq_tiles(s, t, rows):
    if t >= rows:
        return 1, _tile(t, rows, 2 * SUBLANES)
    return _tile(s, max(rows // t, 1), 1), t


def _dot(a, b):
    return jnp.dot(a, b, preferred_element_type=F32)


def _dot_nt(a, b):
    return lax.dot_general(a, b, (((1,), (1,)), ((), ())), preferred_element_type=F32)


def _dot_tn(a, b):
    return lax.dot_general(a, b, (((0,), (0,)), ((), ())), preferred_element_type=F32)


def _sigmoid(x):
    return 0.5 * jnp.tanh(0.5 * x) + 0.5


def _silu(x):
    return x * _sigmoid(x)


def _log_sigmoid(z):
    return jnp.minimum(z, 0.0) - jnp.log(1.0 + jnp.exp(-jnp.abs(z)))


def _split_bf16(x):
    hi = x.astype(BF16)
    lo = (x - hi.astype(F32)).astype(BF16)
    return hi, lo


def _dot_split(a, w_hi, w_lo):
    a_hi, a_lo = _split_bf16(a)
    return _dot(a_hi, w_hi) + _dot(a_lo, w_hi) + _dot(a_hi, w_lo)


def _ada_kernel(c_ref, w_ref, b_ref, o_ref):
    a = _silu(c_ref[...]).astype(BF16)
    o_ref[...] = _dot(a, w_ref[...].astype(BF16)) + b_ref[...]


def _ada(c_all, w_ada, b_ada):
    depth, d, n6 = w_ada.shape
    r = c_all.shape[0]
    tn = _tile(n6, 1024, LANES)
    return pl.pallas_call(
        _ada_kernel,
        out_shape=jax.ShapeDtypeStruct((depth, r, n6), F32),
        grid=(depth, n6 // tn),
        in_specs=[
            pl.BlockSpec((r, d), lambda l, j: (0, 0)),
            pl.BlockSpec((None, d, tn), lambda l, j: (l, 0, j)),
            pl.BlockSpec((None, 1, tn), lambda l, j: (l, 0, j)),
        ],
        out_specs=pl.BlockSpec((None, r, tn), lambda l, j: (l, 0, j)),
        compiler_params=_params("arbitrary", "arbitrary"),
        name="ada_mod",
    )(c_all, w_ada, b_ada.reshape(depth, 1, n6))


def _norm_kernel(x_ref, w_ref, sc_ref, sh_ref, o_ref):
    x = x_ref[...]
    y = x * lax.rsqrt(jnp.mean(x * x, axis=-1, keepdims=True) + EPS) * w_ref[...]
    o_ref[...] = (y * (1.0 + sc_ref[...]) + sh_ref[...]).astype(o_ref.dtype)


def _norm(x, w, sc, sh):
    s, t, d = x.shape
    bs, tt = _seq_tiles(s, t, ROWS_NORM)
    return pl.pallas_call(
        _norm_kernel,
        out_shape=jax.ShapeDtypeStruct((s, t, d), BF16),
        grid=(s // bs, t // tt),
        in_specs=[
            pl.BlockSpec((bs, tt, d), lambda i, j: (i, j, 0)),
            pl.BlockSpec((1, d), lambda i, j: (0, 0)),
            pl.BlockSpec((bs, 1, d), lambda i, j: (i, 0, 0)),
            pl.BlockSpec((bs, 1, d), lambda i, j: (i, 0, 0)),
        ],
        out_specs=pl.BlockSpec((bs, tt, d), lambda i, j: (i, j, 0)),
        compiler_params=_params("arbitrary", "arbitrary"),
        name="mod_norm",
    )(x, w.reshape(1, d), sc, sh)


def _resident(shape, layer=None):
    if layer is None:
        return pl.BlockSpec(shape, lambda i: (0,) * len(shape), pipeline_mode=pl.Buffered(1))
    return pl.BlockSpec((None,) + tuple(shape), lambda i: (layer,) + (0,) * len(shape), pipeline_mode=pl.Buffered(1))


def _cast_kernel(x_ref, o_ref, *, valid_cols):
    x = x_ref[...]
    if valid_cols is not None:
        x = jnp.where(lax.broadcasted_iota(I32, x.shape, 1) < valid_cols, x, 0.0)
    o_ref[...] = x.astype(o_ref.dtype)


def _to_bf16(w, col0=0, ncols=None, valid_cols=None):
    depth, rows, cols = w.shape
    ncols = cols - col0 if ncols is None else ncols
    tc = LANES
    while tc * 2 <= CAST_BLOCK_COLS and col0 % (tc * 2) == 0 and ncols % (tc * 2) == 0:
        tc *= 2
    assert col0 % tc == 0 and ncols % tc == 0 and (valid_cols is None or ncols == tc)
    tr = _tile(rows, max(CAST_BLOCK_ELEMS // tc, 2 * SUBLANES), 2 * SUBLANES)
    c0 = col0 // tc
    return pl.pallas_call(
        functools.partial(_cast_kernel, valid_cols=valid_cols),
        out_shape=jax.ShapeDtypeStruct((depth, rows, ncols), BF16),
        grid=(depth, rows // tr, ncols // tc),
        in_specs=[pl.BlockSpec((None, tr, tc), lambda l, i, j: (l, i, c0 + j))],
        out_specs=pl.BlockSpec((None, tr, tc), lambda l, i, j: (l, i, j)),
        compiler_params=_params("arbitrary", "arbitrary", "arbitrary"),
        name="cast_bf16",
    )(w)


def _proj_ret_kernel(h_ref, w_ref, cos_ref, sin_ref, o_ref, *, heads, dk, kscale):
    h = h_ref[...]
    tn = heads * dk
    cos = cos_ref[...]
    sin = sin_ref[...]
    for sec, mult in ((0, 1.0), (1, kscale)):
        acc = _dot(h, w_ref[:, sec * tn:(sec + 1) * tn])
        for hd in range(heads):
            a = acc[:, hd * dk:(hd + 1) * dk]
            r = a * cos + pltpu.roll(a, dk // 2, 1) * sin
            o_ref[:, sec * tn + hd * dk:sec * tn + (hd + 1) * dk] = (r * mult).astype(o_ref.dtype)
    o_ref[:, 2 * tn:3 * tn] = _dot(h, w_ref[:, 2 * tn:3 * tn]).astype(o_ref.dtype)
    o_ref[:, 3 * tn:] = _silu(_dot(h, w_ref[:, 3 * tn:])).astype(o_ref.dtype)


def _proj_ret(h2d, w, layer, cos, sin, *, heads, dk, tm):
    n, d = h2d.shape
    tn = heads * dk
    assert w.shape[1:] == (d, 4 * tn) and cos.shape[0] % tm == 0
    nt = cos.shape[0] // tm
    return pl.pallas_call(
        functools.partial(_proj_ret_kernel, heads=heads, dk=dk, kscale=dk ** -0.5),
        out_shape=jax.ShapeDtypeStruct((n, 4 * tn), BF16),
        grid=(n // tm,),
        in_specs=[
            pl.BlockSpec((tm, d), lambda i: (i, 0)),
            _resident((d, 4 * tn), layer),
            pl.BlockSpec((tm, dk), lambda i: (i % nt, 0)),
            pl.BlockSpec((tm, dk), lambda i: (i % nt, 0)),
        ],
        out_specs=pl.BlockSpec((tm, 4 * tn), lambda i: (i, 0)),
        compiler_params=_params("arbitrary"),
        name="proj_ret",
    )(h2d, w, cos, sin)


def _proj_gla_kernel(h_ref, w_ref, wc_ref, wup_hi_ref, wup_lo_ref, bup_ref, qk_ref, vg_ref, la_ref, *, hdk, hdv, qscale):
    h = h_ref[...]
    qk = _dot(h, w_ref[:, :2 * hdk])
    qk_ref[:, :hdk] = qk[:, :hdk] * qscale
    qk_ref[:, hdk:] = qk[:, hdk:]
    vg_ref[:, :hdv] = _dot(h, w_ref[:, 2 * hdk:2 * hdk + hdv]).astype(vg_ref.dtype)
    vg_ref[:, hdv:] = _silu(_dot(h, w_ref[:, 2 * hdk + hdv:])).astype(vg_ref.dtype)
    z = _dot_split(_dot(h, wc_ref[...]), wup_hi_ref[...], wup_lo_ref[...])
    la_ref[...] = _log_sigmoid(z + bup_ref[...]) * (1.0 / GLA_TAU)


def _proj_gla(h2d, w, wc, layer, wup_hi, wup_lo, bup, *, hdk, hdv, dk, tm):
    n, d = h2d.shape
    assert w.shape[1:] == (d, 2 * hdk + 2 * hdv) and wc.shape[1:] == (d, LANES) and wup_hi.shape == (LANES, hdk)
    return pl.pallas_call(
        functools.partial(_proj_gla_kernel, hdk=hdk, hdv=hdv, qscale=dk ** -0.5),
        out_shape=(
            jax.ShapeDtypeStruct((n, 2 * hdk), F32),
            jax.ShapeDtypeStruct((n, 2 * hdv), BF16),
            jax.ShapeDtypeStruct((n, hdk), F32),
        ),
        grid=(n // tm,),
        in_specs=[
            pl.BlockSpec((tm, d), lambda i: (i, 0)),
            _resident(w.shape[1:], layer),
            _resident(wc.shape[1:], layer),
            _resident(wup_hi.shape),
            _resident(wup_lo.shape),
            _resident(bup.shape),
        ],
        out_specs=(
            pl.BlockSpec((tm, 2 * hdk), lambda i: (i, 0)),
            pl.BlockSpec((tm, 2 * hdv), lambda i: (i, 0)),
            pl.BlockSpec((tm, hdk), lambda i: (i, 0)),
        ),
        compiler_params=_params("arbitrary"),
        name="proj_gla",
    )(h2d, w, wc, wup_hi, wup_lo, bup)


def _ret_heads(q_ref, k_ref, v_ref, g_ref, dm_ref, ind_ref, sd_ref, cd_ref, w_ref, o_ref, st_ref, *, heads, dk, dv):
    for hd in range(heads):
        q = q_ref[0, :, hd * dk:(hd + 1) * dk]
        k = k_ref[0, :, hd * dk:(hd + 1) * dk]
        v = v_ref[0, :, hd * dv:(hd + 1) * dv]
        s = st_ref[hd]
        p = (_dot_nt(q, k) * dm_ref[hd]).astype(BF16)
        o = _dot(p, v) + _dot(q, s.astype(BF16)) * ind_ref[hd]
        ks = (k.astype(F32) * sd_ref[hd]).astype(BF16)
        st_ref[hd] = s * cd_ref[hd] + _dot_tn(ks, v)
        oc = o - jnp.mean(o, axis=-1, keepdims=True)
        on = oc * lax.rsqrt(jnp.mean(oc * oc, axis=-1, keepdims=True) + EPS)
        gate = g_ref[0, :, hd * dv:(hd + 1) * dv].astype(F32)
        o_ref[0, :, hd * dv:(hd + 1) * dv] = (gate * (on * w_ref[:, hd * dv:(hd + 1) * dv])).astype(o_ref.dtype)


def _ret_tables(heads, c, dk, dv):
    log_gamma = jnp.log1p(-jnp.exp2(RET_DECAY_LOG2_BASE - jnp.arange(heads, dtype=F32)))
    idx = jnp.arange(c, dtype=F32)
    diff = idx[:, None] - idx[None, :]
    causal = diff >= 0
    dmask = jnp.where(causal[None], jnp.exp(log_gamma[:, None, None] * jnp.where(causal, diff, 0.0)[None]), 0.0)
    inner = jnp.exp(log_gamma[:, None] * (idx + 1.0))
    sdecay = jnp.exp(log_gamma[:, None] * (c - 1.0 - idx))
    cdecay = jnp.exp(log_gamma * c)
    return (dmask,
            jnp.broadcast_to(inner[:, :, None], (heads, c, dv)),
            jnp.broadcast_to(sdecay[:, :, None], (heads, c, dk)),
            jnp.broadcast_to(cdecay[:, None, None], (heads, 1, dv)))


def _gla_heads(q_ref, k_ref, la_ref, v_ref, g_ref, w_ref, o_ref, st_ref, *, heads, dk, dv, c):
    row = lax.broadcasted_iota(I32, (c, dk), 0)
    differ = lax.broadcasted_iota(I32, (c, c), 0) ^ lax.broadcasted_iota(I32, (c, c), 1)
    owner = jnp.full((c, c), -1, I32)
    uppers, signs = [], []
    half = 1
    while half < c:
        owner = owner + (differ >= half).astype(I32)
        up = (row & (2 * half - 1)) >= half
        uppers.append(up)
        signs.append(jnp.where(up, LOG2_E, -LOG2_E))
        half *= 2
    for hd in range(heads):
        q = q_ref[0, :, hd * dk:(hd + 1) * dk]
        k = k_ref[0, :, hd * dk:(hd + 1) * dk]
        v = v_ref[0, :, hd * dv:(hd + 1) * dv]
        b = la_ref[0, :, hd * dk:(hd + 1) * dk]
        sh = 1
        while sh < c:
            b = b + jnp.where(row >= sh, pltpu.roll(b, sh, 0), 0.0)
            sh *= 2
        scores = jnp.where(owner < 0, _dot_nt(q.astype(BF16), k.astype(BF16)), 0.0)
        first = b
        for lvl, (upper, sign) in enumerate(zip(uppers, signs)):
            half = 1 << lvl
            mid = jnp.where(upper, first, pltpu.roll(first, c - half, 0))
            scaled = jnp.where(upper, q, k) * jnp.exp2(jnp.minimum((b - mid) * sign, 0.0))
            ql = jnp.where(upper, scaled, 0.0).astype(BF16)
            kl = jnp.where(upper, 0.0, scaled).astype(BF16)
            scores = jnp.where(owner == lvl, _dot_nt(ql, kl), scores)
            first = jnp.where(upper, pltpu.roll(first, half, 0), first)
        st = st_ref[hd]
        o = _dot(scores.astype(BF16), v) + _dot_nt((q * jnp.exp(b)).astype(BF16), st.astype(BF16))
        b_last = b[c - 1:c, :]
        kd = (k * jnp.exp(b_last - b)).astype(BF16)
        st_ref[hd] = st * jnp.exp(b_last) + _dot_tn(v, kd)
        on = o * lax.rsqrt(jnp.mean(o * o, axis=-1, keepdims=True) + EPS)
        gate = g_ref[0, :, hd * dv:(hd + 1) * dv].astype(F32)
        o_ref[0, :, hd * dv:(hd + 1) * dv] = (gate * (on * w_ref[:, hd * dv:(hd + 1) * dv])).astype(o_ref.dtype)


def _mixers_kernel(rq_ref, rk_ref, rv_ref, rg_ref, rs0_ref, dm_ref, ind_ref, sd_ref, cd_ref, rw_ref,
                   gq_ref, gk_ref, la_ref, gv_ref, gg_ref, gs0_ref, gw_ref,
                   ro_ref, rso_ref, go_ref, gso_ref, rst_ref, gst_ref, *, ret, gla, c):
    ci = pl.program_id(1)

    @pl.when(ci == 0)
    def _():
        rst_ref[...] = rs0_ref[0]
        gst_ref[...] = gs0_ref[0]

    rh, rdk, rdv = ret
    gh, gdk, gdv = gla
    _ret_heads(rq_ref, rk_ref, rv_ref, rg_ref, dm_ref, ind_ref, sd_ref, cd_ref, rw_ref, ro_ref, rst_ref,
               heads=rh, dk=rdk, dv=rdv)
    _gla_heads(gq_ref, gk_ref, la_ref, gv_ref, gg_ref, gw_ref, go_ref, gst_ref, heads=gh, dk=gdk, dv=gdv, c=c)

    @pl.when(ci == pl.num_programs(1) - 1)
    def _():
        rso_ref[0] = rst_ref[...]
        gso_ref[0] = gst_ref[...]


def _mixers(qkvg, qk, vg, la, s0_ret, s0_gla_t, ret_norm_w, gla_norm_w, *, ret, gla):
    s, t, _ = qkvg.shape
    rh, rdk, rdv = ret
    gh, gdk, gdv = gla
    assert rdk == rdv
    c = _tile(t, CHUNK_MIXERS, 2 * SUBLANES)
    assert c & (c - 1) == 0, "chunk length must be a power of two"
    rw = rh * rdk
    gwk, gwv = gh * gdk, gh * gdv
    dm, ind, sd, cd = _ret_tables(rh, c, rdk, rdv)
    const3 = lambda b, i: (0, 0, 0)
    const2 = lambda b, i: (0, 0)
    chunk = lambda col: (lambda b, i: (b, i, col))
    state = lambda b, i: (b, 0, 0, 0)
    return pl.pallas_call(
        functools.partial(_mixers_kernel, ret=ret, gla=gla, c=c),
        out_shape=(
            jax.ShapeDtypeStruct((s, t, rw), BF16), jax.ShapeDtypeStruct(s0_ret.shape, F32),
            jax.ShapeDtypeStruct((s, t, gwv), BF16), jax.ShapeDtypeStruct(s0_gla_t.shape, F32),
        ),
        grid=(s, t // c),
        in_specs=[
            pl.BlockSpec((1, c, rw), chunk(0)),
            pl.BlockSpec((1, c, rw), chunk(1)),
            pl.BlockSpec((1, c, rw), chunk(2)),
            pl.BlockSpec((1, c, rw), chunk(3)),
            pl.BlockSpec((1, rh, rdk, rdv), state),
            pl.BlockSpec((rh, c, c), const3),
            pl.BlockSpec((rh, c, rdv), const3),
            pl.BlockSpec((rh, c, rdk), const3),
            pl.BlockSpec((rh, 1, rdv), const3),
            pl.BlockSpec((1, rw), const2),
            pl.BlockSpec((1, c, gwk), chunk(0)),
            pl.BlockSpec((1, c, gwk), chunk(1)),
            pl.BlockSpec((1, c, gwk), chunk(0)),
            pl.BlockSpec((1, c, gwv), chunk(0)),
            pl.BlockSpec((1, c, gwv), chunk(1)),
            pl.BlockSpec((1, gh, gdv, gdk), state),
            pl.BlockSpec((1, gwv), const2),
        ],
        out_specs=(
            pl.BlockSpec((1, c, rw), chunk(0)),
            pl.BlockSpec((1, rh, rdk, rdv), state),
            pl.BlockSpec((1, c, gwv), chunk(0)),
            pl.BlockSpec((1, gh, gdv, gdk), state),
        ),
        scratch_shapes=[pltpu.VMEM((rh, rdk, rdv), F32), pltpu.VMEM((gh, gdv, gdk), F32)],
        compiler_params=_params("arbitrary", "arbitrary"),
        name="token_mixers",
    )(qkvg, qkvg, qkvg, qkvg, s0_ret, dm, ind, sd, cd, ret_norm_w.reshape(1, rw),
      qk, qk, la, vg, vg, s0_gla_t, gla_norm_w.reshape(1, gwv))


def _merge_kernel(br_ref, bg_ref, h_ref, wr_ref, wg_ref, wm0_ref, wm1_ref, bm0_ref, bm1_ref, o_ref):
    h = h_ref[...]
    g0 = _sigmoid(_dot(h, wm0_ref[...]) + bm0_ref[...])
    g1 = _sigmoid(_dot(h, wm1_ref[...]) + bm1_ref[...])
    y = g0 * _dot(br_ref[...], wr_ref[...]) + g1 * _dot(bg_ref[...], wg_ref[...])
    o_ref[...] = y.astype(o_ref.dtype)


def _merge(o_ret, o_gla, h2d, w_branch, w_merge, layer, b_merge, *, tm):
    n, d = h2d.shape
    wdt = o_ret.shape[1]
    tn = _tile(d, 512, LANES)
    nj = d // tn
    b2 = b_merge.reshape(1, 2 * d)
    return pl.pallas_call(
        _merge_kernel,
        out_shape=jax.ShapeDtypeStruct((n, d), BF16),
        grid=(n // tm, nj),
        in_specs=[
            pl.BlockSpec((tm, wdt), lambda i, j: (i, 0)),
            pl.BlockSpec((tm, wdt), lambda i, j: (i, 0)),
            pl.BlockSpec((tm, d), lambda i, j: (i, 0)),
            pl.BlockSpec((None, wdt, tn), lambda i, j: (2 * layer, 0, j)),
            pl.BlockSpec((None, wdt, tn), lambda i, j: (2 * layer + 1, 0, j)),
            pl.BlockSpec((None, d, tn), lambda i, j: (layer, 0, j)),
            pl.BlockSpec((None, d, tn), lambda i, j: (layer, 0, j + nj)),
            pl.BlockSpec((1, tn), lambda i, j: (0, j)),
            pl.BlockSpec((1, tn), lambda i, j: (0, j + nj)),
        ],
        out_specs=pl.BlockSpec((tm, tn), lambda i, j: (i, j)),
        compiler_params=_params("arbitrary", "arbitrary"),
        name="branch_merge",
    )(o_ret, o_gla, h2d, w_branch, w_branch, w_merge, w_merge, b2, b2)


def _pack_bf16_pairs(h):
    half = h.shape[-1] // 2
    a = lax.bitcast_convert_type(h[:, :half].astype(BF16).astype(F32), U32)
    b = lax.bitcast_convert_type(h[:, half:].astype(BF16).astype(F32), U32)
    return a | (b >> 16)


def _unpack_bf16_pairs(w, dtype):
    a = lax.bitcast_convert_type(w & jnp.uint32(0xFFFF0000), F32)
    b = lax.bitcast_convert_type(w << 16, F32)
    return jnp.concatenate([a, b], axis=-1).astype(dtype)


def _route(logits, n_groups, per_group):
    lane = lax.broadcasted_iota(I32, logits.shape, 1).astype(F32)
    neg = jnp.float32(-jnp.inf)

    def first_max(mask):
        m = jnp.max(jnp.where(mask, logits, neg), axis=-1, keepdims=True)
        idx = jnp.min(jnp.where(mask & (logits == m), lane, float(LANES)), axis=-1, keepdims=True)
        return m, idx

    gmask = lane < n_groups
    gmax, gidx = first_max(gmask)
    p_group = 1.0 / jnp.sum(jnp.where(gmask, jnp.exp(logits - gmax), 0.0), axis=-1, keepdims=True)
    lo = n_groups + gidx * per_group
    emask = (lane >= lo) & (lane < lo + per_group)
    m1, i1 = first_max(emask)
    m2, i2 = first_max(emask & (lane != i1))
    w1 = 1.0 / (1.0 + jnp.exp(m2 - m1))
    w2 = 1.0 - w1
    return jnp.where(lane == 0.0, i1 - n_groups, jnp.where(lane == 1.0, i2 - n_groups, jnp.where(
        lane == 2.0, p_group * w1, jnp.where(lane == 3.0, p_group * w2, 0.0))))


def _with_ranks(rt, carry_ref):
    tr = rt.shape[0]
    lane = lax.broadcasted_iota(I32, rt.shape, 1).astype(F32)
    a1 = lane == rt[:, 0:1]
    a2 = lane == rt[:, 1:2]
    hit = jnp.where(a1 | a2, 1.0, 0.0)
    ri = lax.broadcasted_iota(I32, (tr, tr), 0)
    cj = lax.broadcasted_iota(I32, (tr, tr), 1)
    before = _dot(jnp.where(ri > cj, 1.0, 0.0).astype(BF16), hit.astype(BF16)) + carry_ref[...]
    k1 = jnp.sum(jnp.where(a1, before, 0.0), axis=-1, keepdims=True)
    k2 = jnp.sum(jnp.where(a2, before, 0.0), axis=-1, keepdims=True)
    carry_ref[...] += jnp.sum(hit, axis=0, keepdims=True)
    return jnp.where(lane == 4.0, k1, jnp.where(lane == 5.0, k2, rt))


def _out_proj_kernel(m_ref, x_ref, g1_ref, sc_ref, sh_ref, nw_ref, wo_ref, wr_ref, br_ref,
                     x1_ref, hp_ref, rt_ref, cnt_ref, carry_ref, *, n_groups, per_group):
    bs, tt, d = x_ref.shape

    @pl.when((pl.program_id(0) == 0) & (pl.program_id(1) == 0))
    def _():
        carry_ref[...] = jnp.zeros_like(carry_ref)

    y = _dot(m_ref[...].reshape(bs * tt, d), wo_ref[...]).reshape(bs, tt, d)
    x1 = x_ref[...] + g1_ref[...] * y
    x1_ref[...] = x1
    hn = x1 * lax.rsqrt(jnp.mean(x1 * x1, axis=-1, keepdims=True) + EPS) * nw_ref[...]
    h = (hn * (1.0 + sc_ref[...]) + sh_ref[...]).reshape(bs * tt, d)
    _store_token_major(hp_ref, _pack_bf16_pairs(h))
    h_hi, h_lo = _split_bf16(h)
    p = _dot(h_hi, wr_ref[...])
    logits = p[:, :LANES] + p[:, LANES:] + _dot(h_lo, wr_ref[:, :LANES]) + br_ref[...]
    rt = _with_ranks(_route(logits, n_groups, per_group), carry_ref)
    rt_ref[...] = rt.reshape(bs, tt, LANES)
    cnt_ref[...] = carry_ref[...]


def _out_proj(merged, x, g1, sc2, sh2, norm_w, w_o, layer, wr, b_r, *, n_groups, per_group):
    s, t, d = x.shape
    bs, tt = _seq_tiles(s, t, ROWS_OUT_PROJ)
    chunks = d // 2 // LANES
    tok = lambda i, j: (i, j, 0)
    seq = lambda i, j: (i, 0, 0)
    const = lambda i, j: (0, 0)
    return pl.pallas_call(
        functools.partial(_out_proj_kernel, n_groups=n_groups, per_group=per_group),
        out_shape=(
            jax.ShapeDtypeStruct((s, t, d), F32),
            jax.ShapeDtypeStruct((s * t * chunks, LANES), U32),
            jax.ShapeDtypeStruct((s, t, LANES), F32),
            jax.ShapeDtypeStruct((1, LANES), F32),
        ),
        grid=(s // bs, t // tt),
        in_specs=[
            pl.BlockSpec((bs, tt, d), tok),
            pl.BlockSpec((bs, tt, d), tok),
            pl.BlockSpec((bs, 1, d), seq),
            pl.BlockSpec((bs, 1, d), seq),
            pl.BlockSpec((bs, 1, d), seq),
            pl.BlockSpec((1, d), const),
            pl.BlockSpec((None, d, d), lambda i, j: (layer, 0, 0), pipeline_mode=pl.Buffered(1)),
            pl.BlockSpec((d, 2 * LANES), const, pipeline_mode=pl.Buffered(1)),
            pl.BlockSpec((1, LANES), const),
        ],
        out_specs=(
            pl.BlockSpec((bs, tt, d), tok),
            pl.BlockSpec((bs * tt * chunks, LANES), lambda i, j: (i * (t // tt) + j, 0)),
            pl.BlockSpec((bs, tt, LANES), tok),
            pl.BlockSpec((1, LANES), const),
        ),
        scratch_shapes=[pltpu.VMEM((1, LANES), F32)],
        compiler_params=_params("arbitrary", "arbitrary"),
        name="out_proj_router",
    )(merged, x, g1, sc2, sh2, norm_w.reshape(1, d), w_o, wr, b_r)


def _invert_kernel(pos_ref, pad_lo_ref, pad_hi_ref, src_ref, *, n_tokens, n_exp):
    def clear(j, carry):
        src_ref[j] = 0
        return carry

    for e in range(n_exp):
        lax.fori_loop(pad_lo_ref[e], pad_hi_ref[e], clear, 0)

    def put(t, carry):
        for kk in range(TOP_K):
            src_ref[pos_ref[TOP_K * t + kk]] = t
        return carry

    lax.fori_loop(0, n_tokens, put, 0, unroll=INVERT_UNROLL)


def _invert(pos, pad_lo, pad_hi, n_rows):
    n_tokens = pos.shape[0] // TOP_K
    return pl.pallas_call(
        functools.partial(_invert_kernel, n_tokens=n_tokens, n_exp=pad_lo.shape[0]),
        out_shape=jax.ShapeDtypeStruct((n_rows,), I32),
        grid_spec=pltpu.PrefetchScalarGridSpec(
            num_scalar_prefetch=3,
            grid=(1,),
            in_specs=[],
            out_specs=pl.BlockSpec(memory_space=pltpu.SMEM),
        ),
        compiler_params=_params("arbitrary"),
        name="moe_invert",
    )(pos, pad_lo, pad_hi)


def _store_token_major(ref, value):
    chunks = value.shape[1] // LANES
    for c in range(chunks):
        ref[pl.ds(c, value.shape[0], stride=chunks), :] = value[:, c * LANES:(c + 1) * LANES]


def _load_token_major(ref, rows, chunks):
    return jnp.concatenate([ref[pl.ds(c, rows, stride=chunks), :] for c in range(chunks)], axis=-1)


def _token_copy(src_ref, src_tok, dst_ref, dst_tok, chunks, sem):
    def at(ref, tok):
        start = tok * chunks
        if not isinstance(start, int):
            start = pl.multiple_of(start, chunks)
        return ref.at[pl.ds(start, chunks)]

    return pltpu.make_async_copy(at(src_ref, src_tok), at(dst_ref, dst_tok), sem)


def _ffn_kernel(te_ref, valid_ref, src_ref, hp_ref, wg_ref, wu_ref, wd_ref, ys_ref, xbuf, sems, *, tf, chunks):
    del te_ref
    i = pl.program_id(0)
    last = pl.num_programs(0) - 1
    slot = i % 2
    nxt = jnp.where(i < last, i + 1, 0)

    def start(tile, slt, r):
        _token_copy(hp_ref, src_ref[tile * tf + r], xbuf.at[slt], r, chunks, sems.at[slt]).start()

    def drain(slt):
        def wait(r, carry):
            _token_copy(hp_ref, 0, xbuf.at[slt], 0, chunks, sems.at[slt]).wait()
            return carry

        lax.fori_loop(0, tf, wait, 0, unroll=GATHER_UNROLL)

    @pl.when(i == 0)
    def _():
        def issue(r, carry):
            start(0, 0, r)
            return carry

        lax.fori_loop(0, tf, issue, 0, unroll=GATHER_UNROLL)

    def gather_next(slot):
        for r in range(tf):
            start(nxt, 1 - slot, r)

    def step(slot, fused_issue):
        if not fused_issue:
            gather_next(slot)
        drain(slot)

        @pl.when(valid_ref[i] != 0)
        def _():
            x = _unpack_bf16_pairs(_load_token_major(xbuf.at[slot], tf, chunks), BF16)
            if fused_issue:
                gather_next(slot)
            act = (_silu(_dot(x, wg_ref[...])) * _dot(x, wu_ref[...])).astype(BF16)
            _store_token_major(ys_ref, _pack_bf16_pairs(_dot(act, wd_ref[...])))

        @pl.when(valid_ref[i] == 0)
        def _():
            if fused_issue:
                gather_next(slot)
            ys_ref[...] = jnp.zeros_like(ys_ref)

        @pl.when(i == last)
        def _():
            drain(1 - slot)

    pl.when(slot == 0)(functools.partial(step, 0, True))
    pl.when(slot == 1)(functools.partial(step, 1, False))


def _ffn(tile_expert, tile_valid, src, hp, wg, wu, wd, *, tf):
    n_rows = src.shape[0]
    _, d, f = wg.shape
    chunks = d // 2 // LANES
    return pl.pallas_call(
        functools.partial(_ffn_kernel, tf=tf, chunks=chunks),
        out_shape=jax.ShapeDtypeStruct((n_rows * chunks, LANES), U32),
        grid_spec=pltpu.PrefetchScalarGridSpec(
            num_scalar_prefetch=3,
            grid=(n_rows // tf,),
            in_specs=[
                pl.BlockSpec(memory_space=pl.ANY),
                pl.BlockSpec((None, d, f), lambda i, te, tv, sr: (te[i], 0, 0)),
                pl.BlockSpec((None, d, f), lambda i, te, tv, sr: (te[i], 0, 0)),
                pl.BlockSpec((None, f, d), lambda i, te, tv, sr: (te[i], 0, 0)),
            ],
            out_specs=pl.BlockSpec((tf * chunks, LANES), lambda i, te, tv, sr: (i, 0)),
            scratch_shapes=[pltpu.VMEM((2, tf * chunks, LANES), hp.dtype), pltpu.SemaphoreType.DMA((2,))],
        ),
        compiler_params=_params("arbitrary"),
        name="moe_ffn",
    )(tile_expert, tile_valid, src, hp, wg, wu, wd)


def _combine_kernel(pos_ref, x_ref, rt_ref, g2_ref, nw_ref, sc_ref, sh_ref, ys_ref, *rest, last):
    if last:
        o_ref, ybuf, sems = rest
    else:
        o_ref, h_ref, ybuf, sems = rest
    bs, tt, d = x_ref.shape
    rows = bs * tt
    chunks = d // 2 // LANES
    step = pl.program_id(0) * pl.num_programs(1) + pl.program_id(1)
    n_steps = pl.num_programs(0) * pl.num_programs(1)

    def start(stp, slt, r, kk):
        _token_copy(ys_ref, pos_ref[TOP_K * (stp * rows + r) + kk], ybuf.at[slt, kk], r, chunks,
                    sems.at[slt]).start()

    @pl.when(step == 0)
    def _():
        def issue(r, carry):
            for kk in range(TOP_K):
                start(0, 0, r, kk)
            return carry

        lax.fori_loop(0, rows, issue, 0, unroll=GATHER_UNROLL)

    slot = step % 2

    @pl.when(step + 1 < n_steps)
    def _():
        for r in range(rows):
            for kk in range(TOP_K):
                start(step + 1, 1 - slot, r, kk)

    def wait(r, carry):
        for kk in range(TOP_K):
            _token_copy(ys_ref, 0, ybuf.at[slot, kk], 0, chunks, sems.at[slot]).wait()
        return carry

    lax.fori_loop(0, rows, wait, 0, unroll=GATHER_UNROLL)

    rt = rt_ref[...].reshape(rows, LANES)
    y = (rt[:, 2:3] * _unpack_bf16_pairs(_load_token_major(ybuf.at[slot, 0], rows, chunks), F32)
         + rt[:, 3:4] * _unpack_bf16_pairs(_load_token_major(ybuf.at[slot, 1], rows, chunks), F32))
    xn = x_ref[...] + g2_ref[...] * y.reshape(bs, tt, d)
    normed = xn * lax.rsqrt(jnp.mean(xn * xn, axis=-1, keepdims=True) + EPS) * nw_ref[...]
    if last:
        o_ref[...] = normed
    else:
        o_ref[...] = xn
        h_ref[...] = (normed * (1.0 + sc_ref[...]) + sh_ref[...]).astype(h_ref.dtype)


def _combine(pos, x1, route, g2, norm_w, sc, sh, ys, *, last):
    s, t, d = x1.shape
    bs, tt = _seq_tiles(s, t, ROWS_COMBINE)
    tok = lambda i, j, p: (i, j, 0)
    seq = lambda i, j, p: (i, 0, 0)
    x_shape = jax.ShapeDtypeStruct((s, t, d), F32)
    x_spec = pl.BlockSpec((bs, tt, d), tok)
    return pl.pallas_call(
        functools.partial(_combine_kernel, last=last),
        out_shape=x_shape if last else (x_shape, jax.ShapeDtypeStruct((s, t, d), BF16)),
        grid_spec=pltpu.PrefetchScalarGridSpec(
            num_scalar_prefetch=1,
            grid=(s // bs, t // tt),
            in_specs=[
                x_spec,
                pl.BlockSpec((bs, tt, LANES), tok),
                pl.BlockSpec((bs, 1, d), seq),
                pl.BlockSpec((1, d), lambda i, j, p: (0, 0)),
                pl.BlockSpec((bs, 1, d), seq),
                pl.BlockSpec((bs, 1, d), seq),
                pl.BlockSpec(memory_space=pl.ANY),
            ],
            out_specs=x_spec if last else (x_spec, x_spec),
            scratch_shapes=[pltpu.VMEM((2, TOP_K, bs * tt * (d // 2 // LANES), LANES), U32),
                            pltpu.SemaphoreType.DMA((2,))],
        ),
        compiler_params=_params("arbitrary", "arbitrary"),
        name="moe_combine",
    )(pos, x1, route, g2, norm_w.reshape(1, d), sc, sh, ys)


def _moe(x1, hp, route, counts, g2, wg, wu, wd, layer, n_exp, norm_w, sc, sh, *, last):
    s, t, d = x1.shape
    n = s * t
    route2d = route.reshape(n, LANES)
    tf = 2 * SUBLANES
    while tf < ROWS_FFN and tf < 2 * TOP_K * n // n_exp:
        tf *= 2
    n_tiles = (TOP_K * n) // tf + n_exp
    cnt = counts[0, :n_exp].astype(I32)
    padded = ((cnt + tf - 1) // tf) * tf
    ends = jnp.cumsum(padded)
    starts = ends - padded
    expert = route2d[:, :TOP_K].astype(I32)
    rank = route2d[:, 4:4 + TOP_K].astype(I32)
    start_of = jnp.sum(jnp.where(expert[..., None] == jnp.arange(n_exp, dtype=I32), starts, 0), axis=-1)
    pos = (start_of + rank).reshape(TOP_K * n)
    tile_start = jnp.arange(n_tiles, dtype=I32) * tf
    tile_valid = (tile_start < ends[-1]).astype(I32)
    tile_expert = jnp.minimum(jnp.sum((ends[None, :] <= tile_start[:, None]).astype(I32), axis=1), n_exp - 1)
    last_used = jnp.max(jnp.where(tile_valid != 0, tile_expert, 0))
    tile_expert = jnp.where(tile_valid != 0, tile_expert, last_used)
    pad_hi = jnp.where(jnp.arange(n_exp) == n_exp - 1, n_tiles * tf, ends).astype(I32)
    src = _invert(pos, starts + cnt, pad_hi, n_tiles * tf)
    ys = _ffn(tile_expert + layer * n_exp, tile_valid, src, hp, wg, wu, wd, tf=tf)
    return _combine(pos, x1, route, g2, norm_w, sc, sh, ys, last=last)


def _rotary_tables(pos0, t, dk, reps):
    half = dk // 2
    inv = ROPE_BASE ** (-jnp.arange(half, dtype=F32) / half)
    ang = (pos0 + jnp.arange(t)).astype(F32)[:, None] * inv[None, :]
    cos = jnp.cos(ang)
    sin = jnp.sin(ang)
    cos2 = jnp.concatenate([cos, cos], axis=-1)
    sin2 = jnp.concatenate([-sin, sin], axis=-1)
    return jnp.tile(cos2, (reps, 1)), jnp.tile(sin2, (reps, 1))


def _trunk(x, mods, pos0, s_ret_in, s_gla_in_t, big, wts, dims):
    s, t, d = x.shape
    n = s * t
    depth = len(wts)
    rh, rdk, rdv, gh, gdk, gdv = dims["ret_heads"], dims["ret_dk"], dims["ret_dv"], dims["gla_heads"], dims["gla_dk"], dims["gla_dv"]
    bs, tt = _seq_tiles(s, t, ROWS_PROJ)
    tp = bs * tt
    tm = _tile(n, ROWS_MATMUL, 2 * SUBLANES)
    cos, sin = _rotary_tables(pos0, t, rdk, bs)
    ret_states, gla_states = [], []
    h = _norm(x, wts[0]["norm_mix"], mods[1][0], mods[0][0])
    for l in range(depth):
        w = wts[l]
        sh1, sc1, g1, sh2, sc2, g2 = (m[l] for m in mods)
        h2d = h.reshape(n, d)
        qkvg = _proj_ret(h2d, big["w_ret"], l, cos, sin, heads=rh, dk=rdk, tm=tp)
        qk, vg, la = _proj_gla(h2d, big["w_gla"], big["w_code"], l, w["wup_hi"], w["wup_lo"], w["b_gla"],
                               hdk=gh * gdk, hdv=gh * gdv, dk=gdk, tm=tp)
        o_ret, s_r, o_gla, s_g = _mixers(qkvg.reshape(s, t, -1), qk.reshape(s, t, -1), vg.reshape(s, t, -1),
                                         la.reshape(s, t, -1), s_ret_in[l], s_gla_in_t[l], w["ret_norm"],
                                         w["gla_norm"], ret=(rh, rdk, rdv), gla=(gh, gdk, gdv))
        merged = _merge(o_ret.reshape(n, -1), o_gla.reshape(n, -1), h2d, big["w_branch"], big["w_merge"], l,
                        w["b_merge"], tm=tm)
        x1, hp, route, counts = _out_proj(merged.reshape(s, t, d), x, g1, sc2, sh2, w["norm_ffn"], big["w_o"], l,
                                  w["w_router"], w["b_r"], n_groups=dims["n_groups"], per_group=dims["per_group"])
        n_exp = dims["n_groups"] * dims["per_group"]
        experts = (big["w_exp_gate"], big["w_exp_up"], big["w_exp_down"], l, n_exp)
        if l == depth - 1:
            x = _moe(x1, hp, route, counts, g2, *experts, w["final_norm"], sc1, sh1, last=True)
        else:
            x, h = _moe(x1, hp, route, counts, g2, *experts, wts[l + 1]["norm_mix"], mods[1][l + 1], mods[0][l + 1], last=False)
        ret_states.append(s_r)
        gla_states.append(jnp.swapaxes(s_g, -1, -2))
    return x, jnp.stack(ret_states), jnp.stack(gla_states)


def kernel(x_prompt, x_sample, state_ret, state_gla, c_prompt, c_sample, w_ada, b_ada, norm_mix_w, norm_ffn_w, w_in, w_gla_up, b_gla, ret_norm_w, gla_norm_w, w_branch, w_merge, b_merge, w_o, w_router_group, b_router_group, w_router_expert, b_router_expert, w_exp_gate, w_exp_up, w_exp_down, final_norm_w):
    depth, d, _ = w_in.shape
    _, _, rh, rdk, rdv = state_ret.shape
    _, _, gh, gdk, gdv = state_gla.shape
    low_rank = w_gla_up.shape[1]
    n_groups = w_router_group.shape[-1]
    n_exp = w_router_expert.shape[-1]
    assert low_rank <= LANES and n_groups + n_exp <= LANES and n_exp % n_groups == 0
    dims = dict(ret_heads=rh, ret_dk=rdk, ret_dv=rdv, gla_heads=gh, gla_dk=gdk, gla_dv=gdv,
                n_groups=n_groups, per_group=n_exp // n_groups)

    n_ret = 2 * rh * rdk + 2 * rh * rdv
    n_gla = 2 * gh * gdk + 2 * gh * gdv
    assert w_in.shape[-1] == n_ret + n_gla + low_rank
    f = w_exp_gate.shape[-1]
    big = dict(
        w_ret=_to_bf16(w_in, 0, n_ret),
        w_gla=_to_bf16(w_in, n_ret, n_gla),
        w_code=_to_bf16(w_in, n_ret + n_gla, LANES, valid_cols=low_rank),
        w_branch=_to_bf16(w_branch.reshape(depth, -1, d)).reshape(depth * w_branch.shape[1], -1, d),
        w_merge=_to_bf16(w_merge),
        w_o=_to_bf16(w_o),
        w_exp_gate=_to_bf16(w_exp_gate.reshape(depth, n_exp * d, f)).reshape(depth * n_exp, d, f),
        w_exp_up=_to_bf16(w_exp_up.reshape(depth, n_exp * d, f)).reshape(depth * n_exp, d, f),
        w_exp_down=_to_bf16(w_exp_down.reshape(depth, n_exp * f, d)).reshape(depth * n_exp, f, d),
    )
    w_router = jnp.concatenate([w_router_group, w_router_expert], axis=-1)
    w_router = jnp.pad(w_router, ((0, 0), (0, 0), (0, LANES - n_groups - n_exp)))
    b_router = jnp.pad(jnp.concatenate([b_router_group, b_router_expert], axis=-1), ((0, 0), (0, LANES - n_groups - n_exp)))
    wup = jnp.pad(w_gla_up, ((0, 0), (0, LANES - low_rank), (0, 0)))
    wts = []
    for l in range(depth):
        wr_hi, wr_lo = _split_bf16(w_router[l])
        wup_hi, wup_lo = _split_bf16(wup[l])
        wts.append(dict(
            norm_mix=norm_mix_w[l], norm_ffn=norm_ffn_w[l],
            wup_hi=wup_hi, wup_lo=wup_lo, b_gla=b_gla[l].reshape(1, -1),
            ret_norm=ret_norm_w[l], gla_norm=gla_norm_w[l], b_merge=b_merge[l],
            w_router=jnp.concatenate([wr_hi, wr_lo], axis=-1), b_r=b_router[l].reshape(1, LANES),
            final_norm=final_norm_w,
        ))

    nb, ns = c_prompt.shape[0], c_sample.shape[0]
    mod = _ada(jnp.concatenate([c_prompt, c_sample], axis=0), w_ada, b_ada)
    mod = mod.reshape(depth, nb + ns, 6, d)
    mods_p = [mod[:, :nb, i][:, :, None, :] for i in range(6)]
    mods_s = [mod[:, nb:, i][:, :, None, :] for i in range(6)]

    zero_ret = jnp.zeros((depth, nb, rh, rdk, rdv), F32)
    zero_gla_t = jnp.zeros((depth, nb, gh, gdv, gdk), F32)
    y_p, ret_p, gla_p = _trunk(x_prompt, mods_p, 0, zero_ret, zero_gla_t, big, wts, dims)
    y_s, ret_s, gla_s = _trunk(x_sample, mods_s, PAST_LEN, state_ret.astype(F32),
                               jnp.swapaxes(state_gla.astype(F32), -1, -2), big, wts, dims)
    return (y_p, y_s, ret_p, gla_p, ret_s, gla_s)
```

```python
import functools

import jax
import jax.numpy as jnp
from jax import lax
from jax.experimental import pallas as pl
from jax.experimental.pallas import tpu as pltpu

F32 = jnp.float32
BF16 = jnp.bfloat16
U32 = jnp.uint32
I32 = jnp.int32

PAST_LEN = 4096
GLA_TAU = 16.0
ROPE_BASE = 10000.0
EPS = 1e-6
RET_DECAY_LOG2_BASE = -5.0
TOP_K = 2
LOG2_E = 1.4426950408889634

LANES = 128
SUBLANES = 8
V7X_VMEM_BYTES = 64 * 1024 * 1024
VMEM_LIMIT = V7X_VMEM_BYTES - 8 * 1024 * 1024

ROWS_MATMUL = 1024
ROWS_PROJ = 512
ROWS_OUT_PROJ = 512
ROWS_NORM = 512
NORM_ROW_CHUNK = 16
NORM_UNROLL = 8
ROWS_COMBINE = 512
GATHER_UNROLL = 8
INVERT_UNROLL = 16
CAST_BLOCK_COLS = 1024
CAST_BLOCK_ELEMS = 2 * 1024 * 1024
ROWS_FFN = 512
CHUNK_MIXERS = 256


def _params(*sem):
    return pltpu.CompilerParams(dimension_semantics=sem, vmem_limit_bytes=VMEM_LIMIT)


def _tile(n, pref, align):
    if n <= pref:
        return n
    t = (pref // align) * align
    while t >= align:
        if n % t == 0:
            return t
        t -= align
    return n


def _seq_tiles(s, t, rows):
    if t >= rows:
        return 1, _tile(t, rows, 2 * SUBLANES)
    return _tile(s, max(rows // t, 1), 1), t


def _dot(a, b):
    return jnp.dot(a, b, preferred_element_type=F32)


def _dot_nt(a, b):
    return lax.dot_general(a, b, (((1,), (1,)), ((), ())), preferred_element_type=F32)


def _dot_tn(a, b):
    return lax.dot_general(a, b, (((0,), (0,)), ((), ())), preferred_element_type=F32)


def _sigmoid(x):
    return 0.5 * jnp.tanh(0.5 * x) + 0.5


def _silu(x):
    return x * _sigmoid(x)


def _log_sigmoid(z):
    return jnp.minimum(z, 0.0) - jnp.log(1.0 + jnp.exp(-jnp.abs(z)))


def _split_bf16(x):
    hi = x.astype(BF16)
    lo = (x - hi.astype(F32)).astype(BF16)
    return hi, lo


def _dot_split(a, w_hi, w_lo):
    a_hi, a_lo = _split_bf16(a)
    return _dot(a_hi, w_hi) + _dot(a_lo, w_hi) + _dot(a_hi, w_lo)


def _ada_kernel(c_ref, w_ref, b_ref, o_ref):
    a = _silu(c_ref[...]).astype(BF16)
    o_ref[...] = _dot(a, w_ref[...].astype(BF16)) + b_ref[...]


def _ada(c_all, w_ada, b_ada):
    depth, d, n6 = w_ada.shape
    r = c_all.shape[0]
    tn = _tile(n6, 1024, LANES)
    return pl.pallas_call(
        _ada_kernel,
        out_shape=jax.ShapeDtypeStruct((depth, r, n6), F32),
        grid=(depth, n6 // tn),
        in_specs=[
            pl.BlockSpec((r, d), lambda l, j: (0, 0)),
            pl.BlockSpec((None, d, tn), lambda l, j: (l, 0, j)),
            pl.BlockSpec((None, 1, tn), lambda l, j: (l, 0, j)),
        ],
        out_specs=pl.BlockSpec((None, r, tn), lambda l, j: (l, 0, j)),
        compiler_params=_params("arbitrary", "arbitrary"),
        name="ada_mod",
    )(c_all, w_ada, b_ada.reshape(depth, 1, n6))


def _norm_kernel(x_ref, w_ref, sc_ref, sh_ref, o_ref):
    tt = x_ref.shape[1]
    rc = min(NORM_ROW_CHUNK, tt)
    def piece(c, carry):
        rows = pl.ds(pl.multiple_of(c * rc, rc), rc)
        x = x_ref[:, rows, :]
        y = x * lax.rsqrt(jnp.mean(x * x, axis=-1, keepdims=True) + EPS) * w_ref[...]
        o_ref[:, rows, :] = (y * (1.0 + sc_ref[...]) + sh_ref[...]).astype(o_ref.dtype)
        return carry

    lax.fori_loop(0, tt // rc, piece, 0, unroll=min(NORM_UNROLL, tt // rc))


def _norm(x, w, sc, sh):
    s, t, d = x.shape
    bs, tt = _seq_tiles(s, t, ROWS_NORM)
    return pl.pallas_call(
        _norm_kernel,
        out_shape=jax.ShapeDtypeStruct((s, t, d), BF16),
        grid=(s // bs, t // tt),
        in_specs=[
            pl.BlockSpec((bs, tt, d), lambda i, j: (i, j, 0)),
            pl.BlockSpec((1, d), lambda i, j: (0, 0)),
            pl.BlockSpec((bs, 1, d), lambda i, j: (i, 0, 0)),
            pl.BlockSpec((bs, 1, d), lambda i, j: (i, 0, 0)),
        ],
        out_specs=pl.BlockSpec((bs, tt, d), lambda i, j: (i, j, 0)),
        compiler_params=_params("arbitrary", "arbitrary"),
        name="mod_norm",
    )(x, w.reshape(1, d), sc, sh)


def _resident(shape, layer=None):
    if layer is None:
        return pl.BlockSpec(shape, lambda i: (0,) * len(shape), pipeline_mode=pl.Buffered(1))
    return pl.BlockSpec((None,) + tuple(shape), lambda i: (layer,) + (0,) * len(shape), pipeline_mode=pl.Buffered(1))


def _cast_kernel(x_ref, o_ref, *, valid_cols):
    x = x_ref[...]
    if valid_cols is not None:
        x = jnp.where(lax.broadcasted_iota(I32, x.shape, 1) < valid_cols, x, 0.0)
    o_ref[...] = x.astype(o_ref.dtype)


def _to_bf16(w, col0=0, ncols=None, valid_cols=None):
    depth, rows, cols = w.shape
    ncols = cols - col0 if ncols is None else ncols
    tc = LANES
    while tc * 2 <= CAST_BLOCK_COLS and col0 % (tc * 2) == 0 and ncols % (tc * 2) == 0:
        tc *= 2
    assert col0 % tc == 0 and ncols % tc == 0 and (valid_cols is None or ncols == tc)
    tr = _tile(rows, max(CAST_BLOCK_ELEMS // tc, 2 * SUBLANES), 2 * SUBLANES)
    c0 = col0 // tc
    return pl.pallas_call(
        functools.partial(_cast_kernel, valid_cols=valid_cols),
        out_shape=jax.ShapeDtypeStruct((depth, rows, ncols), BF16),
        grid=(depth, rows // tr, ncols // tc),
        in_specs=[pl.BlockSpec((None, tr, tc), lambda l, i, j: (l, i, c0 + j))],
        out_specs=pl.BlockSpec((None, tr, tc), lambda l, i, j: (l, i, j)),
        compiler_params=_params("arbitrary", "arbitrary", "arbitrary"),
        name="cast_bf16",
    )(w)


def _proj_ret_kernel(h_ref, w_ref, cos_ref, sin_ref, o_ref, *, heads, dk, kscale):
    h = h_ref[...]
    tn = heads * dk
    cos = cos_ref[...]
    sin = sin_ref[...]
    for sec, mult in ((0, 1.0), (1, kscale)):
        acc = _dot(h, w_ref[:, sec * tn:(sec + 1) * tn])
        for hd in range(heads):
            a = acc[:, hd * dk:(hd + 1) * dk]
            r = a * cos + pltpu.roll(a, dk // 2, 1) * sin
            o_ref[:, sec * tn + hd * dk:sec * tn + (hd + 1) * dk] = (r * mult).astype(o_ref.dtype)
    o_ref[:, 2 * tn:3 * tn] = _dot(h, w_ref[:, 2 * tn:3 * tn]).astype(o_ref.dtype)
    o_ref[:, 3 * tn:] = _silu(_dot(h, w_ref[:, 3 * tn:])).astype(o_ref.dtype)


def _proj_ret(h2d, w, layer, cos, sin, *, heads, dk, tm):
    n, d = h2d.shape
    tn = heads * dk
    assert w.shape[1:] == (d, 4 * tn) and cos.shape[0] % tm == 0
    nt = cos.shape[0] // tm
    return pl.pallas_call(
        functools.partial(_proj_ret_kernel, heads=heads, dk=dk, kscale=dk ** -0.5),
        out_shape=jax.ShapeDtypeStruct((n, 4 * tn), BF16),
        grid=(n // tm,),
        in_specs=[
            pl.BlockSpec((tm, d), lambda i: (i, 0)),
            _resident((d, 4 * tn), layer),
            pl.BlockSpec((tm, dk), lambda i: (i % nt, 0)),
            pl.BlockSpec((tm, dk), lambda i: (i % nt, 0)),
        ],
        out_specs=pl.BlockSpec((tm, 4 * tn), lambda i: (i, 0)),
        compiler_params=_params("arbitrary"),
        name="proj_ret",
    )(h2d, w, cos, sin)


def _proj_gla_kernel(h_ref, w_ref, wc_ref, wup_hi_ref, wup_lo_ref, bup_ref, qk_ref, vg_ref, la_ref, *, hdk, hdv, qscale):
    h = h_ref[...]
    qk = _dot(h, w_ref[:, :2 * hdk])
    qk_ref[:, :hdk] = qk[:, :hdk] * qscale
    qk_ref[:, hdk:] = qk[:, hdk:]
    vg_ref[:, :hdv] = _dot(h, w_ref[:, 2 * hdk:2 * hdk + hdv]).astype(vg_ref.dtype)
    vg_ref[:, hdv:] = _silu(_dot(h, w_ref[:, 2 * hdk + hdv:])).astype(vg_ref.dtype)
    z = _dot_split(_dot(h, wc_ref[...]), wup_hi_ref[...], wup_lo_ref[...])
    la_ref[...] = _log_sigmoid(z + bup_ref[...]) * (1.0 / GLA_TAU)


def _proj_gla(h2d, w, wc, layer, wup_hi, wup_lo, bup, *, hdk, hdv, dk, tm):
    n, d = h2d.shape
    assert w.shape[1:] == (d, 2 * hdk + 2 * hdv) and wc.shape[1:] == (d, LANES) and wup_hi.shape == (LANES, hdk)
    return pl.pallas_call(
        functools.partial(_proj_gla_kernel, hdk=hdk, hdv=hdv, qscale=dk ** -0.5),
        out_shape=(
            jax.ShapeDtypeStruct((n, 2 * hdk), F32),
            jax.ShapeDtypeStruct((n, 2 * hdv), BF16),
            jax.ShapeDtypeStruct((n, hdk), F32),
        ),
        grid=(n // tm,),
        in_specs=[
            pl.BlockSpec((tm, d), lambda i: (i, 0)),
            _resident(w.shape[1:], layer),
            _resident(wc.shape[1:], layer),
            _resident(wup_hi.shape),
            _resident(wup_lo.shape),
            _resident(bup.shape),
        ],
        out_specs=(
            pl.BlockSpec((tm, 2 * hdk), lambda i: (i, 0)),
            pl.BlockSpec((tm, 2 * hdv), lambda i: (i, 0)),
            pl.BlockSpec((tm, hdk), lambda i: (i, 0)),
        ),
        compiler_params=_params("arbitrary"),
        name="proj_gla",
    )(h2d, w, wc, wup_hi, wup_lo, bup)


def _ret_heads(q_ref, k_ref, v_ref, g_ref, dm_ref, ind_ref, sd_ref, cd_ref, w_ref, o_ref, st_ref, *, heads, dk, dv):
    for hd in range(heads):
        q = q_ref[0, :, hd * dk:(hd + 1) * dk]
        k = k_ref[0, :, hd * dk:(hd + 1) * dk]
        v = v_ref[0, :, hd * dv:(hd + 1) * dv]
        s = st_ref[hd]
        p = (_dot_nt(q, k) * dm_ref[hd]).astype(BF16)
        o = _dot(p, v) + _dot(q, s.astype(BF16)) * ind_ref[hd]
        ks = (k.astype(F32) * sd_ref[hd]).astype(BF16)
        st_ref[hd] = s * cd_ref[hd] + _dot_tn(ks, v)
        oc = o - jnp.mean(o, axis=-1, keepdims=True)
        on = oc * lax.rsqrt(jnp.mean(oc * oc, axis=-1, keepdims=True) + EPS)
        gate = g_ref[0, :, hd * dv:(hd + 1) * dv].astype(F32)
        o_ref[0, :, hd * dv:(hd + 1) * dv] = (gate * (on * w_ref[:, hd * dv:(hd + 1) * dv])).astype(o_ref.dtype)


def _ret_tables(heads, c, dk, dv):
    log_gamma = jnp.log1p(-jnp.exp2(RET_DECAY_LOG2_BASE - jnp.arange(heads, dtype=F32)))
    idx = jnp.arange(c, dtype=F32)
    diff = idx[:, None] - idx[None, :]
    causal = diff >= 0
    dmask = jnp.where(causal[None], jnp.exp(log_gamma[:, None, None] * jnp.where(causal, diff, 0.0)[None]), 0.0)
    inner = jnp.exp(log_gamma[:, None] * (idx + 1.0))
    sdecay = jnp.exp(log_gamma[:, None] * (c - 1.0 - idx))
    cdecay = jnp.exp(log_gamma * c)
    return (dmask,
            jnp.broadcast_to(inner[:, :, None], (heads, c, dv)),
            jnp.broadcast_to(sdecay[:, :, None], (heads, c, dk)),
            jnp.broadcast_to(cdecay[:, None, None], (heads, 1, dv)))


def _gla_heads(q_ref, k_ref, la_ref, v_ref, g_ref, w_ref, o_ref, st_ref, *, heads, dk, dv, c):
    row = lax.broadcasted_iota(I32, (c, dk), 0)
    differ = lax.broadcasted_iota(I32, (c, c), 0) ^ lax.broadcasted_iota(I32, (c, c), 1)
    owner = jnp.full((c, c), -1, I32)
    uppers, signs = [], []
    half = 1
    while half < c:
        owner = owner + (differ >= half).astype(I32)
        up = (row & (2 * half - 1)) >= half
        uppers.append(up)
        signs.append(jnp.where(up, LOG2_E, -LOG2_E))
        half *= 2
    for hd in range(heads):
        q = q_ref[0, :, hd * dk:(hd + 1) * dk]
        k = k_ref[0, :, hd * dk:(hd + 1) * dk]
        v = v_ref[0, :, hd * dv:(hd + 1) * dv]
        b = la_ref[0, :, hd * dk:(hd + 1) * dk]
        sh = 1
        while sh < c:
            b = b + jnp.where(row >= sh, pltpu.roll(b, sh, 0), 0.0)
            sh *= 2
        scores = jnp.where(owner < 0, _dot_nt(q.astype(BF16), k.astype(BF16)), 0.0)
        first = b
        for lvl, (upper, sign) in enumerate(zip(uppers, signs)):
            half = 1 << lvl
            mid = jnp.where(upper, first, pltpu.roll(first, c - half, 0))
            scaled = jnp.where(upper, q, k) * jnp.exp2(jnp.minimum((b - mid) * sign, 0.0))
            ql = jnp.where(upper, scaled, 0.0).astype(BF16)
            kl = jnp.where(upper, 0.0, scaled).astype(BF16)
            scores = jnp.where(owner == lvl, _dot_nt(ql, kl), scores)
            first = jnp.where(upper, pltpu.roll(first, half, 0), first)
        st = st_ref[hd]
        o = _dot(scores.astype(BF16), v) + _dot_nt((q * jnp.exp(b)).astype(BF16), st.astype(BF16))
        b_last = b[c - 1:c, :]
        kd = (k * jnp.exp(b_last - b)).astype(BF16)
        st_ref[hd] = st * jnp.exp(b_last) + _dot_tn(v, kd)
        on = o * lax.rsqrt(jnp.mean(o * o, axis=-1, keepdims=True) + EPS)
        gate = g_ref[0, :, hd * dv:(hd + 1) * dv].astype(F32)
        o_ref[0, :, hd * dv:(hd + 1) * dv] = (gate * (on * w_ref[:, hd * dv:(hd + 1) * dv])).astype(o_ref.dtype)


def _mixers_kernel(rq_ref, rk_ref, rv_ref, rg_ref, rs0_ref, dm_ref, ind_ref, sd_ref, cd_ref, rw_ref,
                   gq_ref, gk_ref, la_ref, gv_ref, gg_ref, gs0_ref, gw_ref,
                   ro_ref, rso_ref, go_ref, gso_ref, rst_ref, gst_ref, *, ret, gla, c):
    ci = pl.program_id(1)

    @pl.when(ci == 0)
    def _():
        rst_ref[...] = rs0_ref[0]
        gst_ref[...] = gs0_ref[0]

    rh, rdk, rdv = ret
    gh, gdk, gdv = gla
    _ret_heads(rq_ref, rk_ref, rv_ref, rg_ref, dm_ref, ind_ref, sd_ref, cd_ref, rw_ref, ro_ref, rst_ref,
               heads=rh, dk=rdk, dv=rdv)
    _gla_heads(gq_ref, gk_ref, la_ref, gv_ref, gg_ref, gw_ref, go_ref, gst_ref, heads=gh, dk=gdk, dv=gdv, c=c)

    @pl.when(ci == pl.num_programs(1) - 1)
    def _():
        rso_ref[0] = rst_ref[...]
        gso_ref[0] = gst_ref[...]


def _mixers(qkvg, qk, vg, la, s0_ret, s0_gla_t, ret_norm_w, gla_norm_w, *, ret, gla):
    s, t, _ = qkvg.shape
    rh, rdk, rdv = ret
    gh, gdk, gdv = gla
    assert rdk == rdv
    c = _tile(t, CHUNK_MIXERS, 2 * SUBLANES)
    assert c & (c - 1) == 0, "chunk length must be a power of two"
    rw = rh * rdk
    gwk, gwv = gh * gdk, gh * gdv
    dm, ind, sd, cd = _ret_tables(rh, c, rdk, rdv)
    const3 = lambda b, i: (0, 0, 0)
    const2 = lambda b, i: (0, 0)
    chunk = lambda col: (lambda b, i: (b, i, col))
    state = lambda b, i: (b, 0, 0, 0)
    return pl.pallas_call(
        functools.partial(_mixers_kernel, ret=ret, gla=gla, c=c),
        out_shape=(
            jax.ShapeDtypeStruct((s, t, rw), BF16), jax.ShapeDtypeStruct(s0_ret.shape, F32),
            jax.ShapeDtypeStruct((s, t, gwv), BF16), jax.ShapeDtypeStruct(s0_gla_t.shape, F32),
        ),
        grid=(s, t // c),
        in_specs=[
            pl.BlockSpec((1, c, rw), chunk(0)),
            pl.BlockSpec((1, c, rw), chunk(1)),
            pl.BlockSpec((1, c, rw), chunk(2)),
            pl.BlockSpec((1, c, rw), chunk(3)),
            pl.BlockSpec((1, rh, rdk, rdv), state),
            pl.BlockSpec((rh, c, c), const3),
            pl.BlockSpec((rh, c, rdv), const3),
            pl.BlockSpec((rh, c, rdk), const3),
            pl.BlockSpec((rh, 1, rdv), const3),
            pl.BlockSpec((1, rw), const2),
            pl.BlockSpec((1, c, gwk), chunk(0)),
            pl.BlockSpec((1, c, gwk), chunk(1)),
            pl.BlockSpec((1, c, gwk), chunk(0)),
            pl.BlockSpec((1, c, gwv), chunk(0)),
            pl.BlockSpec((1, c, gwv), chunk(1)),
            pl.BlockSpec((1, gh, gdv, gdk), state),
            pl.BlockSpec((1, gwv), const2),
        ],
        out_specs=(
            pl.BlockSpec((1, c, rw), chunk(0)),
            pl.BlockSpec((1, rh, rdk, rdv), state),
            pl.BlockSpec((1, c, gwv), chunk(0)),
            pl.BlockSpec((1, gh, gdv, gdk), state),
        ),
        scratch_shapes=[pltpu.VMEM((rh, rdk, rdv), F32), pltpu.VMEM((gh, gdv, gdk), F32)],
        compiler_params=_params("arbitrary", "arbitrary"),
        name="token_mixers",
    )(qkvg, qkvg, qkvg, qkvg, s0_ret, dm, ind, sd, cd, ret_norm_w.reshape(1, rw),
      qk, qk, la, vg, vg, s0_gla_t, gla_norm_w.reshape(1, gwv))


def _merge_kernel(br_ref, bg_ref, h_ref, wr_ref, wg_ref, wm0_ref, wm1_ref, bm0_ref, bm1_ref, o_ref):
    h = h_ref[...]
    g0 = _sigmoid(_dot(h, wm0_ref[...]) + bm0_ref[...])
    g1 = _sigmoid(_dot(h, wm1_ref[...]) + bm1_ref[...])
    y = g0 * _dot(br_ref[...], wr_ref[...]) + g1 * _dot(bg_ref[...], wg_ref[...])
    o_ref[...] = y.astype(o_ref.dtype)


def _merge(o_ret, o_gla, h2d, w_branch, w_merge, layer, b_merge, *, tm):
    n, d = h2d.shape
    wdt = o_ret.shape[1]
    tn = _tile(d, 512, LANES)
    nj = d // tn
    b2 = b_merge.reshape(1, 2 * d)
    return pl.pallas_call(
        _merge_kernel,
        out_shape=jax.ShapeDtypeStruct((n, d), BF16),
        grid=(n // tm, nj),
        in_specs=[
            pl.BlockSpec((tm, wdt), lambda i, j: (i, 0)),
            pl.BlockSpec((tm, wdt), lambda i, j: (i, 0)),
            pl.BlockSpec((tm, d), lambda i, j: (i, 0)),
            pl.BlockSpec((None, wdt, tn), lambda i, j: (2 * layer, 0, j)),
            pl.BlockSpec((None, wdt, tn), lambda i, j: (2 * layer + 1, 0, j)),
            pl.BlockSpec((None, d, tn), lambda i, j: (layer, 0, j)),
            pl.BlockSpec((None, d, tn), lambda i, j: (layer, 0, j + nj)),
            pl.BlockSpec((1, tn), lambda i, j: (0, j)),
            pl.BlockSpec((1, tn), lambda i, j: (0, j + nj)),
        ],
        out_specs=pl.BlockSpec((tm, tn), lambda i, j: (i, j)),
        compiler_params=_params("arbitrary", "arbitrary"),
        name="branch_merge",
    )(o_ret, o_gla, h2d, w_branch, w_branch, w_merge, w_merge, b2, b2)


def _pack_bf16_pairs(h):
    half = h.shape[-1] // 2
    a = lax.bitcast_convert_type(h[:, :half].astype(BF16).astype(F32), U32)
    b = lax.bitcast_convert_type(h[:, half:].astype(BF16).astype(F32), U32)
    return a | (b >> 16)


def _unpack_bf16_pairs(w, dtype):
    a = lax.bitcast_convert_type(w & jnp.uint32(0xFFFF0000), F32)
    b = lax.bitcast_convert_type(w << 16, F32)
    return jnp.concatenate([a, b], axis=-1).astype(dtype)


def _route(logits, n_groups, per_group):
    lane = lax.broadcasted_iota(I32, logits.shape, 1).astype(F32)
    neg = jnp.float32(-jnp.inf)

    def first_max(mask):
        m = jnp.max(jnp.where(mask, logits, neg), axis=-1, keepdims=True)
        idx = jnp.min(jnp.where(mask & (logits == m), lane, float(LANES)), axis=-1, keepdims=True)
        return m, idx

    gmask = lane < n_groups
    gmax, gidx = first_max(gmask)
    p_group = 1.0 / jnp.sum(jnp.where(gmask, jnp.exp(logits - gmax), 0.0), axis=-1, keepdims=True)
    lo = n_groups + gidx * per_group
    emask = (lane >= lo) & (lane < lo + per_group)
    m1, i1 = first_max(emask)
    m2, i2 = first_max(emask & (lane != i1))
    w1 = 1.0 / (1.0 + jnp.exp(m2 - m1))
    w2 = 1.0 - w1
    return jnp.where(lane == 0.0, i1 - n_groups, jnp.where(lane == 1.0, i2 - n_groups, jnp.where(
        lane == 2.0, p_group * w1, jnp.where(lane == 3.0, p_group * w2, 0.0))))


def _with_ranks(rt, carry_ref):
    tr = rt.shape[0]
    lane = lax.broadcasted_iota(I32, rt.shape, 1).astype(F32)
    a1 = lane == rt[:, 0:1]
    a2 = lane == rt[:, 1:2]
    hit = jnp.where(a1 | a2, 1.0, 0.0)
    ri = lax.broadcasted_iota(I32, (tr, tr), 0)
    cj = lax.broadcasted_iota(I32, (tr, tr), 1)
    before = _dot(jnp.where(ri > cj, 1.0, 0.0).astype(BF16), hit.astype(BF16)) + carry_ref[...]
    k1 = jnp.sum(jnp.where(a1, before, 0.0), axis=-1, keepdims=True)
    k2 = jnp.sum(jnp.where(a2, before, 0.0), axis=-1, keepdims=True)
    carry_ref[...] += jnp.sum(hit, axis=0, keepdims=True)
    return jnp.where(lane == 4.0, k1, jnp.where(lane == 5.0, k2, rt))


def _out_proj_kernel(m_ref, x_ref, g1_ref, sc_ref, sh_ref, nw_ref, wo_ref, wr_ref, br_ref,
                     x1_ref, hp_ref, rt_ref, cnt_ref, carry_ref, *, n_groups, per_group):
    bs, tt, d = x_ref.shape

    @pl.when((pl.program_id(0) == 0) & (pl.program_id(1) == 0))
    def _():
        carry_ref[...] = jnp.zeros_like(carry_ref)

    y = _dot(m_ref[...].reshape(bs * tt, d), wo_ref[...]).reshape(bs, tt, d)
    x1 = x_ref[...] + g1_ref[...] * y
    x1_ref[...] = x1
    hn = x1 * lax.rsqrt(jnp.mean(x1 * x1, axis=-1, keepdims=True) + EPS) * nw_ref[...]
    h = (hn * (1.0 + sc_ref[...]) + sh_ref[...]).reshape(bs * tt, d)
    _store_token_major(hp_ref, _pack_bf16_pairs(h))
    h_hi, h_lo = _split_bf16(h)
    p = _dot(h_hi, wr_ref[...])
    logits = p[:, :LANES] + p[:, LANES:] + _dot(h_lo, wr_ref[:, :LANES]) + br_ref[...]
    rt = _with_ranks(_route(logits, n_groups, per_group), carry_ref)
    rt_ref[...] = rt.reshape(bs, tt, LANES)
    cnt_ref[...] = carry_ref[...]


def _out_proj(merged, x, g1, sc2, sh2, norm_w, w_o, layer, wr, b_r, *, n_groups, per_group):
    s, t, d = x.shape
    bs, tt = _seq_tiles(s, t, ROWS_OUT_PROJ)
    chunks = d // 2 // LANES
    tok = lambda i, j: (i, j, 0)
    seq = lambda i, j: (i, 0, 0)
    const = lambda i, j: (0, 0)
    return pl.pallas_call(
        functools.partial(_out_proj_kernel, n_groups=n_groups, per_group=per_group),
        out_shape=(
            jax.ShapeDtypeStruct((s, t, d), F32),
            jax.ShapeDtypeStruct((s * t * chunks, LANES), U32),
            jax.ShapeDtypeStruct((s, t, LANES), F32),
            jax.ShapeDtypeStruct((1, LANES), F32),
        ),
        grid=(s // bs, t // tt),
        in_specs=[
            pl.BlockSpec((bs, tt, d), tok),
            pl.BlockSpec((bs, tt, d), tok),
            pl.BlockSpec((bs, 1, d), seq),
            pl.BlockSpec((bs, 1, d), seq),
            pl.BlockSpec((bs, 1, d), seq),
            pl.BlockSpec((1, d), const),
            pl.BlockSpec((None, d, d), lambda i, j: (layer, 0, 0), pipeline_mode=pl.Buffered(1)),
            pl.BlockSpec((d, 2 * LANES), const, pipeline_mode=pl.Buffered(1)),
            pl.BlockSpec((1, LANES), const),
        ],
        out_specs=(
            pl.BlockSpec((bs, tt, d), tok),
            pl.BlockSpec((bs * tt * chunks, LANES), lambda i, j: (i * (t // tt) + j, 0)),
            pl.BlockSpec((bs, tt, LANES), tok),
            pl.BlockSpec((1, LANES), const),
        ),
        scratch_shapes=[pltpu.VMEM((1, LANES), F32)],
        compiler_params=_params("arbitrary", "arbitrary"),
        name="out_proj_router",
    )(merged, x, g1, sc2, sh2, norm_w.reshape(1, d), w_o, wr, b_r)


def _invert_kernel(pos_ref, pad_lo_ref, pad_hi_ref, src_ref, *, n_tokens, n_exp):
    def clear(j, carry):
        src_ref[j] = 0
        return carry

    for e in range(n_exp):
        lax.fori_loop(pad_lo_ref[e], pad_hi_ref[e], clear, 0)

    def put(t, carry):
        for kk in range(TOP_K):
            src_ref[pos_ref[TOP_K * t + kk]] = t
        return carry

    lax.fori_loop(0, n_tokens, put, 0, unroll=INVERT_UNROLL)


def _invert(pos, pad_lo, pad_hi, n_rows):
    n_tokens = pos.shape[0] // TOP_K
    return pl.pallas_call(
        functools.partial(_invert_kernel, n_tokens=n_tokens, n_exp=pad_lo.shape[0]),
        out_shape=jax.ShapeDtypeStruct((n_rows,), I32),
        grid_spec=pltpu.PrefetchScalarGridSpec(
            num_scalar_prefetch=3,
            grid=(1,),
            in_specs=[],
            out_specs=pl.BlockSpec(memory_space=pltpu.SMEM),
        ),
        compiler_params=_params("arbitrary"),
        name="moe_invert",
    )(pos, pad_lo, pad_hi)


def _store_token_major(ref, value):
    chunks = value.shape[1] // LANES
    for c in range(chunks):
        ref[pl.ds(c, value.shape[0], stride=chunks), :] = value[:, c * LANES:(c + 1) * LANES]


def _load_token_major(ref, rows, chunks):
    return jnp.concatenate([ref[pl.ds(c, rows, stride=chunks), :] for c in range(chunks)], axis=-1)


def _token_copy(src_ref, src_tok, dst_ref, dst_tok, chunks, sem):
    def at(ref, tok):
        start = tok * chunks
        if not isinstance(start, int):
            start = pl.multiple_of(start, chunks)
        return ref.at[pl.ds(start, chunks)]

    return pltpu.make_async_copy(at(src_ref, src_tok), at(dst_ref, dst_tok), sem)


def _ffn_kernel(te_ref, valid_ref, src_ref, hp_ref, wg_ref, wu_ref, wd_ref, ys_ref, xbuf, sems, *, tf, chunks):
    del te_ref
    i = pl.program_id(0)
    last = pl.num_programs(0) - 1
    slot = i % 2
    nxt = jnp.where(i < last, i + 1, 0)

    def start(tile, slt, r):
        _token_copy(hp_ref, src_ref[tile * tf + r], xbuf.at[slt], r, chunks, sems.at[slt]).start()

    def drain(slt):
        def wait(r, carry):
            _token_copy(hp_ref, 0, xbuf.at[slt], 0, chunks, sems.at[slt]).wait()
            return carry

        lax.fori_loop(0, tf, wait, 0, unroll=GATHER_UNROLL)

    @pl.when(i == 0)
    def _():
        def issue(r, carry):
            start(0, 0, r)
            return carry

        lax.fori_loop(0, tf, issue, 0, unroll=GATHER_UNROLL)

    def gather_next(slot):
        for r in range(tf):
            start(nxt, 1 - slot, r)

    def step(slot, fused_issue):
        if not fused_issue:
            gather_next(slot)
        drain(slot)

        @pl.when(valid_ref[i] != 0)
        def _():
            x = _unpack_bf16_pairs(_load_token_major(xbuf.at[slot], tf, chunks), BF16)
            if fused_issue:
                gather_next(slot)
            act = (_silu(_dot(x, wg_ref[...])) * _dot(x, wu_ref[...])).astype(BF16)
            _store_token_major(ys_ref, _pack_bf16_pairs(_dot(act, wd_ref[...])))

        @pl.when(valid_ref[i] == 0)
        def _():
            if fused_issue:
                gather_next(slot)
            ys_ref[...] = jnp.zeros_like(ys_ref)

        @pl.when(i == last)
        def _():
            drain(1 - slot)

    pl.when(slot == 0)(functools.partial(step, 0, True))
    pl.when(slot == 1)(functools.partial(step, 1, False))


def _ffn(tile_expert, tile_valid, src, hp, wg, wu, wd, *, tf):
    n_rows = src.shape[0]
    _, d, f = wg.shape
    chunks = d // 2 // LANES
    return pl.pallas_call(
        functools.partial(_ffn_kernel, tf=tf, chunks=chunks),
        out_shape=jax.ShapeDtypeStruct((n_rows * chunks, LANES), U32),
        grid_spec=pltpu.PrefetchScalarGridSpec(
            num_scalar_prefetch=3,
            grid=(n_rows // tf,),
            in_specs=[
                pl.BlockSpec(memory_space=pl.ANY),
                pl.BlockSpec((None, d, f), lambda i, te, tv, sr: (te[i], 0, 0)),
                pl.BlockSpec((None, d, f), lambda i, te, tv, sr: (te[i], 0, 0)),
                pl.BlockSpec((None, f, d), lambda i, te, tv, sr: (te[i], 0, 0)),
            ],
            out_specs=pl.BlockSpec((tf * chunks, LANES), lambda i, te, tv, sr: (i, 0)),
            scratch_shapes=[pltpu.VMEM((2, tf * chunks, LANES), hp.dtype), pltpu.SemaphoreType.DMA((2,))],
        ),
        compiler_params=_params("arbitrary"),
        name="moe_ffn",
    )(tile_expert, tile_valid, src, hp, wg, wu, wd)


def _combine_kernel(pos_ref, x_ref, rt_ref, g2_ref, nw_ref, sc_ref, sh_ref, ys_ref, *rest, last):
    if last:
        o_ref, ybuf, sems = rest
    else:
        o_ref, h_ref, ybuf, sems = rest
    bs, tt, d = x_ref.shape
    rows = bs * tt
    chunks = d // 2 // LANES
    step = pl.program_id(0) * pl.num_programs(1) + pl.program_id(1)
    n_steps = pl.num_programs(0) * pl.num_programs(1)

    def start(stp, slt, r, kk):
        _token_copy(ys_ref, pos_ref[TOP_K * (stp * rows + r) + kk], ybuf.at[slt, kk], r, chunks,
                    sems.at[slt]).start()

    @pl.when(step == 0)
    def _():
        def issue(r, carry):
            for kk in range(TOP_K):
                start(0, 0, r, kk)
            return carry

        lax.fori_loop(0, rows, issue, 0, unroll=GATHER_UNROLL)

    slot = step % 2

    @pl.when(step + 1 < n_steps)
    def _():
        for r in range(rows):
            for kk in range(TOP_K):
                start(step + 1, 1 - slot, r, kk)

    def wait(r, carry):
        for kk in range(TOP_K):
            _token_copy(ys_ref, 0, ybuf.at[slot, kk], 0, chunks, sems.at[slot]).wait()
        return carry

    lax.fori_loop(0, rows, wait, 0, unroll=GATHER_UNROLL)

    rt = rt_ref[...].reshape(rows, LANES)
    y = (rt[:, 2:3] * _unpack_bf16_pairs(_load_token_major(ybuf.at[slot, 0], rows, chunks), F32)
         + rt[:, 3:4] * _unpack_bf16_pairs(_load_token_major(ybuf.at[slot, 1], rows, chunks), F32))
    xn = x_ref[...] + g2_ref[...] * y.reshape(bs, tt, d)
    normed = xn * lax.rsqrt(jnp.mean(xn * xn, axis=-1, keepdims=True) + EPS) * nw_ref[...]
    if last:
        o_ref[...] = normed
    else:
        o_ref[...] = xn
        h_ref[...] = (normed * (1.0 + sc_ref[...]) + sh_ref[...]).astype(h_ref.dtype)


def _combine(pos, x1, route, g2, norm_w, sc, sh, ys, *, last):
    s, t, d = x1.shape
    bs, tt = _seq_tiles(s, t, ROWS_COMBINE)
    tok = lambda i, j, p: (i, j, 0)
    seq = lambda i, j, p: (i, 0, 0)
    x_shape = jax.ShapeDtypeStruct((s, t, d), F32)
    x_spec = pl.BlockSpec((bs, tt, d), tok)
    return pl.pallas_call(
        functools.partial(_combine_kernel, last=last),
        out_shape=x_shape if last else (x_shape, jax.ShapeDtypeStruct((s, t, d), BF16)),
        grid_spec=pltpu.PrefetchScalarGridSpec(
            num_scalar_prefetch=1,
            grid=(s // bs, t // tt),
            in_specs=[
                x_spec,
                pl.BlockSpec((bs, tt, LANES), tok),
                pl.BlockSpec((bs, 1, d), seq),
                pl.BlockSpec((1, d), lambda i, j, p: (0, 0)),
                pl.BlockSpec((bs, 1, d), seq),
                pl.BlockSpec((bs, 1, d), seq),
                pl.BlockSpec(memory_space=pl.ANY),
            ],
            out_specs=x_spec if last else (x_spec, x_spec),
            scratch_shapes=[pltpu.VMEM((2, TOP_K, bs * tt * (d // 2 // LANES), LANES), U32),
                            pltpu.SemaphoreType.DMA((2,))],
        ),
        compiler_params=_params("arbitrary", "arbitrary"),
        name="moe_combine",
    )(pos, x1, route, g2, norm_w.reshape(1, d), sc, sh, ys)


def _moe(x1, hp, route, counts, g2, wg, wu, wd, layer, n_exp, norm_w, sc, sh, *, last):
    s, t, d = x1.shape
    n = s * t
    route2d = route.reshape(n, LANES)
    tf = 2 * SUBLANES
    while tf < ROWS_FFN and tf < 2 * TOP_K * n // n_exp:
        tf *= 2
    n_tiles = (TOP_K * n) // tf + n_exp
    cnt = counts[0, :n_exp].astype(I32)
    padded = ((cnt + tf - 1) // tf) * tf
    ends = jnp.cumsum(padded)
    starts = ends - padded
    expert = route2d[:, :TOP_K].astype(I32)
    rank = route2d[:, 4:4 + TOP_K].astype(I32)
    start_of = jnp.sum(jnp.where(expert[..., None] == jnp.arange(n_exp, dtype=I32), starts, 0), axis=-1)
    pos = (start_of + rank).reshape(TOP_K * n)
    tile_start = jnp.arange(n_tiles, dtype=I32) * tf
    tile_valid = (tile_start < ends[-1]).astype(I32)
    tile_expert = jnp.minimum(jnp.sum((ends[None, :] <= tile_start[:, None]).astype(I32), axis=1), n_exp - 1)
    last_used = jnp.max(jnp.where(tile_valid != 0, tile_expert, 0))
    tile_expert = jnp.where(tile_valid != 0, tile_expert, last_used)
    pad_hi = jnp.where(jnp.arange(n_exp) == n_exp - 1, n_tiles * tf, ends).astype(I32)
    src = _invert(pos, starts + cnt, pad_hi, n_tiles * tf)
    ys = _ffn(tile_expert + layer * n_exp, tile_valid, src, hp, wg, wu, wd, tf=tf)
    return _combine(pos, x1, route, g2, norm_w, sc, sh, ys, last=last)


def _rotary_tables(pos0, t, dk, reps):
    half = dk // 2
    inv = ROPE_BASE ** (-jnp.arange(half, dtype=F32) / half)
    ang = (pos0 + jnp.arange(t)).astype(F32)[:, None] * inv[None, :]
    cos = jnp.cos(ang)
    sin = jnp.sin(ang)
    cos2 = jnp.concatenate([cos, cos], axis=-1)
    sin2 = jnp.concatenate([-sin, sin], axis=-1)
    return jnp.tile(cos2, (reps, 1)), jnp.tile(sin2, (reps, 1))


def _trunk(x, mods, pos0, s_ret_in, s_gla_in_t, big, wts, dims):
    s, t, d = x.shape
    n = s * t
    depth = len(wts)
    rh, rdk, rdv, gh, gdk, gdv = dims["ret_heads"], dims["ret_dk"], dims["ret_dv"], dims["gla_heads"], dims["gla_dk"], dims["gla_dv"]
    bs, tt = _seq_tiles(s, t, ROWS_PROJ)
    tp = bs * tt
    tm = _tile(n, ROWS_MATMUL, 2 * SUBLANES)
    cos, sin = _rotary_tables(pos0, t, rdk, bs)
    ret_states, gla_states = [], []
    h = _norm(x, wts[0]["norm_mix"], mods[1][0], mods[0][0])
    for l in range(depth):
        w = wts[l]
        sh1, sc1, g1, sh2, sc2, g2 = (m[l] for m in mods)
        h2d = h.reshape(n, d)
        qkvg = _proj_ret(h2d, big["w_ret"], l, cos, sin, heads=rh, dk=rdk, tm=tp)
        qk, vg, la = _proj_gla(h2d, big["w_gla"], big["w_code"], l, w["wup_hi"], w["wup_lo"], w["b_gla"],
                               hdk=gh * gdk, hdv=gh * gdv, dk=gdk, tm=tp)
        o_ret, s_r, o_gla, s_g = _mixers(qkvg.reshape(s, t, -1), qk.reshape(s, t, -1), vg.reshape(s, t, -1),
                                         la.reshape(s, t, -1), s_ret_in[l], s_gla_in_t[l], w["ret_norm"],
                                         w["gla_norm"], ret=(rh, rdk, rdv), gla=(gh, gdk, gdv))
        merged = _merge(o_ret.reshape(n, -1), o_gla.reshape(n, -1), h2d, big["w_branch"], big["w_merge"], l,
                        w["b_merge"], tm=tm)
        x1, hp, route, counts = _out_proj(merged.reshape(s, t, d), x, g1, sc2, sh2, w["norm_ffn"], big["w_o"], l,
                                  w["w_router"], w["b_r"], n_groups=dims["n_groups"], per_group=dims["per_group"])
        n_exp = dims["n_groups"] * dims["per_group"]
        experts = (big["w_exp_gate"], big["w_exp_up"], big["w_exp_down"], l, n_exp)
        if l == depth - 1:
            x = _moe(x1, hp, route, counts, g2, *experts, w["final_norm"], sc1, sh1, last=True)
        else:
            x, h = _moe(x1, hp, route, counts, g2, *experts, wts[l + 1]["norm_mix"], mods[1][l + 1], mods[0][l + 1], last=False)
        ret_states.append(s_r)
        gla_states.append(jnp.swapaxes(s_g, -1, -2))
    return x, jnp.stack(ret_states), jnp.stack(gla_states)


def kernel(x_prompt, x_sample, state_ret, state_gla, c_prompt, c_sample, w_ada, b_ada, norm_mix_w, norm_ffn_w, w_in, w_gla_up, b_gla, ret_norm_w, gla_norm_w, w_branch, w_merge, b_merge, w_o, w_router_group, b_router_group, w_router_expert, b_router_expert, w_exp_gate, w_exp_up, w_exp_down, final_norm_w):
    depth, d, _ = w_in.shape
    _, _, rh, rdk, rdv = state_ret.shape
    _, _, gh, gdk, gdv = state_gla.shape
    low_rank = w_gla_up.shape[1]
    n_groups = w_router_group.shape[-1]
    n_exp = w_router_expert.shape[-1]
    assert low_rank <= LANES and n_groups + n_exp <= LANES and n_exp % n_groups == 0
    dims = dict(ret_heads=rh, ret_dk=rdk, ret_dv=rdv, gla_heads=gh, gla_dk=gdk, gla_dv=gdv,
                n_groups=n_groups, per_group=n_exp // n_groups)

    n_ret = 2 * rh * rdk + 2 * rh * rdv
    n_gla = 2 * gh * gdk + 2 * gh * gdv
    assert w_in.shape[-1] == n_ret + n_gla + low_rank
    f = w_exp_gate.shape[-1]
    big = dict(
        w_ret=_to_bf16(w_in, 0, n_ret),
        w_gla=_to_bf16(w_in, n_ret, n_gla),
        w_code=_to_bf16(w_in, n_ret + n_gla, LANES, valid_cols=low_rank),
        w_branch=_to_bf16(w_branch.reshape(depth, -1, d)).reshape(depth * w_branch.shape[1], -1, d),
        w_merge=_to_bf16(w_merge),
        w_o=_to_bf16(w_o),
        w_exp_gate=_to_bf16(w_exp_gate.reshape(depth, n_exp * d, f)).reshape(depth * n_exp, d, f),
        w_exp_up=_to_bf16(w_exp_up.reshape(depth, n_exp * d, f)).reshape(depth * n_exp, d, f),
        w_exp_down=_to_bf16(w_exp_down.reshape(depth, n_exp * f, d)).reshape(depth * n_exp, f, d),
    )
    w_router = jnp.concatenate([w_router_group, w_router_expert], axis=-1)
    w_router = jnp.pad(w_router, ((0, 0), (0, 0), (0, LANES - n_groups - n_exp)))
    b_router = jnp.pad(jnp.concatenate([b_router_group, b_router_expert], axis=-1), ((0, 0), (0, LANES - n_groups - n_exp)))
    wup = jnp.pad(w_gla_up, ((0, 0), (0, LANES - low_rank), (0, 0)))
    wts = []
    for l in range(depth):
        wr_hi, wr_lo = _split_bf16(w_router[l])
        wup_hi, wup_lo = _split_bf16(wup[l])
        wts.append(dict(
            norm_mix=norm_mix_w[l], norm_ffn=norm_ffn_w[l],
            wup_hi=wup_hi, wup_lo=wup_lo, b_gla=b_gla[l].reshape(1, -1),
            ret_norm=ret_norm_w[l], gla_norm=gla_norm_w[l], b_merge=b_merge[l],
            w_router=jnp.concatenate([wr_hi, wr_lo], axis=-1), b_r=b_router[l].reshape(1, LANES),
            final_norm=final_norm_w,
        ))

    nb, ns = c_prompt.shape[0], c_sample.shape[0]
    mod = _ada(jnp.concatenate([c_prompt, c_sample], axis=0), w_ada, b_ada)
    mod = mod.reshape(depth, nb + ns, 6, d)
    mods_p = [mod[:, :nb, i][:, :, None, :] for i in range(6)]
    mods_s = [mod[:, nb:, i][:, :, None, :] for i in range(6)]

    zero_ret = jnp.zeros((depth, nb, rh, rdk, rdv), F32)
    zero_gla_t = jnp.zeros((depth, nb, gh, gdv, gdk), F32)
    y_p, ret_p, gla_p = _trunk(x_prompt, mods_p, 0, zero_ret, zero_gla_t, big, wts, dims)
    y_s, ret_s, gla_s = _trunk(x_sample, mods_s, PAST_LEN, state_ret.astype(F32),
                               jnp.swapaxes(state_gla.astype(F32), -1, -2), big, wts, dims)
    return (y_p, y_s, ret_p, gla_p, ret_s, gla_s)
```

```python
import functools

import jax
import jax.numpy as jnp
from jax import lax
from jax.experimental import pallas as pl
from jax.experimental.pallas import tpu as pltpu

F32 = jnp.float32
BF16 = jnp.bfloat16
U32 = jnp.uint32
I32 = jnp.int32

PAST_LEN = 4096
GLA_TAU = 16.0
ROPE_BASE = 10000.0
EPS = 1e-6
RET_DECAY_LOG2_BASE = -5.0
TOP_K = 2
LOG2_E = 1.4426950408889634

LANES = 128
SUBLANES = 8
V7X_VMEM_BYTES = 64 * 1024 * 1024
VMEM_LIMIT = V7X_VMEM_BYTES - 8 * 1024 * 1024

ROWS_MATMUL = 1024
ROWS_PROJ = 512
ROWS_OUT_PROJ = 512
ROWS_NORM = 512
ROWS_COMBINE = 128
GATHER_UNROLL = 8
INVERT_UNROLL = 16
CAST_BLOCK_COLS = 1024
CAST_BLOCK_ELEMS = 2 * 1024 * 1024
ROWS_FFN = 256
CHUNK_MIXERS = 256


def _params(*sem):
    return pltpu.CompilerParams(dimension_semantics=sem, vmem_limit_bytes=VMEM_LIMIT)


def _tile(n, pref, align):
    if n <= pref:
        return n
    t = (pref // align) * align
    while t >= align:
        if n % t == 0:
            return t
        t -= align
    return n


def _seq_tiles(s, t, rows):
    if t >= rows:
        return 1, _tile(t, rows, 2 * SUBLANES)
    return _tile(s, max(rows // t, 1), 1), t


def _dot(a, b):
    return jnp.dot(a, b, preferred_element_type=F32)


def _dot_nt(a, b):
    return lax.dot_general(a, b, (((1,), (1,)), ((), ())), preferred_element_type=F32)


def _dot_tn(a, b):
    return lax.dot_general(a, b, (((0,), (0,)), ((), ())), preferred_element_type=F32)


def _sigmoid(x):
    return 0.5 * jnp.tanh(0.5 * x) + 0.5


def _silu(x):
    return x * _sigmoid(x)


def _log_sigmoid(z):
    return jnp.minimum(z, 0.0) - jnp.log(1.0 + jnp.exp(-jnp.abs(z)))


def _split_bf16(x):
    hi = x.astype(BF16)
    lo = (x - hi.astype(F32)).astype(BF16)
    return hi, lo


def _dot_split(a, w_hi, w_lo):
    a_hi, a_lo = _split_bf16(a)
    return _dot(a_hi, w_hi) + _dot(a_lo, w_hi) + _dot(a_hi, w_lo)


def _ada_kernel(c_ref, w_ref, b_ref, o_ref):
    a = _silu(c_ref[...]).astype(BF16)
    o_ref[...] = _dot(a, w_ref[...].astype(BF16)) + b_ref[...]


def _ada(c_all, w_ada, b_ada):
    depth, d, n6 = w_ada.shape
    r = c_all.shape[0]
    tn = _tile(n6, 1024, LANES)
    return pl.pallas_call(
        _ada_kernel,
        out_shape=jax.ShapeDtypeStruct((depth, r, n6), F32),
        grid=(depth, n6 // tn),
        in_specs=[
            pl.BlockSpec((r, d), lambda l, j: (0, 0)),
            pl.BlockSpec((None, d, tn), lambda l, j: (l, 0, j)),
            pl.BlockSpec((None, 1, tn), lambda l, j: (l, 0, j)),
        ],
        out_specs=pl.BlockSpec((None, r, tn), lambda l, j: (l, 0, j)),
        compiler_params=_params("arbitrary", "arbitrary"),
        name="ada_mod",
    )(c_all, w_ada, b_ada.reshape(depth, 1, n6))


def _norm_kernel(x_ref, w_ref, sc_ref, sh_ref, o_ref):
    x = x_ref[...]
    y = x * lax.rsqrt(jnp.mean(x * x, axis=-1, keepdims=True) + EPS) * w_ref[...]
    o_ref[...] = (y * (1.0 + sc_ref[...]) + sh_ref[...]).astype(o_ref.dtype)


def _norm(x, w, sc, sh):
    s, t, d = x.shape
    bs, tt = _seq_tiles(s, t, ROWS_NORM)
    return pl.pallas_call(
        _norm_kernel,
        out_shape=jax.ShapeDtypeStruct((s, t, d), BF16),
        grid=(s // bs, t // tt),
        in_specs=[
            pl.BlockSpec((bs, tt, d), lambda i, j: (i, j, 0)),
            pl.BlockSpec((1, d), lambda i, j: (0, 0)),
            pl.BlockSpec((bs, 1, d), lambda i, j: (i, 0, 0)),
            pl.BlockSpec((bs, 1, d), lambda i, j: (i, 0, 0)),
        ],
        out_specs=pl.BlockSpec((bs, tt, d), lambda i, j: (i, j, 0)),
        compiler_params=_params("arbitrary", "arbitrary"),
        name="mod_norm",
    )(x, w.reshape(1, d), sc, sh)


def _resident(shape, layer=None):
    if layer is None:
        return pl.BlockSpec(shape, lambda i: (0,) * len(shape), pipeline_mode=pl.Buffered(1))
    return pl.BlockSpec((None,) + tuple(shape), lambda i: (layer,) + (0,) * len(shape), pipeline_mode=pl.Buffered(1))


def _cast_kernel(x_ref, o_ref, *, valid_cols):
    x = x_ref[...]
    if valid_cols is not None:
        x = jnp.where(lax.broadcasted_iota(I32, x.shape, 1) < valid_cols, x, 0.0)
    o_ref[...] = x.astype(o_ref.dtype)


def _to_bf16(w, col0=0, ncols=None, valid_cols=None):
    depth, rows, cols = w.shape
    ncols = cols - col0 if ncols is None else ncols
    tc = LANES
    while tc * 2 <= CAST_BLOCK_COLS and col0 % (tc * 2) == 0 and ncols % (tc * 2) == 0:
        tc *= 2
    assert col0 % tc == 0 and ncols % tc == 0 and (valid_cols is None or ncols == tc)
    tr = _tile(rows, max(CAST_BLOCK_ELEMS // tc, 2 * SUBLANES), 2 * SUBLANES)
    c0 = col0 // tc
    return pl.pallas_call(
        functools.partial(_cast_kernel, valid_cols=valid_cols),
        out_shape=jax.ShapeDtypeStruct((depth, rows, ncols), BF16),
        grid=(depth, rows // tr, ncols // tc),
        in_specs=[pl.BlockSpec((None, tr, tc), lambda l, i, j: (l, i, c0 + j))],
        out_specs=pl.BlockSpec((None, tr, tc), lambda l, i, j: (l, i, j)),
        compiler_params=_params("arbitrary", "arbitrary", "arbitrary"),
        name="cast_bf16",
    )(w)


def _proj_ret_kernel(h_ref, w_ref, cos_ref, sin_ref, o_ref, *, heads, dk, kscale):
    h = h_ref[...]
    tn = heads * dk
    cos = cos_ref[...]
    sin = sin_ref[...]
    for sec, mult in ((0, 1.0), (1, kscale)):
        acc = _dot(h, w_ref[:, sec * tn:(sec + 1) * tn])
        for hd in range(heads):
            a = acc[:, hd * dk:(hd + 1) * dk]
            r = a * cos + pltpu.roll(a, dk // 2, 1) * sin
            o_ref[:, sec * tn + hd * dk:sec * tn + (hd + 1) * dk] = (r * mult).astype(o_ref.dtype)
    o_ref[:, 2 * tn:3 * tn] = _dot(h, w_ref[:, 2 * tn:3 * tn]).astype(o_ref.dtype)
    o_ref[:, 3 * tn:] = _silu(_dot(h, w_ref[:, 3 * tn:])).astype(o_ref.dtype)


def _proj_ret(h2d, w, layer, cos, sin, *, heads, dk, tm):
    n, d = h2d.shape
    tn = heads * dk
    assert w.shape[1:] == (d, 4 * tn) and cos.shape[0] % tm == 0
    nt = cos.shape[0] // tm
    return pl.pallas_call(
        functools.partial(_proj_ret_kernel, heads=heads, dk=dk, kscale=dk ** -0.5),
        out_shape=jax.ShapeDtypeStruct((n, 4 * tn), BF16),
        grid=(n // tm,),
        in_specs=[
            pl.BlockSpec((tm, d), lambda i: (i, 0)),
            _resident((d, 4 * tn), layer),
            pl.BlockSpec((tm, dk), lambda i: (i % nt, 0)),
            pl.BlockSpec((tm, dk), lambda i: (i % nt, 0)),
        ],
        out_specs=pl.BlockSpec((tm, 4 * tn), lambda i: (i, 0)),
        compiler_params=_params("arbitrary"),
        name="proj_ret",
    )(h2d, w, cos, sin)


def _proj_gla_kernel(h_ref, w_ref, wc_ref, wup_hi_ref, wup_lo_ref, bup_ref, qk_ref, vg_ref, la_ref, *, hdk, hdv, qscale):
    h = h_ref[...]
    qk = _dot(h, w_ref[:, :2 * hdk])
    qk_ref[:, :hdk] = qk[:, :hdk] * qscale
    qk_ref[:, hdk:] = qk[:, hdk:]
    vg_ref[:, :hdv] = _dot(h, w_ref[:, 2 * hdk:2 * hdk + hdv]).astype(vg_ref.dtype)
    vg_ref[:, hdv:] = _silu(_dot(h, w_ref[:, 2 * hdk + hdv:])).astype(vg_ref.dtype)
    z = _dot_split(_dot(h, wc_ref[...]), wup_hi_ref[...], wup_lo_ref[...])
    la_ref[...] = _log_sigmoid(z + bup_ref[...]) * (1.0 / GLA_TAU)


def _proj_gla(h2d, w, wc, layer, wup_hi, wup_lo, bup, *, hdk, hdv, dk, tm):
    n, d = h2d.shape
    assert w.shape[1:] == (d, 2 * hdk + 2 * hdv) and wc.shape[1:] == (d, LANES) and wup_hi.shape == (LANES, hdk)
    return pl.pallas_call(
        functools.partial(_proj_gla_kernel, hdk=hdk, hdv=hdv, qscale=dk ** -0.5),
        out_shape=(
            jax.ShapeDtypeStruct((n, 2 * hdk), F32),
            jax.ShapeDtypeStruct((n, 2 * hdv), BF16),
            jax.ShapeDtypeStruct((n, hdk), F32),
        ),
        grid=(n // tm,),
        in_specs=[
            pl.BlockSpec((tm, d), lambda i: (i, 0)),
            _resident(w.shape[1:], layer),
            _resident(wc.shape[1:], layer),
            _resident(wup_hi.shape),
            _resident(wup_lo.shape),
            _resident(bup.shape),
        ],
        out_specs=(
            pl.BlockSpec((tm, 2 * hdk), lambda i: (i, 0)),
            pl.BlockSpec((tm, 2 * hdv), lambda i: (i, 0)),
            pl.BlockSpec((tm, hdk), lambda i: (i, 0)),
        ),
        compiler_params=_params("arbitrary"),
        name="proj_gla",
    )(h2d, w, wc, wup_hi, wup_lo, bup)


def _ret_heads(q_ref, k_ref, v_ref, g_ref, dm_ref, ind_ref, sd_ref, cd_ref, w_ref, o_ref, st_ref, *, heads, dk, dv):
    for hd in range(heads):
        q = q_ref[0, :, hd * dk:(hd + 1) * dk]
        k = k_ref[0, :, hd * dk:(hd + 1) * dk]
        v = v_ref[0, :, hd * dv:(hd + 1) * dv]
        s = st_ref[hd]
        p = (_dot_nt(q, k) * dm_ref[hd]).astype(BF16)
        o = _dot(p, v) + _dot(q, s.astype(BF16)) * ind_ref[hd]
        ks = (k.astype(F32) * sd_ref[hd]).astype(BF16)
        st_ref[hd] = s * cd_ref[hd] + _dot_tn(ks, v)
        oc = o - jnp.mean(o, axis=-1, keepdims=True)
        on = oc * lax.rsqrt(jnp.mean(oc * oc, axis=-1, keepdims=True) + EPS)
        gate = g_ref[0, :, hd * dv:(hd + 1) * dv].astype(F32)
        o_ref[0, :, hd * dv:(hd + 1) * dv] = (gate * (on * w_ref[:, hd * dv:(hd + 1) * dv])).astype(o_ref.dtype)


def _ret_tables(heads, c, dk, dv):
    log_gamma = jnp.log1p(-jnp.exp2(RET_DECAY_LOG2_BASE - jnp.arange(heads, dtype=F32)))
    idx = jnp.arange(c, dtype=F32)
    diff = idx[:, None] - idx[None, :]
    causal = diff >= 0
    dmask = jnp.where(causal[None], jnp.exp(log_gamma[:, None, None] * jnp.where(causal, diff, 0.0)[None]), 0.0)
    inner = jnp.exp(log_gamma[:, None] * (idx + 1.0))
    sdecay = jnp.exp(log_gamma[:, None] * (c - 1.0 - idx))
    cdecay = jnp.exp(log_gamma * c)
    return (dmask,
            jnp.broadcast_to(inner[:, :, None], (heads, c, dv)),
            jnp.broadcast_to(sdecay[:, :, None], (heads, c, dk)),
            jnp.broadcast_to(cdecay[:, None, None], (heads, 1, dv)))


def _gla_heads(q_ref, k_ref, la_ref, v_ref, g_ref, w_ref, o_ref, st_ref, *, heads, dk, dv, c):
    row = lax.broadcasted_iota(I32, (c, dk), 0)
    differ = lax.broadcasted_iota(I32, (c, c), 0) ^ lax.broadcasted_iota(I32, (c, c), 1)
    owner = jnp.full((c, c), -1, I32)
    uppers, signs = [], []
    half = 1
    while half < c:
        owner = owner + (differ >= half).astype(I32)
        up = (row & (2 * half - 1)) >= half
        uppers.append(up)
        signs.append(jnp.where(up, LOG2_E, -LOG2_E))
        half *= 2
    for hd in range(heads):
        q = q_ref[0, :, hd * dk:(hd + 1) * dk]
        k = k_ref[0, :, hd * dk:(hd + 1) * dk]
        v = v_ref[0, :, hd * dv:(hd + 1) * dv]
        b = la_ref[0, :, hd * dk:(hd + 1) * dk]
        sh = 1
        while sh < c:
            b = b + jnp.where(row >= sh, pltpu.roll(b, sh, 0), 0.0)
            sh *= 2
        scores = jnp.where(owner < 0, _dot_nt(q.astype(BF16), k.astype(BF16)), 0.0)
        first = b
        for lvl, (upper, sign) in enumerate(zip(uppers, signs)):
            half = 1 << lvl
            mid = jnp.where(upper, first, pltpu.roll(first, c - half, 0))
            scaled = jnp.where(upper, q, k) * jnp.exp2(jnp.minimum((b - mid) * sign, 0.0))
            ql = jnp.where(upper, scaled, 0.0).astype(BF16)
            kl = jnp.where(upper, 0.0, scaled).astype(BF16)
            scores = jnp.where(owner == lvl, _dot_nt(ql, kl), scores)
            first = jnp.where(upper, pltpu.roll(first, half, 0), first)
        st = st_ref[hd]
        o = _dot(scores.astype(BF16), v) + _dot_nt((q * jnp.exp(b)).astype(BF16), st.astype(BF16))
        b_last = b[c - 1:c, :]
        kd = (k * jnp.exp(b_last - b)).astype(BF16)
        st_ref[hd] = st * jnp.exp(b_last) + _dot_tn(v, kd)
        on = o * lax.rsqrt(jnp.mean(o * o, axis=-1, keepdims=True) + EPS)
        gate = g_ref[0, :, hd * dv:(hd + 1) * dv].astype(F32)
        o_ref[0, :, hd * dv:(hd + 1) * dv] = (gate * (on * w_ref[:, hd * dv:(hd + 1) * dv])).astype(o_ref.dtype)


def _mixers_kernel(rq_ref, rk_ref, rv_ref, rg_ref, rs0_ref, dm_ref, ind_ref, sd_ref, cd_ref, rw_ref,
                   gq_ref, gk_ref, la_ref, gv_ref, gg_ref, gs0_ref, gw_ref,
                   ro_ref, rso_ref, go_ref, gso_ref, rst_ref, gst_ref, *, ret, gla, c):
    ci = pl.program_id(1)

    @pl.when(ci == 0)
    def _():
        rst_ref[...] = rs0_ref[0]
        gst_ref[...] = gs0_ref[0]

    rh, rdk, rdv = ret
    gh, gdk, gdv = gla
    _ret_heads(rq_ref, rk_ref, rv_ref, rg_ref, dm_ref, ind_ref, sd_ref, cd_ref, rw_ref, ro_ref, rst_ref,
               heads=rh, dk=rdk, dv=rdv)
    _gla_heads(gq_ref, gk_ref, la_ref, gv_ref, gg_ref, gw_ref, go_ref, gst_ref, heads=gh, dk=gdk, dv=gdv, c=c)

    @pl.when(ci == pl.num_programs(1) - 1)
    def _():
        rso_ref[0] = rst_ref[...]
        gso_ref[0] = gst_ref[...]


def _mixers(qkvg, qk, vg, la, s0_ret, s0_gla_t, ret_norm_w, gla_norm_w, *, ret, gla):
    s, t, _ = qkvg.shape
    rh, rdk, rdv = ret
    gh, gdk, gdv = gla
    assert rdk == rdv
    c = _tile(t, CHUNK_MIXERS, 2 * SUBLANES)
    assert c & (c - 1) == 0, "chunk length must be a power of two"
    rw = rh * rdk
    gwk, gwv = gh * gdk, gh * gdv
    dm, ind, sd, cd = _ret_tables(rh, c, rdk, rdv)
    const3 = lambda b, i: (0, 0, 0)
    const2 = lambda b, i: (0, 0)
    chunk = lambda col: (lambda b, i: (b, i, col))
    state = lambda b, i: (b, 0, 0, 0)
    return pl.pallas_call(
        functools.partial(_mixers_kernel, ret=ret, gla=gla, c=c),
        out_shape=(
            jax.ShapeDtypeStruct((s, t, rw), BF16), jax.ShapeDtypeStruct(s0_ret.shape, F32),
            jax.ShapeDtypeStruct((s, t, gwv), BF16), jax.ShapeDtypeStruct(s0_gla_t.shape, F32),
        ),
        grid=(s, t // c),
        in_specs=[
            pl.BlockSpec((1, c, rw), chunk(0)),
            pl.BlockSpec((1, c, rw), chunk(1)),
            pl.BlockSpec((1, c, rw), chunk(2)),
            pl.BlockSpec((1, c, rw), chunk(3)),
            pl.BlockSpec((1, rh, rdk, rdv), state),
            pl.BlockSpec((rh, c, c), const3),
            pl.BlockSpec((rh, c, rdv), const3),
            pl.BlockSpec((rh, c, rdk), const3),
            pl.BlockSpec((rh, 1, rdv), const3),
            pl.BlockSpec((1, rw), const2),
            pl.BlockSpec((1, c, gwk), chunk(0)),
            pl.BlockSpec((1, c, gwk), chunk(1)),
            pl.BlockSpec((1, c, gwk), chunk(0)),
            pl.BlockSpec((1, c, gwv), chunk(0)),
            pl.BlockSpec((1, c, gwv), chunk(1)),
            pl.BlockSpec((1, gh, gdv, gdk), state),
            pl.BlockSpec((1, gwv), const2),
        ],
        out_specs=(
            pl.BlockSpec((1, c, rw), chunk(0)),
            pl.BlockSpec((1, rh, rdk, rdv), state),
            pl.BlockSpec((1, c, gwv), chunk(0)),
            pl.BlockSpec((1, gh, gdv, gdk), state),
        ),
        scratch_shapes=[pltpu.VMEM((rh, rdk, rdv), F32), pltpu.VMEM((gh, gdv, gdk), F32)],
        compiler_params=_params("arbitrary", "arbitrary"),
        name="token_mixers",
    )(qkvg, qkvg, qkvg, qkvg, s0_ret, dm, ind, sd, cd, ret_norm_w.reshape(1, rw),
      qk, qk, la, vg, vg, s0_gla_t, gla_norm_w.reshape(1, gwv))


def _merge_kernel(br_ref, bg_ref, h_ref, wr_ref, wg_ref, wm0_ref, wm1_ref, bm0_ref, bm1_ref, o_ref):
    h = h_ref[...]
    g0 = _sigmoid(_dot(h, wm0_ref[...]) + bm0_ref[...])
    g1 = _sigmoid(_dot(h, wm1_ref[...]) + bm1_ref[...])
    y = g0 * _dot(br_ref[...], wr_ref[...]) + g1 * _dot(bg_ref[...], wg_ref[...])
    o_ref[...] = y.astype(o_ref.dtype)


def _merge(o_ret, o_gla, h2d, w_branch, w_merge, layer, b_merge, *, tm):
    n, d = h2d.shape
    wdt = o_ret.shape[1]
    tn = _tile(d, 512, LANES)
    nj = d // tn
    b2 = b_merge.reshape(1, 2 * d)
    return pl.pallas_call(
        _merge_kernel,
        out_shape=jax.ShapeDtypeStruct((n, d), BF16),
        grid=(n // tm, nj),
        in_specs=[
            pl.BlockSpec((tm, wdt), lambda i, j: (i, 0)),
            pl.BlockSpec((tm, wdt), lambda i, j: (i, 0)),
            pl.BlockSpec((tm, d), lambda i, j: (i, 0)),
            pl.BlockSpec((None, wdt, tn), lambda i, j: (2 * layer, 0, j)),
            pl.BlockSpec((None, wdt, tn), lambda i, j: (2 * layer + 1, 0, j)),
            pl.BlockSpec((None, d, tn), lambda i, j: (layer, 0, j)),
            pl.BlockSpec((None, d, tn), lambda i, j: (layer, 0, j + nj)),
            pl.BlockSpec((1, tn), lambda i, j: (0, j)),
            pl.BlockSpec((1, tn), lambda i, j: (0, j + nj)),
        ],
        out_specs=pl.BlockSpec((tm, tn), lambda i, j: (i, j)),
        compiler_params=_params("arbitrary", "arbitrary"),
        name="branch_merge",
    )(o_ret, o_gla, h2d, w_branch, w_branch, w_merge, w_merge, b2, b2)


def _pack_bf16_pairs(h):
    half = h.shape[-1] // 2
    a = lax.bitcast_convert_type(h[:, :half].astype(BF16).astype(F32), U32)
    b = lax.bitcast_convert_type(h[:, half:].astype(BF16).astype(F32), U32)
    return a | (b >> 16)


def _unpack_bf16_pairs(w, dtype):
    a = lax.bitcast_convert_type(w & jnp.uint32(0xFFFF0000), F32)
    b = lax.bitcast_convert_type(w << 16, F32)
    return jnp.concatenate([a, b], axis=-1).astype(dtype)


def _route(logits, n_groups, per_group):
    lane = lax.broadcasted_iota(I32, logits.shape, 1).astype(F32)
    neg = jnp.float32(-jnp.inf)

    def first_max(mask):
        m = jnp.max(jnp.where(mask, logits, neg), axis=-1, keepdims=True)
        idx = jnp.min(jnp.where(mask & (logits == m), lane, float(LANES)), axis=-1, keepdims=True)
        return m, idx

    gmask = lane < n_groups
    gmax, gidx = first_max(gmask)
    p_group = 1.0 / jnp.sum(jnp.where(gmask, jnp.exp(logits - gmax), 0.0), axis=-1, keepdims=True)
    lo = n_groups + gidx * per_group
    emask = (lane >= lo) & (lane < lo + per_group)
    m1, i1 = first_max(emask)
    m2, i2 = first_max(emask & (lane != i1))
    w1 = 1.0 / (1.0 + jnp.exp(m2 - m1))
    w2 = 1.0 - w1
    return jnp.where(lane == 0.0, i1 - n_groups, jnp.where(lane == 1.0, i2 - n_groups, jnp.where(
        lane == 2.0, p_group * w1, jnp.where(lane == 3.0, p_group * w2, 0.0))))


def _with_ranks(rt, carry_ref):
    tr = rt.shape[0]
    lane = lax.broadcasted_iota(I32, rt.shape, 1).astype(F32)
    a1 = lane == rt[:, 0:1]
    a2 = lane == rt[:, 1:2]
    hit = jnp.where(a1 | a2, 1.0, 0.0)
    ri = lax.broadcasted_iota(I32, (tr, tr), 0)
    cj = lax.broadcasted_iota(I32, (tr, tr), 1)
    before = _dot(jnp.where(ri > cj, 1.0, 0.0).astype(BF16), hit.astype(BF16)) + carry_ref[...]
    k1 = jnp.sum(jnp.where(a1, before, 0.0), axis=-1, keepdims=True)
    k2 = jnp.sum(jnp.where(a2, before, 0.0), axis=-1, keepdims=True)
    carry_ref[...] += jnp.sum(hit, axis=0, keepdims=True)
    return jnp.where(lane == 4.0, k1, jnp.where(lane == 5.0, k2, rt))


def _out_proj_kernel(m_ref, x_ref, g1_ref, sc_ref, sh_ref, nw_ref, wo_ref, wr_ref, br_ref,
                     x1_ref, hp_ref, rt_ref, cnt_ref, carry_ref, *, n_groups, per_group):
    bs, tt, d = x_ref.shape

    @pl.when((pl.program_id(0) == 0) & (pl.program_id(1) == 0))
    def _():
        carry_ref[...] = jnp.zeros_like(carry_ref)

    y = _dot(m_ref[...].reshape(bs * tt, d), wo_ref[...]).reshape(bs, tt, d)
    x1 = x_ref[...] + g1_ref[...] * y
    x1_ref[...] = x1
    hn = x1 * lax.rsqrt(jnp.mean(x1 * x1, axis=-1, keepdims=True) + EPS) * nw_ref[...]
    h = (hn * (1.0 + sc_ref[...]) + sh_ref[...]).reshape(bs * tt, d)
    _store_token_major(hp_ref, _pack_bf16_pairs(h))
    h_hi, h_lo = _split_bf16(h)
    p = _dot(h_hi, wr_ref[...])
    logits = p[:, :LANES] + p[:, LANES:] + _dot(h_lo, wr_ref[:, :LANES]) + br_ref[...]
    rt = _with_ranks(_route(logits, n_groups, per_group), carry_ref)
    rt_ref[...] = rt.reshape(bs, tt, LANES)
    cnt_ref[...] = carry_ref[...]


def _out_proj(merged, x, g1, sc2, sh2, norm_w, w_o, layer, wr, b_r, *, n_groups, per_group):
    s, t, d = x.shape
    bs, tt = _seq_tiles(s, t, ROWS_OUT_PROJ)
    chunks = d // 2 // LANES
    tok = lambda i, j: (i, j, 0)
    seq = lambda i, j: (i, 0, 0)
    const = lambda i, j: (0, 0)
    return pl.pallas_call(
        functools.partial(_out_proj_kernel, n_groups=n_groups, per_group=per_group),
        out_shape=(
            jax.ShapeDtypeStruct((s, t, d), F32),
            jax.ShapeDtypeStruct((s * t * chunks, LANES), U32),
            jax.ShapeDtypeStruct((s, t, LANES), F32),
            jax.ShapeDtypeStruct((1, LANES), F32),
        ),
        grid=(s // bs, t // tt),
        in_specs=[
            pl.BlockSpec((bs, tt, d), tok),
            pl.BlockSpec((bs, tt, d), tok),
            pl.BlockSpec((bs, 1, d), seq),
            pl.BlockSpec((bs, 1, d), seq),
            pl.BlockSpec((bs, 1, d), seq),
            pl.BlockSpec((1, d), const),
            pl.BlockSpec((None, d, d), lambda i, j: (layer, 0, 0), pipeline_mode=pl.Buffered(1)),
            pl.BlockSpec((d, 2 * LANES), const, pipeline_mode=pl.Buffered(1)),
            pl.BlockSpec((1, LANES), const),
        ],
        out_specs=(
            pl.BlockSpec((bs, tt, d), tok),
            pl.BlockSpec((bs * tt * chunks, LANES), lambda i, j: (i * (t // tt) + j, 0)),
            pl.BlockSpec((bs, tt, LANES), tok),
            pl.BlockSpec((1, LANES), const),
        ),
        scratch_shapes=[pltpu.VMEM((1, LANES), F32)],
        compiler_params=_params("arbitrary", "arbitrary"),
        name="out_proj_router",
    )(merged, x, g1, sc2, sh2, norm_w.reshape(1, d), w_o, wr, b_r)


def _invert_kernel(pos_ref, pad_lo_ref, pad_hi_ref, src_ref, *, n_tokens, n_exp):
    def clear(j, carry):
        src_ref[j] = 0
        return carry

    for e in range(n_exp):
        lax.fori_loop(pad_lo_ref[e], pad_hi_ref[e], clear, 0)

    def put(t, carry):
        for kk in range(TOP_K):
            src_ref[pos_ref[TOP_K * t + kk]] = t
        return carry

    lax.fori_loop(0, n_tokens, put, 0, unroll=INVERT_UNROLL)


def _invert(pos, pad_lo, pad_hi, n_rows):
    n_tokens = pos.shape[0] // TOP_K
    return pl.pallas_call(
        functools.partial(_invert_kernel, n_tokens=n_tokens, n_exp=pad_lo.shape[0]),
        out_shape=jax.ShapeDtypeStruct((n_rows,), I32),
        grid_spec=pltpu.PrefetchScalarGridSpec(
            num_scalar_prefetch=3,
            grid=(1,),
            in_specs=[],
            out_specs=pl.BlockSpec(memory_space=pltpu.SMEM),
        ),
        compiler_params=_params("arbitrary"),
        name="moe_invert",
    )(pos, pad_lo, pad_hi)


def _store_token_major(ref, value):
    chunks = value.shape[1] // LANES
    for c in range(chunks):
        ref[pl.ds(c, value.shape[0], stride=chunks), :] = value[:, c * LANES:(c + 1) * LANES]


def _load_token_major(ref, rows, chunks):
    return jnp.concatenate([ref[pl.ds(c, rows, stride=chunks), :] for c in range(chunks)], axis=-1)


def _token_copy(src_ref, src_tok, dst_ref, dst_tok, chunks, sem):
    def at(ref, tok):
        start = tok * chunks
        if not isinstance(start, int):
            start = pl.multiple_of(start, chunks)
        return ref.at[pl.ds(start, chunks)]

    return pltpu.make_async_copy(at(src_ref, src_tok), at(dst_ref, dst_tok), sem)


def _ffn_kernel(te_ref, valid_ref, src_ref, hp_ref, wg_ref, wu_ref, wd_ref, ys_ref, xbuf, sems, *, tf, chunks):
    del te_ref
    i = pl.program_id(0)
    last = pl.num_programs(0) - 1
    slot = i % 2
    nxt = jnp.where(i < last, i + 1, 0)

    def start(tile, slt, r):
        _token_copy(hp_ref, src_ref[tile * tf + r], xbuf.at[slt], r, chunks, sems.at[slt]).start()

    def drain(slt):
        def wait(r, carry):
            _token_copy(hp_ref, 0, xbuf.at[slt], 0, chunks, sems.at[slt]).wait()
            return carry

        lax.fori_loop(0, tf, wait, 0, unroll=GATHER_UNROLL)

    @pl.when(i == 0)
    def _():
        def issue(r, carry):
            start(0, 0, r)
            return carry

        lax.fori_loop(0, tf, issue, 0, unroll=GATHER_UNROLL)

    def gather_next(slot):
        for r in range(tf):
            start(nxt, 1 - slot, r)

    def step(slot, fused_issue):
        if not fused_issue:
            gather_next(slot)
        drain(slot)

        @pl.when(valid_ref[i] != 0)
        def _():
            x = _unpack_bf16_pairs(_load_token_major(xbuf.at[slot], tf, chunks), BF16)
            if fused_issue:
                gather_next(slot)
            act = (_silu(_dot(x, wg_ref[...])) * _dot(x, wu_ref[...])).astype(BF16)
            _store_token_major(ys_ref, _pack_bf16_pairs(_dot(act, wd_ref[...])))

        @pl.when(valid_ref[i] == 0)
        def _():
            if fused_issue:
                gather_next(slot)
            ys_ref[...] = jnp.zeros_like(ys_ref)

        @pl.when(i == last)
        def _():
            drain(1 - slot)

    pl.when(slot == 0)(functools.partial(step, 0, True))
    pl.when(slot == 1)(functools.partial(step, 1, False))


def _ffn(tile_expert, tile_valid, src, hp, wg, wu, wd, *, tf):
    n_rows = src.shape[0]
    _, d, f = wg.shape
    chunks = d // 2 // LANES
    return pl.pallas_call(
        functools.partial(_ffn_kernel, tf=tf, chunks=chunks),
        out_shape=jax.ShapeDtypeStruct((n_rows * chunks, LANES), U32),
        grid_spec=pltpu.PrefetchScalarGridSpec(
            num_scalar_prefetch=3,
            grid=(n_rows // tf,),
            in_specs=[
                pl.BlockSpec(memory_space=pl.ANY),
                pl.BlockSpec((None, d, f), lambda i, te, tv, sr: (te[i], 0, 0)),
                pl.BlockSpec((None, d, f), lambda i, te, tv, sr: (te[i], 0, 0)),
                pl.BlockSpec((None, f, d), lambda i, te, tv, sr: (te[i], 0, 0)),
            ],
            out_specs=pl.BlockSpec((tf * chunks, LANES), lambda i, te, tv, sr: (i, 0)),
            scratch_shapes=[pltpu.VMEM((2, tf * chunks, LANES), hp.dtype), pltpu.SemaphoreType.DMA((2,))],
        ),
        compiler_params=_params("arbitrary"),
        name="moe_ffn",
    )(tile_expert, tile_valid, src, hp, wg, wu, wd)


def _combine_kernel(pos_ref, x_ref, rt_ref, g2_ref, nw_ref, sc_ref, sh_ref, ys_ref, *rest, last):
    if last:
        o_ref, ybuf, sems = rest
    else:
        o_ref, h_ref, ybuf, sems = rest
    bs, tt, d = x_ref.shape
    rows = bs * tt
    chunks = d // 2 // LANES
    step = pl.program_id(0) * pl.num_programs(1) + pl.program_id(1)
    n_steps = pl.num_programs(0) * pl.num_programs(1)

    def start(stp, slt, r, kk):
        _token_copy(ys_ref, pos_ref[TOP_K * (stp * rows + r) + kk], ybuf.at[slt, kk], r, chunks,
                    sems.at[slt]).start()

    @pl.when(step == 0)
    def _():
        def issue(r, carry):
            for kk in range(TOP_K):
                start(0, 0, r, kk)
            return carry

        lax.fori_loop(0, rows, issue, 0, unroll=GATHER_UNROLL)

    slot = step % 2

    @pl.when(step + 1 < n_steps)
    def _():
        for r in range(rows):
            for kk in range(TOP_K):
                start(step + 1, 1 - slot, r, kk)

    def wait(r, carry):
        for kk in range(TOP_K):
            _token_copy(ys_ref, 0, ybuf.at[slot, kk], 0, chunks, sems.at[slot]).wait()
        return carry

    lax.fori_loop(0, rows, wait, 0, unroll=GATHER_UNROLL)

    rt = rt_ref[...].reshape(rows, LANES)
    y = (rt[:, 2:3] * _unpack_bf16_pairs(_load_token_major(ybuf.at[slot, 0], rows, chunks), F32)
         + rt[:, 3:4] * _unpack_bf16_pairs(_load_token_major(ybuf.at[slot, 1], rows, chunks), F32))
    xn = x_ref[...] + g2_ref[...] * y.reshape(bs, tt, d)
    normed = xn * lax.rsqrt(jnp.mean(xn * xn, axis=-1, keepdims=True) + EPS) * nw_ref[...]
    if last:
        o_ref[...] = normed
    else:
        o_ref[...] = xn
        h_ref[...] = (normed * (1.0 + sc_ref[...]) + sh_ref[...]).astype(h_ref.dtype)


def _combine(pos, x1, route, g2, norm_w, sc, sh, ys, *, last):
    s, t, d = x1.shape
    bs, tt = _seq_tiles(s, t, ROWS_COMBINE)
    tok = lambda i, j, p: (i, j, 0)
    seq = lambda i, j, p: (i, 0, 0)
    x_shape = jax.ShapeDtypeStruct((s, t, d), F32)
    x_spec = pl.BlockSpec((bs, tt, d), tok)
    return pl.pallas_call(
        functools.partial(_combine_kernel, last=last),
        out_shape=x_shape if last else (x_shape, jax.ShapeDtypeStruct((s, t, d), BF16)),
        grid_spec=pltpu.PrefetchScalarGridSpec(
            num_scalar_prefetch=1,
            grid=(s // bs, t // tt),
            in_specs=[
                x_spec,
                pl.BlockSpec((bs, tt, LANES), tok),
                pl.BlockSpec((bs, 1, d), seq),
                pl.BlockSpec((1, d), lambda i, j, p: (0, 0)),
                pl.BlockSpec((bs, 1, d), seq),
                pl.BlockSpec((bs, 1, d), seq),
                pl.BlockSpec(memory_space=pl.ANY),
            ],
            out_specs=x_spec if last else (x_spec, x_spec),
            scratch_shapes=[pltpu.VMEM((2, TOP_K, bs * tt * (d // 2 // LANES), LANES), U32),
                            pltpu.SemaphoreType.DMA((2,))],
        ),
        compiler_params=_params("arbitrary", "arbitrary"),
        name="moe_combine",
    )(pos, x1, route, g2, norm_w.reshape(1, d), sc, sh, ys)


def _moe(x1, hp, route, counts, g2, wg, wu, wd, layer, n_exp, norm_w, sc, sh, *, last):
    s, t, d = x1.shape
    n = s * t
    route2d = route.reshape(n, LANES)
    tf = 2 * SUBLANES
    while tf < ROWS_FFN and tf < 2 * TOP_K * n // n_exp:
        tf *= 2
    n_tiles = (TOP_K * n) // tf + n_exp
    cnt = counts[0, :n_exp].astype(I32)
    padded = ((cnt + tf - 1) // tf) * tf
    ends = jnp.cumsum(padded)
    starts = ends - padded
    expert = route2d[:, :TOP_K].astype(I32)
    rank = route2d[:, 4:4 + TOP_K].astype(I32)
    start_of = jnp.sum(jnp.where(expert[..., None] == jnp.arange(n_exp, dtype=I32), starts, 0), axis=-1)
    pos = (start_of + rank).reshape(TOP_K * n)
    tile_start = jnp.arange(n_tiles, dtype=I32) * tf
    tile_valid = (tile_start < ends[-1]).astype(I32)
    tile_expert = jnp.minimum(jnp.sum((ends[None, :] <= tile_start[:, None]).astype(I32), axis=1), n_exp - 1)
    last_used = jnp.max(jnp.where(tile_valid != 0, tile_expert, 0))
    tile_expert = jnp.where(tile_valid != 0, tile_expert, last_used)
    pad_hi = jnp.where(jnp.arange(n_exp) == n_exp - 1, n_tiles * tf, ends).astype(I32)
    src = _invert(pos, starts + cnt, pad_hi, n_tiles * tf)
    ys = _ffn(tile_expert + layer * n_exp, tile_valid, src, hp, wg, wu, wd, tf=tf)
    return _combine(pos, x1, route, g2, norm_w, sc, sh, ys, last=last)


def _rotary_tables(pos0, t, dk, reps):
    half = dk // 2
    inv = ROPE_BASE ** (-jnp.arange(half, dtype=F32) / half)
    ang = (pos0 + jnp.arange(t)).astype(F32)[:, None] * inv[None, :]
    cos = jnp.cos(ang)
    sin = jnp.sin(ang)
    cos2 = jnp.concatenate([cos, cos], axis=-1)
    sin2 = jnp.concatenate([-sin, sin], axis=-1)
    return jnp.tile(cos2, (reps, 1)), jnp.tile(sin2, (reps, 1))


def _trunk(x, mods, pos0, s_ret_in, s_gla_in_t, big, wts, dims):
    s, t, d = x.shape
    n = s * t
    depth = len(wts)
    rh, rdk, rdv, gh, gdk, gdv = dims["ret_heads"], dims["ret_dk"], dims["ret_dv"], dims["gla_heads"], dims["gla_dk"], dims["gla_dv"]
    bs, tt = _seq_tiles(s, t, ROWS_PROJ)
    tp = bs * tt
    tm = _tile(n, ROWS_MATMUL, 2 * SUBLANES)
    cos, sin = _rotary_tables(pos0, t, rdk, bs)
    ret_states, gla_states = [], []
    h = _norm(x, wts[0]["norm_mix"], mods[1][0], mods[0][0])
    for l in range(depth):
        w = wts[l]
        sh1, sc1, g1, sh2, sc2, g2 = (m[l] for m in mods)
        h2d = h.reshape(n, d)
        qkvg = _proj_ret(h2d, big["w_ret"], l, cos, sin, heads=rh, dk=rdk, tm=tp)
        qk, vg, la = _proj_gla(h2d, big["w_gla"], big["w_code"], l, w["wup_hi"], w["wup_lo"], w["b_gla"],
                               hdk=gh * gdk, hdv=gh * gdv, dk=gdk, tm=tp)
        o_ret, s_r, o_gla, s_g = _mixers(qkvg.reshape(s, t, -1), qk.reshape(s, t, -1), vg.reshape(s, t, -1),
                                         la.reshape(s, t, -1), s_ret_in[l], s_gla_in_t[l], w["ret_norm"],
                                         w["gla_norm"], ret=(rh, rdk, rdv), gla=(gh, gdk, gdv))
        merged = _merge(o_ret.reshape(n, -1), o_gla.reshape(n, -1), h2d, big["w_branch"], big["w_merge"], l,
                        w["b_merge"], tm=tm)
        x1, hp, route, counts = _out_proj(merged.reshape(s, t, d), x, g1, sc2, sh2, w["norm_ffn"], big["w_o"], l,
                                  w["w_router"], w["b_r"], n_groups=dims["n_groups"], per_group=dims["per_group"])
        n_exp = dims["n_groups"] * dims["per_group"]
        experts = (big["w_exp_gate"], big["w_exp_up"], big["w_exp_down"], l, n_exp)
        if l == depth - 1:
            x = _moe(x1, hp, route, counts, g2, *experts, w["final_norm"], sc1, sh1, last=True)
        else:
            x, h = _moe(x1, hp, route, counts, g2, *experts, wts[l + 1]["norm_mix"], mods[1][l + 1], mods[0][l + 1], last=False)
        ret_states.append(s_r)
        gla_states.append(jnp.swapaxes(s_g, -1, -2))
    return x, jnp.stack(ret_states), jnp.stack(gla_states)


def kernel(x_prompt, x_sample, state_ret, state_gla, c_prompt, c_sample, w_ada, b_ada, norm_mix_w, norm_ffn_w, w_in, w_gla_up, b_gla, ret_norm_w, gla_norm_w, w_branch, w_merge, b_merge, w_o, w_router_group, b_router_group, w_router_expert, b_router_expert, w_exp_gate, w_exp_up, w_exp_down, final_norm_w):
    depth, d, _ = w_in.shape
    _, _, rh, rdk, rdv = state_ret.shape
    _, _, gh, gdk, gdv = state_gla.shape
    low_rank = w_gla_up.shape[1]
    n_groups = w_router_group.shape[-1]
    n_exp = w_router_expert.shape[-1]
    assert low_rank <= LANES and n_groups + n_exp <= LANES and n_exp % n_groups == 0
    dims = dict(ret_heads=rh, ret_dk=rdk, ret_dv=rdv, gla_heads=gh, gla_dk=gdk, gla_dv=gdv,
                n_groups=n_groups, per_group=n_exp // n_groups)

    n_ret = 2 * rh * rdk + 2 * rh * rdv
    n_gla = 2 * gh * gdk + 2 * gh * gdv
    assert w_in.shape[-1] == n_ret + n_gla + low_rank
    f = w_exp_gate.shape[-1]
    big = dict(
        w_ret=_to_bf16(w_in, 0, n_ret),
        w_gla=_to_bf16(w_in, n_ret, n_gla),
        w_code=_to_bf16(w_in, n_ret + n_gla, LANES, valid_cols=low_rank),
        w_branch=_to_bf16(w_branch.reshape(depth, -1, d)).reshape(depth * w_branch.shape[1], -1, d),
        w_merge=_to_bf16(w_merge),
        w_o=_to_bf16(w_o),
        w_exp_gate=_to_bf16(w_exp_gate.reshape(depth, n_exp * d, f)).reshape(depth * n_exp, d, f),
        w_exp_up=_to_bf16(w_exp_up.reshape(depth, n_exp * d, f)).reshape(depth * n_exp, d, f),
        w_exp_down=_to_bf16(w_exp_down.reshape(depth, n_exp * f, d)).reshape(depth * n_exp, f, d),
    )
    w_router = jnp.concatenate([w_router_group, w_router_expert], axis=-1)
    w_router = jnp.pad(w_router, ((0, 0), (0, 0), (0, LANES - n_groups - n_exp)))
    b_router = jnp.pad(jnp.concatenate([b_router_group, b_router_expert], axis=-1), ((0, 0), (0, LANES - n_groups - n_exp)))
    wup = jnp.pad(w_gla_up, ((0, 0), (0, LANES - low_rank), (0, 0)))
    wts = []
    for l in range(depth):
        wr_hi, wr_lo = _split_bf16(w_router[l])
        wup_hi, wup_lo = _split_bf16(wup[l])
        wts.append(dict(
            norm_mix=norm_mix_w[l], norm_ffn=norm_ffn_w[l],
            wup_hi=wup_hi, wup_lo=wup_lo, b_gla=b_gla[l].reshape(1, -1),
            ret_norm=ret_norm_w[l], gla_norm=gla_norm_w[l], b_merge=b_merge[l],
            w_router=jnp.concatenate([wr_hi, wr_lo], axis=-1), b_r=b_router[l].reshape(1, LANES),
            final_norm=final_norm_w,
        ))

    nb, ns = c_prompt.shape[0], c_sample.shape[0]
    mod = _ada(jnp.concatenate([c_prompt, c_sample], axis=0), w_ada, b_ada)
    mod = mod.reshape(depth, nb + ns, 6, d)
    mods_p = [mod[:, :nb, i][:, :, None, :] for i in range(6)]
    mods_s = [mod[:, nb:, i][:, :, None, :] for i in range(6)]

    zero_ret = jnp.zeros((depth, nb, rh, rdk, rdv), F32)
    zero_gla_t = jnp.zeros((depth, nb, gh, gdv, gdk), F32)
    y_p, ret_p, gla_p = _trunk(x_prompt, mods_p, 0, zero_ret, zero_gla_t, big, wts, dims)
    y_s, ret_s, gla_s = _trunk(x_sample, mods_s, PAST_LEN, state_ret.astype(F32),
                               jnp.swapaxes(state_gla.astype(F32), -1, -2), big, wts, dims)
    return (y_p, y_s, ret_p, gla_p, ret_s, gla_s)
```

```python
import functools

import jax
import jax.numpy as jnp
from jax import lax
from jax.experimental import pallas as pl
from jax.experimental.pallas import tpu as pltpu

F32 = jnp.float32
BF16 = jnp.bfloat16
U32 = jnp.uint32
I32 = jnp.int32

PAST_LEN = 4096
GLA_TAU = 16.0
ROPE_BASE = 10000.0
EPS = 1e-6
RET_DECAY_LOG2_BASE = -5.0
TOP_K = 2
LOG2_E = 1.4426950408889634

LANES = 128
SUBLANES = 8
V7X_VMEM_BYTES = 64 * 1024 * 1024
VMEM_LIMIT = V7X_VMEM_BYTES - 8 * 1024 * 1024

ROWS_MATMUL = 1024
ROWS_PROJ = 512
ROWS_OUT_PROJ = 512
ROWS_NORM = 512
ROWS_COMBINE = 256
GATHER_UNROLL = 8
INVERT_UNROLL = 16
CAST_BLOCK_COLS = 1024
CAST_BLOCK_ELEMS = 2 * 1024 * 1024
ROWS_FFN = 256
CHUNK_MIXERS = 256


def _params(*sem):
    return pltpu.CompilerParams(dimension_semantics=sem, vmem_limit_bytes=VMEM_LIMIT)


def _tile(n, pref, align):
    if n <= pref:
        return n
    t = (pref // align) * align
    while t >= align:
        if n % t == 0:
            return t
        t -= align
    return n


def _seq_tiles(s, t, rows):
    if t >= rows:
        return 1, _tile(t, rows, 2 * SUBLANES)
    return _tile(s, max(rows // t, 1), 1), t


def _dot(a, b):
    return jnp.dot(a, b, preferred_element_type=F32)


def _dot_nt(a, b):
    return lax.dot_general(a, b, (((1,), (1,)), ((), ())), preferred_element_type=F32)


def _dot_tn(a, b):
    return lax.dot_general(a, b, (((0,), (0,)), ((), ())), preferred_element_type=F32)


def _sigmoid(x):
    return 0.5 * jnp.tanh(0.5 * x) + 0.5


def _silu(x):
    return x * _sigmoid(x)


def _log_sigmoid(z):
    return jnp.minimum(z, 0.0) - jnp.log(1.0 + jnp.exp(-jnp.abs(z)))


def _split_bf16(x):
    hi = x.astype(BF16)
    lo = (x - hi.astype(F32)).astype(BF16)
    return hi, lo


def _dot_split(a, w_hi, w_lo):
    a_hi, a_lo = _split_bf16(a)
    return _dot(a_hi, w_hi) + _dot(a_lo, w_hi) + _dot(a_hi, w_lo)


def _ada_kernel(c_ref, w_ref, b_ref, o_ref):
    a = _silu(c_ref[...]).astype(BF16)
    o_ref[...] = _dot(a, w_ref[...].astype(BF16)) + b_ref[...]


def _ada(c_all, w_ada, b_ada):
    depth, d, n6 = w_ada.shape
    r = c_all.shape[0]
    tn = _tile(n6, 1024, LANES)
    return pl.pallas_call(
        _ada_kernel,
        out_shape=jax.ShapeDtypeStruct((depth, r, n6), F32),
        grid=(depth, n6 // tn),
        in_specs=[
            pl.BlockSpec((r, d), lambda l, j: (0, 0)),
            pl.BlockSpec((None, d, tn), lambda l, j: (l, 0, j)),
            pl.BlockSpec((None, 1, tn), lambda l, j: (l, 0, j)),
        ],
        out_specs=pl.BlockSpec((None, r, tn), lambda l, j: (l, 0, j)),
        compiler_params=_params("arbitrary", "arbitrary"),
        name="ada_mod",
    )(c_all, w_ada, b_ada.reshape(depth, 1, n6))


def _norm_kernel(x_ref, w_ref, sc_ref, sh_ref, o_ref):
    x = x_ref[...]
    y = x * lax.rsqrt(jnp.mean(x * x, axis=-1, keepdims=True) + EPS) * w_ref[...]
    o_ref[...] = (y * (1.0 + sc_ref[...]) + sh_ref[...]).astype(o_ref.dtype)


def _norm(x, w, sc, sh):
    s, t, d = x.shape
    bs, tt = _seq_tiles(s, t, ROWS_NORM)
    return pl.pallas_call(
        _norm_kernel,
        out_shape=jax.ShapeDtypeStruct((s, t, d), BF16),
        grid=(s // bs, t // tt),
        in_specs=[
            pl.BlockSpec((bs, tt, d), lambda i, j: (i, j, 0)),
            pl.BlockSpec((1, d), lambda i, j: (0, 0)),
            pl.BlockSpec((bs, 1, d), lambda i, j: (i, 0, 0)),
            pl.BlockSpec((bs, 1, d), lambda i, j: (i, 0, 0)),
        ],
        out_specs=pl.BlockSpec((bs, tt, d), lambda i, j: (i, j, 0)),
        compiler_params=_params("arbitrary", "arbitrary"),
        name="mod_norm",
    )(x, w.reshape(1, d), sc, sh)


def _resident(shape, layer=None):
    if layer is None:
        return pl.BlockSpec(shape, lambda i: (0,) * len(shape), pipeline_mode=pl.Buffered(1))
    return pl.BlockSpec((None,) + tuple(shape), lambda i: (layer,) + (0,) * len(shape), pipeline_mode=pl.Buffered(1))


def _cast_kernel(x_ref, o_ref, *, valid_cols):
    x = x_ref[...]
    if valid_cols is not None:
        x = jnp.where(lax.broadcasted_iota(I32, x.shape, 1) < valid_cols, x, 0.0)
    o_ref[...] = x.astype(o_ref.dtype)


def _to_bf16(w, col0=0, ncols=None, valid_cols=None):
    depth, rows, cols = w.shape
    ncols = cols - col0 if ncols is None else ncols
    tc = LANES
    while tc * 2 <= CAST_BLOCK_COLS and col0 % (tc * 2) == 0 and ncols % (tc * 2) == 0:
        tc *= 2
    assert col0 % tc == 0 and ncols % tc == 0 and (valid_cols is None or ncols == tc)
    tr = _tile(rows, max(CAST_BLOCK_ELEMS // tc, 2 * SUBLANES), 2 * SUBLANES)
    c0 = col0 // tc
    return pl.pallas_call(
        functools.partial(_cast_kernel, valid_cols=valid_cols),
        out_shape=jax.ShapeDtypeStruct((depth, rows, ncols), BF16),
        grid=(depth, rows // tr, ncols // tc),
        in_specs=[pl.BlockSpec((None, tr, tc), lambda l, i, j: (l, i, c0 + j))],
        out_specs=pl.BlockSpec((None, tr, tc), lambda l, i, j: (l, i, j)),
        compiler_params=_params("arbitrary", "arbitrary", "arbitrary"),
        name="cast_bf16",
    )(w)


def _proj_ret_kernel(h_ref, w_ref, cos_ref, sin_ref, o_ref, *, heads, dk, kscale):
    h = h_ref[...]
    tn = heads * dk
    cos = cos_ref[...]
    sin = sin_ref[...]
    for sec, mult in ((0, 1.0), (1, kscale)):
        acc = _dot(h, w_ref[:, sec * tn:(sec + 1) * tn])
        for hd in range(heads):
            a = acc[:, hd * dk:(hd + 1) * dk]
            r = a * cos + pltpu.roll(a, dk // 2, 1) * sin
            o_ref[:, sec * tn + hd * dk:sec * tn + (hd + 1) * dk] = (r * mult).astype(o_ref.dtype)
    o_ref[:, 2 * tn:3 * tn] = _dot(h, w_ref[:, 2 * tn:3 * tn]).astype(o_ref.dtype)
    o_ref[:, 3 * tn:] = _silu(_dot(h, w_ref[:, 3 * tn:])).astype(o_ref.dtype)


def _proj_ret(h2d, w, layer, cos, sin, *, heads, dk, tm):
    n, d = h2d.shape
    tn = heads * dk
    assert w.shape[1:] == (d, 4 * tn) and cos.shape[0] % tm == 0
    nt = cos.shape[0] // tm
    return pl.pallas_call(
        functools.partial(_proj_ret_kernel, heads=heads, dk=dk, kscale=dk ** -0.5),
        out_shape=jax.ShapeDtypeStruct((n, 4 * tn), BF16),
        grid=(n // tm,),
        in_specs=[
            pl.BlockSpec((tm, d), lambda i: (i, 0)),
            _resident((d, 4 * tn), layer),
            pl.BlockSpec((tm, dk), lambda i: (i % nt, 0)),
            pl.BlockSpec((tm, dk), lambda i: (i % nt, 0)),
        ],
        out_specs=pl.BlockSpec((tm, 4 * tn), lambda i: (i, 0)),
        compiler_params=_params("arbitrary"),
        name="proj_ret",
    )(h2d, w, cos, sin)


def _proj_gla_kernel(h_ref, w_ref, wc_ref, wup_hi_ref, wup_lo_ref, bup_ref, qk_ref, vg_ref, la_ref, *, hdk, hdv, qscale):
    h = h_ref[...]
    qk = _dot(h, w_ref[:, :2 * hdk])
    qk_ref[:, :hdk] = qk[:, :hdk] * qscale
    qk_ref[:, hdk:] = qk[:, hdk:]
    vg_ref[:, :hdv] = _dot(h, w_ref[:, 2 * hdk:2 * hdk + hdv]).astype(vg_ref.dtype)
    vg_ref[:, hdv:] = _silu(_dot(h, w_ref[:, 2 * hdk + hdv:])).astype(vg_ref.dtype)
    z = _dot_split(_dot(h, wc_ref[...]), wup_hi_ref[...], wup_lo_ref[...])
    la_ref[...] = _log_sigmoid(z + bup_ref[...]) * (1.0 / GLA_TAU)


def _proj_gla(h2d, w, wc, layer, wup_hi, wup_lo, bup, *, hdk, hdv, dk, tm):
    n, d = h2d.shape
    assert w.shape[1:] == (d, 2 * hdk + 2 * hdv) and wc.shape[1:] == (d, LANES) and wup_hi.shape == (LANES, hdk)
    return pl.pallas_call(
        functools.partial(_proj_gla_kernel, hdk=hdk, hdv=hdv, qscale=dk ** -0.5),
        out_shape=(
            jax.ShapeDtypeStruct((n, 2 * hdk), F32),
            jax.ShapeDtypeStruct((n, 2 * hdv), BF16),
            jax.ShapeDtypeStruct((n, hdk), F32),
        ),
        grid=(n // tm,),
        in_specs=[
            pl.BlockSpec((tm, d), lambda i: (i, 0)),
            _resident(w.shape[1:], layer),
            _resident(wc.shape[1:], layer),
            _resident(wup_hi.shape),
            _resident(wup_lo.shape),
            _resident(bup.shape),
        ],
        out_specs=(
            pl.BlockSpec((tm, 2 * hdk), lambda i: (i, 0)),
            pl.BlockSpec((tm, 2 * hdv), lambda i: (i, 0)),
            pl.BlockSpec((tm, hdk), lambda i: (i, 0)),
        ),
        compiler_params=_params("arbitrary"),
        name="proj_gla",
    )(h2d, w, wc, wup_hi, wup_lo, bup)


def _ret_heads(q_ref, k_ref, v_ref, g_ref, dm_ref, ind_ref, sd_ref, cd_ref, w_ref, o_ref, st_ref, *, heads, dk, dv):
    for hd in range(heads):
        q = q_ref[0, :, hd * dk:(hd + 1) * dk]
        k = k_ref[0, :, hd * dk:(hd + 1) * dk]
        v = v_ref[0, :, hd * dv:(hd + 1) * dv]
        s = st_ref[hd]
        p = (_dot_nt(q, k) * dm_ref[hd]).astype(BF16)
        o = _dot(p, v) + _dot(q, s.astype(BF16)) * ind_ref[hd]
        ks = (k.astype(F32) * sd_ref[hd]).astype(BF16)
        st_ref[hd] = s * cd_ref[hd] + _dot_tn(ks, v)
        oc = o - jnp.mean(o, axis=-1, keepdims=True)
        on = oc * lax.rsqrt(jnp.mean(oc * oc, axis=-1, keepdims=True) + EPS)
        gate = g_ref[0, :, hd * dv:(hd + 1) * dv].astype(F32)
        o_ref[0, :, hd * dv:(hd + 1) * dv] = (gate * (on * w_ref[:, hd * dv:(hd + 1) * dv])).astype(o_ref.dtype)


def _ret_tables(heads, c, dk, dv):
    log_gamma = jnp.log1p(-jnp.exp2(RET_DECAY_LOG2_BASE - jnp.arange(heads, dtype=F32)))
    idx = jnp.arange(c, dtype=F32)
    diff = idx[:, None] - idx[None, :]
    causal = diff >= 0
    dmask = jnp.where(causal[None], jnp.exp(log_gamma[:, None, None] * jnp.where(causal, diff, 0.0)[None]), 0.0)
    inner = jnp.exp(log_gamma[:, None] * (idx + 1.0))
    sdecay = jnp.exp(log_gamma[:, None] * (c - 1.0 - idx))
    cdecay = jnp.exp(log_gamma * c)
    return (dmask,
            jnp.broadcast_to(inner[:, :, None], (heads, c, dv)),
            jnp.broadcast_to(sdecay[:, :, None], (heads, c, dk)),
            jnp.broadcast_to(cdecay[:, None, None], (heads, 1, dv)))


def _gla_heads(q_ref, k_ref, la_ref, v_ref, g_ref, w_ref, o_ref, st_ref, *, heads, dk, dv, c):
    row = lax.broadcasted_iota(I32, (c, dk), 0)
    differ = lax.broadcasted_iota(I32, (c, c), 0) ^ lax.broadcasted_iota(I32, (c, c), 1)
    owner = jnp.full((c, c), -1, I32)
    uppers, signs = [], []
    half = 1
    while half < c:
        owner = owner + (differ >= half).astype(I32)
        up = (row & (2 * half - 1)) >= half
        uppers.append(up)
        signs.append(jnp.where(up, LOG2_E, -LOG2_E))
        half *= 2
    for hd in range(heads):
        q = q_ref[0, :, hd * dk:(hd + 1) * dk]
        k = k_ref[0, :, hd * dk:(hd + 1) * dk]
        v = v_ref[0, :, hd * dv:(hd + 1) * dv]
        b = la_ref[0, :, hd * dk:(hd + 1) * dk]
        sh = 1
        while sh < c:
            b = b + jnp.where(row >= sh, pltpu.roll(b, sh, 0), 0.0)
            sh *= 2
        scores = jnp.where(owner < 0, _dot_nt(q.astype(BF16), k.astype(BF16)), 0.0)
        first = b
        for lvl, (upper, sign) in enumerate(zip(uppers, signs)):
            half = 1 << lvl
            mid = jnp.where(upper, first, pltpu.roll(first, c - half, 0))
            scaled = jnp.where(upper, q, k) * jnp.exp2(jnp.minimum((b - mid) * sign, 0.0))
            ql = jnp.where(upper, scaled, 0.0).astype(BF16)
            kl = jnp.where(upper, 0.0, scaled).astype(BF16)
            scores = jnp.where(owner == lvl, _dot_nt(ql, kl), scores)
            first = jnp.where(upper, pltpu.roll(first, half, 0), first)
        st = st_ref[hd]
        o = _dot(scores.astype(BF16), v) + _dot_nt((q * jnp.exp(b)).astype(BF16), st.astype(BF16))
        b_last = b[c - 1:c, :]
        kd = (k * jnp.exp(b_last - b)).astype(BF16)
        st_ref[hd] = st * jnp.exp(b_last) + _dot_tn(v, kd)
        on = o * lax.rsqrt(jnp.mean(o * o, axis=-1, keepdims=True) + EPS)
        gate = g_ref[0, :, hd * dv:(hd + 1) * dv].astype(F32)
        o_ref[0, :, hd * dv:(hd + 1) * dv] = (gate * (on * w_ref[:, hd * dv:(hd + 1) * dv])).astype(o_ref.dtype)


def _mixers_kernel(rq_ref, rk_ref, rv_ref, rg_ref, rs0_ref, dm_ref, ind_ref, sd_ref, cd_ref, rw_ref,
                   gq_ref, gk_ref, la_ref, gv_ref, gg_ref, gs0_ref, gw_ref,
                   ro_ref, rso_ref, go_ref, gso_ref, rst_ref, gst_ref, *, ret, gla, c):
    ci = pl.program_id(1)

    @pl.when(ci == 0)
    def _():
        rst_ref[...] = rs0_ref[0]
        gst_ref[...] = gs0_ref[0]

    rh, rdk, rdv = ret
    gh, gdk, gdv = gla
    _ret_heads(rq_ref, rk_ref, rv_ref, rg_ref, dm_ref, ind_ref, sd_ref, cd_ref, rw_ref, ro_ref, rst_ref,
               heads=rh, dk=rdk, dv=rdv)
    _gla_heads(gq_ref, gk_ref, la_ref, gv_ref, gg_ref, gw_ref, go_ref, gst_ref, heads=gh, dk=gdk, dv=gdv, c=c)

    @pl.when(ci == pl.num_programs(1) - 1)
    def _():
        rso_ref[0] = rst_ref[...]
        gso_ref[0] = gst_ref[...]


def _mixers(qkvg, qk, vg, la, s0_ret, s0_gla_t, ret_norm_w, gla_norm_w, *, ret, gla):
    s, t, _ = qkvg.shape
    rh, rdk, rdv = ret
    gh, gdk, gdv = gla
    assert rdk == rdv
    c = _tile(t, CHUNK_MIXERS, 2 * SUBLANES)
    assert c & (c - 1) == 0, "chunk length must be a power of two"
    rw = rh * rdk
    gwk, gwv = gh * gdk, gh * gdv
    dm, ind, sd, cd = _ret_tables(rh, c, rdk, rdv)
    const3 = lambda b, i: (0, 0, 0)
    const2 = lambda b, i: (0, 0)
    chunk = lambda col: (lambda b, i: (b, i, col))
    state = lambda b, i: (b, 0, 0, 0)
    return pl.pallas_call(
        functools.partial(_mixers_kernel, ret=ret, gla=gla, c=c),
        out_shape=(
            jax.ShapeDtypeStruct((s, t, rw), BF16), jax.ShapeDtypeStruct(s0_ret.shape, F32),
            jax.ShapeDtypeStruct((s, t, gwv), BF16), jax.ShapeDtypeStruct(s0_gla_t.shape, F32),
        ),
        grid=(s, t // c),
        in_specs=[
            pl.BlockSpec((1, c, rw), chunk(0)),
            pl.BlockSpec((1, c, rw), chunk(1)),
            pl.BlockSpec((1, c, rw), chunk(2)),
            pl.BlockSpec((1, c, rw), chunk(3)),
            pl.BlockSpec((1, rh, rdk, rdv), state),
            pl.BlockSpec((rh, c, c), const3),
            pl.BlockSpec((rh, c, rdv), const3),
            pl.BlockSpec((rh, c, rdk), const3),
            pl.BlockSpec((rh, 1, rdv), const3),
            pl.BlockSpec((1, rw), const2),
            pl.BlockSpec((1, c, gwk), chunk(0)),
            pl.BlockSpec((1, c, gwk), chunk(1)),
            pl.BlockSpec((1, c, gwk), chunk(0)),
            pl.BlockSpec((1, c, gwv), chunk(0)),
            pl.BlockSpec((1, c, gwv), chunk(1)),
            pl.BlockSpec((1, gh, gdv, gdk), state),
            pl.BlockSpec((1, gwv), const2),
        ],
        out_specs=(
            pl.BlockSpec((1, c, rw), chunk(0)),
            pl.BlockSpec((1, rh, rdk, rdv), state),
            pl.BlockSpec((1, c, gwv), chunk(0)),
            pl.BlockSpec((1, gh, gdv, gdk), state),
        ),
        scratch_shapes=[pltpu.VMEM((rh, rdk, rdv), F32), pltpu.VMEM((gh, gdv, gdk), F32)],
        compiler_params=_params("arbitrary", "arbitrary"),
        name="token_mixers",
    )(qkvg, qkvg, qkvg, qkvg, s0_ret, dm, ind, sd, cd, ret_norm_w.reshape(1, rw),
      qk, qk, la, vg, vg, s0_gla_t, gla_norm_w.reshape(1, gwv))


def _merge_kernel(br_ref, bg_ref, h_ref, wr_ref, wg_ref, wm0_ref, wm1_ref, bm0_ref, bm1_ref, o_ref):
    h = h_ref[...]
    g0 = _sigmoid(_dot(h, wm0_ref[...]) + bm0_ref[...])
    g1 = _sigmoid(_dot(h, wm1_ref[...]) + bm1_ref[...])
    y = g0 * _dot(br_ref[...], wr_ref[...]) + g1 * _dot(bg_ref[...], wg_ref[...])
    o_ref[...] = y.astype(o_ref.dtype)


def _merge(o_ret, o_gla, h2d, w_branch, w_merge, layer, b_merge, *, tm):
    n, d = h2d.shape
    wdt = o_ret.shape[1]
    tn = _tile(d, 512, LANES)
    nj = d // tn
    b2 = b_merge.reshape(1, 2 * d)
    return pl.pallas_call(
        _merge_kernel,
        out_shape=jax.ShapeDtypeStruct((n, d), BF16),
        grid=(n // tm, nj),
        in_specs=[
            pl.BlockSpec((tm, wdt), lambda i, j: (i, 0)),
            pl.BlockSpec((tm, wdt), lambda i, j: (i, 0)),
            pl.BlockSpec((tm, d), lambda i, j: (i, 0)),
            pl.BlockSpec((None, wdt, tn), lambda i, j: (2 * layer, 0, j)),
            pl.BlockSpec((None, wdt, tn), lambda i, j: (2 * layer + 1, 0, j)),
            pl.BlockSpec((None, d, tn), lambda i, j: (layer, 0, j)),
            pl.BlockSpec((None, d, tn), lambda i, j: (layer, 0, j + nj)),
            pl.BlockSpec((1, tn), lambda i, j: (0, j)),
            pl.BlockSpec((1, tn), lambda i, j: (0, j + nj)),
        ],
        out_specs=pl.BlockSpec((tm, tn), lambda i, j: (i, j)),
        compiler_params=_params("arbitrary", "arbitrary"),
        name="branch_merge",
    )(o_ret, o_gla, h2d, w_branch, w_branch, w_merge, w_merge, b2, b2)


def _pack_bf16_pairs(h):
    half = h.shape[-1] // 2
    a = lax.bitcast_convert_type(h[:, :half].astype(BF16).astype(F32), U32)
    b = lax.bitcast_convert_type(h[:, half:].astype(BF16).astype(F32), U32)
    return a | (b >> 16)


def _unpack_bf16_pairs(w, dtype):
    a = lax.bitcast_convert_type(w & jnp.uint32(0xFFFF0000), F32)
    b = lax.bitcast_convert_type(w << 16, F32)
    return jnp.concatenate([a, b], axis=-1).astype(dtype)


def _route(logits, n_groups, per_group):
    lane = lax.broadcasted_iota(I32, logits.shape, 1).astype(F32)
    neg = jnp.float32(-jnp.inf)

    def first_max(mask):
        m = jnp.max(jnp.where(mask, logits, neg), axis=-1, keepdims=True)
        idx = jnp.min(jnp.where(mask & (logits == m), lane, float(LANES)), axis=-1, keepdims=True)
        return m, idx

    gmask = lane < n_groups
    gmax, gidx = first_max(gmask)
    p_group = 1.0 / jnp.sum(jnp.where(gmask, jnp.exp(logits - gmax), 0.0), axis=-1, keepdims=True)
    lo = n_groups + gidx * per_group
    emask = (lane >= lo) & (lane < lo + per_group)
    m1, i1 = first_max(emask)
    m2, i2 = first_max(emask & (lane != i1))
    w1 = 1.0 / (1.0 + jnp.exp(m2 - m1))
    w2 = 1.0 - w1
    return jnp.where(lane == 0.0, i1 - n_groups, jnp.where(lane == 1.0, i2 - n_groups, jnp.where(
        lane == 2.0, p_group * w1, jnp.where(lane == 3.0, p_group * w2, 0.0))))


def _with_ranks(rt, carry_ref):
    tr = rt.shape[0]
    lane = lax.broadcasted_iota(I32, rt.shape, 1).astype(F32)
    a1 = lane == rt[:, 0:1]
    a2 = lane == rt[:, 1:2]
    hit = jnp.where(a1 | a2, 1.0, 0.0)
    ri = lax.broadcasted_iota(I32, (tr, tr), 0)
    cj = lax.broadcasted_iota(I32, (tr, tr), 1)
    before = _dot(jnp.where(ri > cj, 1.0, 0.0).astype(BF16), hit.astype(BF16)) + carry_ref[...]
    k1 = jnp.sum(jnp.where(a1, before, 0.0), axis=-1, keepdims=True)
    k2 = jnp.sum(jnp.where(a2, before, 0.0), axis=-1, keepdims=True)
    carry_ref[...] += jnp.sum(hit, axis=0, keepdims=True)
    return jnp.where(lane == 4.0, k1, jnp.where(lane == 5.0, k2, rt))


def _out_proj_kernel(m_ref, x_ref, g1_ref, sc_ref, sh_ref, nw_ref, wo_ref, wr_ref, br_ref,
                     x1_ref, hp_ref, rt_ref, cnt_ref, carry_ref, *, n_groups, per_group):
    bs, tt, d = x_ref.shape

    @pl.when((pl.program_id(0) == 0) & (pl.program_id(1) == 0))
    def _():
        carry_ref[...] = jnp.zeros_like(carry_ref)

    y = _dot(m_ref[...].reshape(bs * tt, d), wo_ref[...]).reshape(bs, tt, d)
    x1 = x_ref[...] + g1_ref[...] * y
    x1_ref[...] = x1
    hn = x1 * lax.rsqrt(jnp.mean(x1 * x1, axis=-1, keepdims=True) + EPS) * nw_ref[...]
    h = (hn * (1.0 + sc_ref[...]) + sh_ref[...]).reshape(bs * tt, d)
    _store_token_major(hp_ref, _pack_bf16_pairs(h))
    h_hi, h_lo = _split_bf16(h)
    p = _dot(h_hi, wr_ref[...])
    logits = p[:, :LANES] + p[:, LANES:] + _dot(h_lo, wr_ref[:, :LANES]) + br_ref[...]
    rt = _with_ranks(_route(logits, n_groups, per_group), carry_ref)
    rt_ref[...] = rt.reshape(bs, tt, LANES)
    cnt_ref[...] = carry_ref[...]


def _out_proj(merged, x, g1, sc2, sh2, norm_w, w_o, layer, wr, b_r, *, n_groups, per_group):
    s, t, d = x.shape
    bs, tt = _seq_tiles(s, t, ROWS_OUT_PROJ)
    chunks = d // 2 // LANES
    tok = lambda i, j: (i, j, 0)
    seq = lambda i, j: (i, 0, 0)
    const = lambda i, j: (0, 0)
    return pl.pallas_call(
        functools.partial(_out_proj_kernel, n_groups=n_groups, per_group=per_group),
        out_shape=(
            jax.ShapeDtypeStruct((s, t, d), F32),
            jax.ShapeDtypeStruct((s * t * chunks, LANES), U32),
            jax.ShapeDtypeStruct((s, t, LANES), F32),
            jax.ShapeDtypeStruct((1, LANES), F32),
        ),
        grid=(s // bs, t // tt),
        in_specs=[
            pl.BlockSpec((bs, tt, d), tok),
            pl.BlockSpec((bs, tt, d), tok),
            pl.BlockSpec((bs, 1, d), seq),
            pl.BlockSpec((bs, 1, d), seq),
            pl.BlockSpec((bs, 1, d), seq),
            pl.BlockSpec((1, d), const),
            pl.BlockSpec((None, d, d), lambda i, j: (layer, 0, 0), pipeline_mode=pl.Buffered(1)),
            pl.BlockSpec((d, 2 * LANES), const, pipeline_mode=pl.Buffered(1)),
            pl.BlockSpec((1, LANES), const),
        ],
        out_specs=(
            pl.BlockSpec((bs, tt, d), tok),
            pl.BlockSpec((bs * tt * chunks, LANES), lambda i, j: (i * (t // tt) + j, 0)),
            pl.BlockSpec((bs, tt, LANES), tok),
            pl.BlockSpec((1, LANES), const),
        ),
        scratch_shapes=[pltpu.VMEM((1, LANES), F32)],
        compiler_params=_params("arbitrary", "arbitrary"),
        name="out_proj_router",
    )(merged, x, g1, sc2, sh2, norm_w.reshape(1, d), w_o, wr, b_r)


def _invert_kernel(pos_ref, pad_lo_ref, pad_hi_ref, src_ref, *, n_tokens, n_exp):
    def clear(j, carry):
        src_ref[j] = 0
        return carry

    for e in range(n_exp):
        lax.fori_loop(pad_lo_ref[e], pad_hi_ref[e], clear, 0)

    def put(t, carry):
        for kk in range(TOP_K):
            src_ref[pos_ref[TOP_K * t + kk]] = t
        return carry

    lax.fori_loop(0, n_tokens, put, 0, unroll=INVERT_UNROLL)


def _invert(pos, pad_lo, pad_hi, n_rows):
    n_tokens = pos.shape[0] // TOP_K
    return pl.pallas_call(
        functools.partial(_invert_kernel, n_tokens=n_tokens, n_exp=pad_lo.shape[0]),
        out_shape=jax.ShapeDtypeStruct((n_rows,), I32),
        grid_spec=pltpu.PrefetchScalarGridSpec(
            num_scalar_prefetch=3,
            grid=(1,),
            in_specs=[],
            out_specs=pl.BlockSpec(memory_space=pltpu.SMEM),
        ),
        compiler_params=_params("arbitrary"),
        name="moe_invert",
    )(pos, pad_lo, pad_hi)


def _store_token_major(ref, value):
    chunks = value.shape[1] // LANES
    for c in range(chunks):
        ref[pl.ds(c, value.shape[0], stride=chunks), :] = value[:, c * LANES:(c + 1) * LANES]


def _load_token_major(ref, rows, chunks):
    return jnp.concatenate([ref[pl.ds(c, rows, stride=chunks), :] for c in range(chunks)], axis=-1)


def _token_copy(src_ref, src_tok, dst_ref, dst_tok, chunks, sem):
    def at(ref, tok):
        start = tok * chunks
        if not isinstance(start, int):
            start = pl.multiple_of(start, chunks)
        return ref.at[pl.ds(start, chunks)]

    return pltpu.make_async_copy(at(src_ref, src_tok), at(dst_ref, dst_tok), sem)


def _ffn_kernel(te_ref, valid_ref, src_ref, hp_ref, wg_ref, wu_ref, wd_ref, ys_ref, xbuf, sems, *, tf, chunks):
    del te_ref
    i = pl.program_id(0)
    last = pl.num_programs(0) - 1
    slot = i % 2
    nxt = jnp.where(i < last, i + 1, 0)

    def start(tile, slt, r):
        _token_copy(hp_ref, src_ref[tile * tf + r], xbuf.at[slt], r, chunks, sems.at[slt]).start()

    def drain(slt):
        def wait(r, carry):
            _token_copy(hp_ref, 0, xbuf.at[slt], 0, chunks, sems.at[slt]).wait()
            return carry

        lax.fori_loop(0, tf, wait, 0, unroll=GATHER_UNROLL)

    @pl.when(i == 0)
    def _():
        def issue(r, carry):
            start(0, 0, r)
            return carry

        lax.fori_loop(0, tf, issue, 0, unroll=GATHER_UNROLL)

    def gather_next(slot):
        for r in range(tf):
            start(nxt, 1 - slot, r)

    def step(slot, fused_issue):
        if not fused_issue:
            gather_next(slot)
        drain(slot)

        @pl.when(valid_ref[i] != 0)
        def _():
            x = _unpack_bf16_pairs(_load_token_major(xbuf.at[slot], tf, chunks), BF16)
            if fused_issue:
                gather_next(slot)
            act = (_silu(_dot(x, wg_ref[...])) * _dot(x, wu_ref[...])).astype(BF16)
            _store_token_major(ys_ref, _pack_bf16_pairs(_dot(act, wd_ref[...])))

        @pl.when(valid_ref[i] == 0)
        def _():
            if fused_issue:
                gather_next(slot)
            ys_ref[...] = jnp.zeros_like(ys_ref)

        @pl.when(i == last)
        def _():
            drain(1 - slot)

    pl.when(slot == 0)(functools.partial(step, 0, True))
    pl.when(slot == 1)(functools.partial(step, 1, False))


def _ffn(tile_expert, tile_valid, src, hp, wg, wu, wd, *, tf):
    n_rows = src.shape[0]
    _, d, f = wg.shape
    chunks = d // 2 // LANES
    return pl.pallas_call(
        functools.partial(_ffn_kernel, tf=tf, chunks=chunks),
        out_shape=jax.ShapeDtypeStruct((n_rows * chunks, LANES), U32),
        grid_spec=pltpu.PrefetchScalarGridSpec(
            num_scalar_prefetch=3,
            grid=(n_rows // tf,),
            in_specs=[
                pl.BlockSpec(memory_space=pl.ANY),
                pl.BlockSpec((None, d, f), lambda i, te, tv, sr: (te[i], 0, 0)),
                pl.BlockSpec((None, d, f), lambda i, te, tv, sr: (te[i], 0, 0)),
                pl.BlockSpec((None, f, d), lambda i, te, tv, sr: (te[i], 0, 0)),
            ],
            out_specs=pl.BlockSpec((tf * chunks, LANES), lambda i, te, tv, sr: (i, 0)),
            scratch_shapes=[pltpu.VMEM((2, tf * chunks, LANES), hp.dtype), pltpu.SemaphoreType.DMA((2,))],
        ),
        compiler_params=_params("arbitrary"),
        name="moe_ffn",
    )(tile_expert, tile_valid, src, hp, wg, wu, wd)


def _combine_kernel(pos_ref, x_ref, rt_ref, g2_ref, nw_ref, sc_ref, sh_ref, ys_ref, *rest, last):
    if last:
        o_ref, ybuf, sems = rest
    else:
        o_ref, h_ref, ybuf, sems = rest
    bs, tt, d = x_ref.shape
    rows = bs * tt
    chunks = d // 2 // LANES
    step = pl.program_id(0) * pl.num_programs(1) + pl.program_id(1)
    n_steps = pl.num_programs(0) * pl.num_programs(1)

    def start(stp, slt, r, kk):
        _token_copy(ys_ref, pos_ref[TOP_K * (stp * rows + r) + kk], ybuf.at[slt, kk], r, chunks,
                    sems.at[slt]).start()

    @pl.when(step == 0)
    def _():
        def issue(r, carry):
            for kk in range(TOP_K):
                start(0, 0, r, kk)
            return carry

        lax.fori_loop(0, rows, issue, 0, unroll=GATHER_UNROLL)

    slot = step % 2

    @pl.when(step + 1 < n_steps)
    def _():
        for r in range(rows):
            for kk in range(TOP_K):
                start(step + 1, 1 - slot, r, kk)

    def wait(r, carry):
        for kk in range(TOP_K):
            _token_copy(ys_ref, 0, ybuf.at[slot, kk], 0, chunks, sems.at[slot]).wait()
        return carry

    lax.fori_loop(0, rows, wait, 0, unroll=GATHER_UNROLL)

    rt = rt_ref[...].reshape(rows, LANES)
    y = (rt[:, 2:3] * _unpack_bf16_pairs(_load_token_major(ybuf.at[slot, 0], rows, chunks), F32)
         + rt[:, 3:4] * _unpack_bf16_pairs(_load_token_major(ybuf.at[slot, 1], rows, chunks), F32))
    xn = x_ref[...] + g2_ref[...] * y.reshape(bs, tt, d)
    normed = xn * lax.rsqrt(jnp.mean(xn * xn, axis=-1, keepdims=True) + EPS) * nw_ref[...]
    if last:
        o_ref[...] = normed
    else:
        o_ref[...] = xn
        h_ref[...] = (normed * (1.0 + sc_ref[...]) + sh_ref[...]).astype(h_ref.dtype)


def _combine(pos, x1, route, g2, norm_w, sc, sh, ys, *, last):
    s, t, d = x1.shape
    bs, tt = _seq_tiles(s, t, ROWS_COMBINE)
    tok = lambda i, j, p: (i, j, 0)
    seq = lambda i, j, p: (i, 0, 0)
    x_shape = jax.ShapeDtypeStruct((s, t, d), F32)
    x_spec = pl.BlockSpec((bs, tt, d), tok)
    return pl.pallas_call(
        functools.partial(_combine_kernel, last=last),
        out_shape=x_shape if last else (x_shape, jax.ShapeDtypeStruct((s, t, d), BF16)),
        grid_spec=pltpu.PrefetchScalarGridSpec(
            num_scalar_prefetch=1,
            grid=(s // bs, t // tt),
            in_specs=[
                x_spec,
                pl.BlockSpec((bs, tt, LANES), tok),
                pl.BlockSpec((bs, 1, d), seq),
                pl.BlockSpec((1, d), lambda i, j, p: (0, 0)),
                pl.BlockSpec((bs, 1, d), seq),
                pl.BlockSpec((bs, 1, d), seq),
                pl.BlockSpec(memory_space=pl.ANY),
            ],
            out_specs=x_spec if last else (x_spec, x_spec),
            scratch_shapes=[pltpu.VMEM((2, TOP_K, bs * tt * (d // 2 // LANES), LANES), U32),
                            pltpu.SemaphoreType.DMA((2,))],
        ),
        compiler_params=_params("arbitrary", "arbitrary"),
        name="moe_combine",
    )(pos, x1, route, g2, norm_w.reshape(1, d), sc, sh, ys)


def _moe(x1, hp, route, counts, g2, wg, wu, wd, layer, n_exp, norm_w, sc, sh, *, last):
    s, t, d = x1.shape
    n = s * t
    route2d = route.reshape(n, LANES)
    tf = 2 * SUBLANES
    while tf < ROWS_FFN and tf < 2 * TOP_K * n // n_exp:
        tf *= 2
    n_tiles = (TOP_K * n) // tf + n_exp
    cnt = counts[0, :n_exp].astype(I32)
    padded = ((cnt + tf - 1) // tf) * tf
    ends = jnp.cumsum(padded)
    starts = ends - padded
    expert = route2d[:, :TOP_K].astype(I32)
    rank = route2d[:, 4:4 + TOP_K].astype(I32)
    start_of = jnp.sum(jnp.where(expert[..., None] == jnp.arange(n_exp, dtype=I32), starts, 0), axis=-1)
    pos = (start_of + rank).reshape(TOP_K * n)
    tile_start = jnp.arange(n_tiles, dtype=I32) * tf
    tile_valid = (tile_start < ends[-1]).astype(I32)
    tile_expert = jnp.minimum(jnp.sum((ends[None, :] <= tile_start[:, None]).astype(I32), axis=1), n_exp - 1)
    last_used = jnp.max(jnp.where(tile_valid != 0, tile_expert, 0))
    tile_expert = jnp.where(tile_valid != 0, tile_expert, last_used)
    pad_hi = jnp.where(jnp.arange(n_exp) == n_exp - 1, n_tiles * tf, ends).astype(I32)
    src = _invert(pos, starts + cnt, pad_hi, n_tiles * tf)
    ys = _ffn(tile_expert + layer * n_exp, tile_valid, src, hp, wg, wu, wd, tf=tf)
    return _combine(pos, x1, route, g2, norm_w, sc, sh, ys, last=last)


def _rotary_tables(pos0, t, dk, reps):
    half = dk // 2
    inv = ROPE_BASE ** (-jnp.arange(half, dtype=F32) / half)
    ang = (pos0 + jnp.arange(t)).astype(F32)[:, None] * inv[None, :]
    cos = jnp.cos(ang)
    sin = jnp.sin(ang)
    cos2 = jnp.concatenate([cos, cos], axis=-1)
    sin2 = jnp.concatenate([-sin, sin], axis=-1)
    return jnp.tile(cos2, (reps, 1)), jnp.tile(sin2, (reps, 1))


def _trunk(x, mods, pos0, s_ret_in, s_gla_in_t, big, wts, dims):
    s, t, d = x.shape
    n = s * t
    depth = len(wts)
    rh, rdk, rdv, gh, gdk, gdv = dims["ret_heads"], dims["ret_dk"], dims["ret_dv"], dims["gla_heads"], dims["gla_dk"], dims["gla_dv"]
    bs, tt = _seq_tiles(s, t, ROWS_PROJ)
    tp = bs * tt
    tm = _tile(n, ROWS_MATMUL, 2 * SUBLANES)
    cos, sin = _rotary_tables(pos0, t, rdk, bs)
    ret_states, gla_states = [], []
    h = _norm(x, wts[0]["norm_mix"], mods[1][0], mods[0][0])
    for l in range(depth):
        w = wts[l]
        sh1, sc1, g1, sh2, sc2, g2 = (m[l] for m in mods)
        h2d = h.reshape(n, d)
        qkvg = _proj_ret(h2d, big["w_ret"], l, cos, sin, heads=rh, dk=rdk, tm=tp)
        qk, vg, la = _proj_gla(h2d, big["w_gla"], big["w_code"], l, w["wup_hi"], w["wup_lo"], w["b_gla"],
                               hdk=gh * gdk, hdv=gh * gdv, dk=gdk, tm=tp)
        o_ret, s_r, o_gla, s_g = _mixers(qkvg.reshape(s, t, -1), qk.reshape(s, t, -1), vg.reshape(s, t, -1),
                                         la.reshape(s, t, -1), s_ret_in[l], s_gla_in_t[l], w["ret_norm"],
                                         w["gla_norm"], ret=(rh, rdk, rdv), gla=(gh, gdk, gdv))
        merged = _merge(o_ret.reshape(n, -1), o_gla.reshape(n, -1), h2d, big["w_branch"], big["w_merge"], l,
                        w["b_merge"], tm=tm)
        x1, hp, route, counts = _out_proj(merged.reshape(s, t, d), x, g1, sc2, sh2, w["norm_ffn"], big["w_o"], l,
                                  w["w_router"], w["b_r"], n_groups=dims["n_groups"], per_group=dims["per_group"])
        n_exp = dims["n_groups"] * dims["per_group"]
        experts = (big["w_exp_gate"], big["w_exp_up"], big["w_exp_down"], l, n_exp)
        if l == depth - 1:
            x = _moe(x1, hp, route, counts, g2, *experts, w["final_norm"], sc1, sh1, last=True)
        else:
            x, h = _moe(x1, hp, route, counts, g2, *experts, wts[l + 1]["norm_mix"], mods[1][l + 1], mods[0][l + 1], last=False)
        ret_states.append(s_r)
        gla_states.append(jnp.swapaxes(s_g, -1, -2))
    return x, jnp.stack(ret_states), jnp.stack(gla_states)


def kernel(x_prompt, x_sample, state_ret, state_gla, c_prompt, c_sample, w_ada, b_ada, norm_mix_w, norm_ffn_w, w_in, w_gla_up, b_gla, ret_norm_w, gla_norm_w, w_branch, w_merge, b_merge, w_o, w_router_group, b_router_group, w_router_expert, b_router_expert, w_exp_gate, w_exp_up, w_exp_down, final_norm_w):
    depth, d, _ = w_in.shape
    _, _, rh, rdk, rdv = state_ret.shape
    _, _, gh, gdk, gdv = state_gla.shape
    low_rank = w_gla_up.shape[1]
    n_groups = w_router_group.shape[-1]
    n_exp = w_router_expert.shape[-1]
    assert low_rank <= LANES and n_groups + n_exp <= LANES and n_exp % n_groups == 0
    dims = dict(ret_heads=rh, ret_dk=rdk, ret_dv=rdv, gla_heads=gh, gla_dk=gdk, gla_dv=gdv,
                n_groups=n_groups, per_group=n_exp // n_groups)

    n_ret = 2 * rh * rdk + 2 * rh * rdv
    n_gla = 2 * gh * gdk + 2 * gh * gdv
    assert w_in.shape[-1] == n_ret + n_gla + low_rank
    f = w_exp_gate.shape[-1]
    big = dict(
        w_ret=_to_bf16(w_in, 0, n_ret),
        w_gla=_to_bf16(w_in, n_ret, n_gla),
        w_code=_to_bf16(w_in, n_ret + n_gla, LANES, valid_cols=low_rank),
        w_branch=_to_bf16(w_branch.reshape(depth, -1, d)).reshape(depth * w_branch.shape[1], -1, d),
        w_merge=_to_bf16(w_merge),
        w_o=_to_bf16(w_o),
        w_exp_gate=_to_bf16(w_exp_gate.reshape(depth, n_exp * d, f)).reshape(depth * n_exp, d, f),
        w_exp_up=_to_bf16(w_exp_up.reshape(depth, n_exp * d, f)).reshape(depth * n_exp, d, f),
        w_exp_down=_to_bf16(w_exp_down.reshape(depth, n_exp * f, d)).reshape(depth * n_exp, f, d),
    )
    w_router = jnp.concatenate([w_router_group, w_router_expert], axis=-1)
    w_router = jnp.pad(w_router, ((0, 0), (0, 0), (0, LANES - n_groups - n_exp)))
    b_router = jnp.pad(jnp.concatenate([b_router_group, b_router_expert], axis=-1), ((0, 0), (0, LANES - n_groups - n_exp)))
    wup = jnp.pad(w_gla_up, ((0, 0), (0, LANES - low_rank), (0, 0)))
    wts = []
    for l in range(depth):
        wr_hi, wr_lo = _split_bf16(w_router[l])
        wup_hi, wup_lo = _split_bf16(wup[l])
        wts.append(dict(
            norm_mix=norm_mix_w[l], norm_ffn=norm_ffn_w[l],
            wup_hi=wup_hi, wup_lo=wup_lo, b_gla=b_gla[l].reshape(1, -1),
            ret_norm=ret_norm_w[l], gla_norm=gla_norm_w[l], b_merge=b_merge[l],
            w_router=jnp.concatenate([wr_hi, wr_lo], axis=-1), b_r=b_router[l].reshape(1, LANES),
            final_norm=final_norm_w,
        ))

    nb, ns = c_prompt.shape[0], c_sample.shape[0]
    mod = _ada(jnp.concatenate([c_prompt, c_sample], axis=0), w_ada, b_ada)
    mod = mod.reshape(depth, nb + ns, 6, d)
    mods_p = [mod[:, :nb, i][:, :, None, :] for i in range(6)]
    mods_s = [mod[:, nb:, i][:, :, None, :] for i in range(6)]

    zero_ret = jnp.zeros((depth, nb, rh, rdk, rdv), F32)
    zero_gla_t = jnp.zeros((depth, nb, gh, gdv, gdk), F32)
    y_p, ret_p, gla_p = _trunk(x_prompt, mods_p, 0, zero_ret, zero_gla_t, big, wts, dims)
    y_s, ret_s, gla_s = _trunk(x_sample, mods_s, PAST_LEN, state_ret.astype(F32),
                               jnp.swapaxes(state_gla.astype(F32), -1, -2), big, wts, dims)
    return (y_p, y_s, ret_p, gla_p, ret_s, gla_s)
```
